```python
import math
import jax
import jax.numpy as jnp
from jax import lax
import numpy as np


D_MODEL = 1024
BATCH = 4
SEQ = 8192
DEPTH = 2

GRID_W = 64
HEAD_DIM = 64
Q_BLOCK = 128
NA_HEADS = 6
NA_ROWS = 8
NA_COLS = 16
GQA_HEADS = 6
GQA_KV_HEADS = 2
ROPE_THETA = 10000.0
DIL_GROUPS = ((128, 1), (512, 4), (2048, 16))
DIL_HEADS_PER_GROUP = 2
DIL_HEADS = DIL_HEADS_PER_GROUP * len(DIL_GROUPS)
DIFF_HEADS = 4
DIFF_V_DIM = 2 * HEAD_DIM
N_BRANCHES = 4
T5_BUCKETS = 32
T5_MAX_DISTANCE = 1024
T5_HEADS = DIL_HEADS + DIFF_HEADS
N_EXPERTS = 16
EC_CAPACITY = 2
EXPERT_FF = 2048
PLE_DIM = 256
DEEPNORM_ALPHA = (2 * DEPTH) ** 0.25
DEEPNORM_BETA = (8 * DEPTH) ** -0.25
NEG_INF = -1e30

NA_W = NA_HEADS * HEAD_DIM
GQA_Q_W = GQA_HEADS * HEAD_DIM
GQA_KV_W = GQA_KV_HEADS * HEAD_DIM
DIL_W = DIL_HEADS * HEAD_DIM
DIL_OUT_W = DIL_HEADS_PER_GROUP * HEAD_DIM
DIFF_QK_W = DIFF_HEADS * 2 * HEAD_DIM
DIFF_V_W = DIFF_HEADS * DIFF_V_DIM
IN_SPLITS = (NA_W, NA_W, NA_W, GQA_Q_W, GQA_KV_W, GQA_KV_W, DIL_W, DIL_W, DIL_W,
             DIFF_QK_W, DIFF_QK_W, DIFF_V_W, N_BRANCHES * D_MODEL)
IN_WIDTH = sum(IN_SPLITS)

kernel_name = 'hybrid_gated_mixers_ec_moe_encoder'


def layer_norm(x, g, b, eps=1e-5):
    xf = x.astype(jnp.float32)
    mu = jnp.mean(xf, axis=-1, keepdims=True)
    xc = xf - mu
    var = jnp.mean(xc * xc, axis=-1, keepdims=True)
    return (xc * lax.rsqrt(var + eps) * g + b).astype(x.dtype)


def rms_norm(x, g, eps=1e-6):
    xf = x.astype(jnp.float32)
    y = xf * lax.rsqrt(jnp.mean(xf * xf, axis=-1, keepdims=True) + eps)
    return (y * g).astype(x.dtype)


def t5_bucket(rel):
    half = T5_BUCKETS // 2
    exact = half // 2
    n = jnp.abs(rel)
    nf = jnp.maximum(n, 1).astype(jnp.float32)
    large = exact + (jnp.log(nf / exact) / math.log(T5_MAX_DISTANCE / exact) * (half - exact)).astype(jnp.int32)
    large = jnp.minimum(large, half - 1)
    return jnp.where(rel > 0, half, 0) + jnp.where(n < exact, n, large)


def _to_blocks(t):
    b, s = t.shape[:2]
    return jnp.swapaxes(t.reshape(b, s // Q_BLOCK, Q_BLOCK, *t.shape[2:]), 0, 1)


def _from_blocks(t):
    n, b, blk = t.shape[:3]
    return jnp.swapaxes(t, 0, 1).reshape(b, n * blk, *t.shape[3:])


def neighborhood_attention(q, k, v, rpb):
    B, S, H, dh = q.shape
    rows = S // GRID_W
    wr = min(NA_ROWS, rows)
    scale = dh ** -0.5
    qg = q.reshape(B, rows, GRID_W, H, dh)
    kg = k.reshape(B, rows, GRID_W, H, dh)
    vg = v.reshape(B, rows, GRID_W, H, dh)
    col = jnp.arange(GRID_W)
    cidx = jnp.clip(col - NA_COLS // 2, 0, GRID_W - NA_COLS)[:, None] + jnp.arange(NA_COLS)[None, :]
    dc = cidx - col[:, None] + (NA_COLS - 1)

    def row_fn(r):
        rs = jnp.clip(r - wr // 2, 0, rows - wr)
        kr = lax.dynamic_slice_in_dim(kg, rs, wr, axis=1)[:, :, cidx]
        vr = lax.dynamic_slice_in_dim(vg, rs, wr, axis=1)[:, :, cidx]
        qr = lax.dynamic_index_in_dim(qg, r, axis=1, keepdims=False)
        dr = rs + jnp.arange(wr) - r + (NA_ROWS - 1)
        bias = rpb[:, dr[None, :, None], dc[:, None, :]].astype(jnp.float32)
        logits = jnp.einsum('bchd,bicjhd->bhcij', qr, kr).astype(jnp.float32) * scale + bias[None]
        pr = jax.nn.softmax(logits.reshape(B, H, GRID_W, wr * NA_COLS), axis=-1)
        pr = pr.reshape(B, H, GRID_W, wr, NA_COLS).astype(v.dtype)
        return jnp.einsum('bhcij,bicjhd->bchd', pr, vr)

    out = lax.map(row_fn, jnp.arange(rows))
    return jnp.swapaxes(out, 0, 1).reshape(B, S, H * dh)


def axial_rope(x, row, col):
    half = HEAD_DIM // 2
    quarter = half // 2
    freqs = ROPE_THETA ** (-jnp.arange(quarter, dtype=jnp.float32) / quarter)

    def rot(xh, pos):
        ang = pos[:, None] * freqs[None, :]
        c = jnp.cos(ang)[None, :, None, :].astype(x.dtype)
        s = jnp.sin(ang)[None, :, None, :].astype(x.dtype)
        x1, x2 = xh[..., :quarter], xh[..., quarter:]
        return jnp.concatenate([x1 * c - x2 * s, x2 * c + x1 * s], axis=-1)

    return jnp.concatenate([rot(x[..., :half], row), rot(x[..., half:], col)], axis=-1)


def gqa_attention(q, k, v):
    B, S, _, dh = q.shape
    rep = GQA_HEADS // GQA_KV_HEADS
    scale = dh ** -0.5
    qb = _to_blocks(q.reshape(B, S, GQA_KV_HEADS, rep, dh))

    def blk(qi):
        logits = jnp.einsum('bqgrd,bkgd->bgrqk', qi, k).astype(jnp.float32) * scale
        pr = jax.nn.softmax(logits, axis=-1).astype(v.dtype)
        return jnp.einsum('bgrqk,bkgd->bqgrd', pr, v)

    return _from_blocks(lax.map(blk, qb)).reshape(B, S, GQA_HEADS * dh)


def dilated_attention(q, k, v, t5_table):
    B, S, _, dh = q.shape
    hpg = DIL_HEADS_PER_GROUP
    scale = dh ** -0.5
    groups = []
    for g, (w, d) in enumerate(DIL_GROUPS):
        n = w // (2 * d)
        off = jnp.arange(-n, n + 1) * d
        hs = slice(g * hpg, (g + 1) * hpg)
        bias = t5_table[t5_bucket(off), hs].T.astype(jnp.float32)
        groups.append((off, k[:, :, hs], v[:, :, hs], bias))

    def blk(args):
        qi, bi = args
        qpos = bi * Q_BLOCK + jnp.arange(Q_BLOCK)
        outs, lses = [], []
        for g, (off, kg, vg, bias) in enumerate(groups):
            kpos = qpos[:, None] + off[None, :]
            valid = (kpos >= 0) & (kpos < S)
            kidx = jnp.clip(kpos, 0, S - 1)
            kk = kg[:, kidx]
            vv = vg[:, kidx]
            qg = qi[:, :, g * hpg:(g + 1) * hpg]
            logits = jnp.einsum('bqhd,bqjhd->bhqj', qg, kk).astype(jnp.float32) * scale + bias[None, :, None, :]
            logits = jnp.where(valid[None, None], logits, NEG_INF)
            lse = jax.nn.logsumexp(logits, axis=-1)
            pr = jnp.exp(logits - lse[..., None]).astype(v.dtype)
            outs.append(jnp.einsum('bhqj,bqjhd->bqhd', pr, vv))
            lses.append(lse)
        wts = jax.nn.softmax(jnp.stack(lses), axis=0)
        wts = jnp.swapaxes(wts, 2, 3)[..., None].astype(v.dtype)
        return jnp.sum(wts * jnp.stack(outs), axis=0)

    out = lax.map(blk, (_to_blocks(q), jnp.arange(S // Q_BLOCK)))
    return _from_blocks(out).reshape(B, S, DIL_OUT_W)


def diff_attention(q, k, v, lam_params, gain, t5_table, lam_init):
    B, S, H = q.shape[:3]
    lp = lam_params.astype(jnp.float32)
    lam = jnp.exp(jnp.sum(lp[0] * lp[1])) - jnp.exp(jnp.sum(lp[2] * lp[3])) + lam_init
    k1, k2 = k[..., 0, :], k[..., 1, :]
    scale = HEAD_DIM ** -0.5
    kpos = jnp.arange(S)

    def blk(args):
        qi, bi = args
        qpos = bi * Q_BLOCK + jnp.arange(Q_BLOCK)
        bias = jnp.moveaxis(t5_table[t5_bucket(kpos[None, :] - qpos[:, None])], -1, 0).astype(jnp.float32)
        l1 = jnp.einsum('bqhd,bkhd->bhqk', qi[..., 0, :], k1).astype(jnp.float32) * scale + bias[None]
        l2 = jnp.einsum('bqhd,bkhd->bhqk', qi[..., 1, :], k2).astype(jnp.float32) * scale + bias[None]
        attn = jax.nn.softmax(l1, axis=-1) - lam * jax.nn.softmax(l2, axis=-1)
        return jnp.einsum('bhqk,bkhe->bqhe', attn.astype(v.dtype), v)

    out = _from_blocks(lax.map(blk, (_to_blocks(q), jnp.arange(S // Q_BLOCK))))
    out = rms_norm(out, gain) * (1.0 - lam_init)
    return out.reshape(B, S, H * DIFF_V_DIM)


def token_mixer(x, w_in, w_na, w_gqa, w_dil, w_diff, w_o, rpb, qk_gain, lam_params, diff_gain, t5_bias, lam_init):
    B, S, D = x.shape
    h = x @ w_in
    offs = np.cumsum(IN_SPLITS)[:-1].tolist()
    qa, ka, va, qb, kb, vb, qc, kc, vc, qd, kd, vd, gates = jnp.split(h, offs, axis=-1)

    def heads(t, n):
        return t.reshape(B, S, n, HEAD_DIM)

    ya = neighborhood_attention(heads(qa, NA_HEADS), heads(ka, NA_HEADS), heads(va, NA_HEADS), rpb)
    t = jnp.arange(S)
    row = (t // GRID_W).astype(jnp.float32)
    col = (t % GRID_W).astype(jnp.float32)
    qn = axial_rope(rms_norm(heads(qb, GQA_HEADS), qk_gain[0]), row, col)
    kn = axial_rope(rms_norm(heads(kb, GQA_KV_HEADS), qk_gain[1]), row, col)
    yb = gqa_attention(qn, kn, heads(vb, GQA_KV_HEADS))
    yc = dilated_attention(heads(qc, DIL_HEADS), heads(kc, DIL_HEADS), heads(vc, DIL_HEADS), t5_bias[:, :DIL_HEADS])
    yd = diff_attention(qd.reshape(B, S, DIFF_HEADS, 2, HEAD_DIM), kd.reshape(B, S, DIFF_HEADS, 2, HEAD_DIM),
                        vd.reshape(B, S, DIFF_HEADS, DIFF_V_DIM), lam_params, diff_gain,
                        t5_bias[:, DIL_HEADS:], lam_init)
    g = jax.nn.sigmoid(gates.reshape(B, S, N_BRANCHES, D))
    merged = (g[:, :, 0] * (ya @ w_na) + g[:, :, 1] * (yb @ w_gqa)
              + g[:, :, 2] * (yc @ w_dil) + g[:, :, 3] * (yd @ w_diff))
    return merged @ w_o


def expert_choice_ffn(x, w_router, w_gate, w_up, w_down):
    B, S, D = x.shape
    cap = EC_CAPACITY * S // N_EXPERTS
    aff = jax.nn.softmax((x @ w_router).astype(jnp.float32), axis=-1)
    gate, idx = lax.top_k(jnp.swapaxes(aff, 1, 2), cap)
    bidx = jnp.arange(B)[:, None, None]
    xin = x[bidx, idx]
    hid = jax.nn.silu(jnp.einsum('becd,edf->becf', xin, w_gate)) * jnp.einsum('becd,edf->becf', xin, w_up)
    yo = jnp.einsum('becf,efd->becd', hid, w_down) * gate[..., None].astype(x.dtype)
    flat = (bidx * S + idx).reshape(-1)
    y = jnp.zeros((B * S, D), yo.dtype).at[flat].add(yo.reshape(-1, D))
    return y.reshape(B, S, D)


def setup_inputs(seed: int = 0) -> dict:
    key = jax.random.key(seed)
    ks = jax.random.split(key, 21)

    def nrm(k, shape, scale):
        return jax.random.normal(k, shape, jnp.float32) * scale

    beta = DEEPNORM_BETA
    D = D_MODEL
    return {
        'x': nrm(ks[0], (BATCH, SEQ, D), 1.0),
        'p': nrm(ks[1], (DEPTH, BATCH, SEQ, PLE_DIM), 1.0),
        'w_in': nrm(ks[2], (DEPTH, D, IN_WIDTH), D ** -0.5),
        'w_branch_na': nrm(ks[3], (DEPTH, NA_W, D), NA_W ** -0.5),
        'w_branch_gqa': nrm(ks[4], (DEPTH, GQA_Q_W, D), GQA_Q_W ** -0.5),
        'w_branch_dil': nrm(ks[5], (DEPTH, DIL_OUT_W, D), DIL_OUT_W ** -0.5),
        'w_branch_diff': nrm(ks[6], (DEPTH, DIFF_V_W, D), DIFF_V_W ** -0.5),
        'w_out': nrm(ks[7], (DEPTH, D, D), beta * D ** -0.5),
        'na_rel_bias': nrm(ks[8], (DEPTH, NA_HEADS, 2 * NA_ROWS - 1, 2 * NA_COLS - 1), 0.1),
        'qk_norm_gain': 1.0 + nrm(ks[9], (DEPTH, 2, HEAD_DIM), 0.02),
        'diff_lambda': nrm(ks[10], (DEPTH, 4, HEAD_DIM), 0.1),
        'diff_norm_gain': 1.0 + nrm(ks[11], (DEPTH, DIFF_V_DIM), 0.02),
        't5_rel_bias': nrm(ks[12], (T5_BUCKETS, T5_HEADS), 0.1),
        'w_router': nrm(ks[13], (DEPTH, D, N_EXPERTS), D ** -0.5),
        'w_expert_gate': nrm(ks[14], (DEPTH, N_EXPERTS, D, EXPERT_FF), D ** -0.5),
        'w_expert_up': nrm(ks[15], (DEPTH, N_EXPERTS, D, EXPERT_FF), D ** -0.5),
        'w_expert_down': nrm(ks[16], (DEPTH, N_EXPERTS, EXPERT_FF, D), beta * EXPERT_FF ** -0.5),
        'w_ple_proj': nrm(ks[17], (DEPTH, PLE_DIM, D), beta * PLE_DIM ** -0.5),
        'w_ple_gate': nrm(ks[18], (DEPTH, D, D), D ** -0.5),
        'ln_gain': 1.0 + nrm(ks[19], (DEPTH, 3, D), 0.02),
        'ln_bias': nrm(ks[20], (DEPTH, 3, D), 0.02),
    }


def reference(x, p, w_in, w_branch_na, w_branch_gqa, w_branch_dil, w_branch_diff, w_out, na_rel_bias,
              qk_norm_gain, diff_lambda, diff_norm_gain, t5_rel_bias, w_router, w_expert_gate, w_expert_up,
              w_expert_down, w_ple_proj, w_ple_gate, ln_gain, ln_bias):
    for i in range(DEPTH):
        lam_init = 0.8 - 0.6 * math.exp(-0.3 * i)
        y = token_mixer(x, w_in[i], w_branch_na[i], w_branch_gqa[i], w_branch_dil[i], w_branch_diff[i],
                        w_out[i], na_rel_bias[i], qk_norm_gain[i], diff_lambda[i], diff_norm_gain[i],
                        t5_rel_bias, lam_init)
        x = layer_norm(DEEPNORM_ALPHA * x + y, ln_gain[i, 0], ln_bias[i, 0])
        y = expert_choice_ffn(x, w_router[i], w_expert_gate[i], w_expert_up[i], w_expert_down[i])
        x = layer_norm(DEEPNORM_ALPHA * x + y, ln_gain[i, 1], ln_bias[i, 1])
        e = (p[i] @ w_ple_proj[i]) * jax.nn.sigmoid(x @ w_ple_gate[i])
        x = layer_norm(DEEPNORM_ALPHA * x + e, ln_gain[i, 2], ln_bias[i, 2])
    return x
```

```python
import functools
import math

import numpy as np
import jax
import jax.numpy as jnp
from jax import lax
from jax.experimental import pallas as pl
from jax.experimental.pallas import tpu as pltpu

F32 = jnp.float32
BF16 = jnp.bfloat16
NEG_INF = -1e30

V7X_VMEM_BYTES = 64 * 1024 * 1024
LANES = 128

GRID_W = 64
HEAD_DIM = 64
NA_HEADS, NA_ROWS, NA_COLS = 6, 8, 16
GQA_HEADS, GQA_KV_HEADS = 6, 2
ROPE_THETA = 10000.0
DIL_GROUPS = ((128, 1), (512, 4), (2048, 16))
DIL_HEADS = 6
DIFF_HEADS = 4
T5_BUCKETS, T5_MAX_DISTANCE = 32, 1024
N_EXPERTS, EC_CAPACITY = 16, 2
SLOT_BLOCK = 128
SLOT_SHIFT = SLOT_BLOCK.bit_length() - 1
TOKEN_CHUNK = 256


def _cparams(semantics, vmem_mib):
    return pltpu.CompilerParams(dimension_semantics=semantics,
                                vmem_limit_bytes=min(vmem_mib * 1024 * 1024, V7X_VMEM_BYTES - 4 * 1024 * 1024))


def _mm_kernel(a_ref, b_ref, o_ref):
    o_ref[...] = jnp.dot(a_ref[...], b_ref[...], preferred_element_type=F32).astype(o_ref.dtype)


def _matmul(a, b, out_dtype, tm, tn, name):
    m, k = a.shape
    n = b.shape[1]
    return pl.pallas_call(
        _mm_kernel,
        grid=(n // tn, m // tm),
        in_specs=[pl.BlockSpec((tm, k), lambda j, i: (i, 0)),
                  pl.BlockSpec((k, tn), lambda j, i: (0, j))],
        out_specs=pl.BlockSpec((tm, tn), lambda j, i: (i, j)),
        out_shape=jax.ShapeDtypeStruct((m, n), out_dtype),
        compiler_params=_cparams(("arbitrary", "arbitrary"), 40),
        name=name,
    )(a, b)


def _lane_lo(rows):
    return lax.broadcasted_iota(jnp.int32, (rows, LANES), 1) < HEAD_DIM


def _dot_nt(a, b):
    return lax.dot_general(a, b, (((1,), (1,)), ((), ())), preferred_element_type=F32)


def _na_kernel(q_ref, k_ref, v_ref, t2_ref, o_ref, *, rows):
    r = pl.program_id(1)
    rs = jnp.clip(r - NA_ROWS // 2, 0, rows - NA_ROWS)
    start = pl.multiple_of(rs * GRID_W, GRID_W)
    dr0 = rs - r + (NA_ROWS - 1)
    win = NA_ROWS * GRID_W
    lo = _lane_lo(GRID_W)
    for p in range(NA_HEADS // 2):
        cols = slice(LANES * p, LANES * (p + 1))
        qp = q_ref[0, :, cols]
        kw = k_ref[0, pl.ds(start, win), cols]
        vw = v_ref[0, pl.ds(start, win), cols]
        outs = []
        for hh in range(2):
            h = 2 * p + hh
            qm = jnp.where(lo if hh == 0 else jnp.logical_not(lo), qp, jnp.zeros_like(qp))
            bias = jnp.concatenate([t2_ref[h, dr0 + 2 * j] for j in range(NA_ROWS // 2)], axis=-1)
            s = _dot_nt(qm, kw) + bias
            m = jnp.max(s, axis=-1, keepdims=True)
            e = jnp.exp(s - m)
            pr = (e / jnp.sum(e, axis=-1, keepdims=True)).astype(BF16)
            outs.append(jnp.dot(pr, vw, preferred_element_type=F32))
        o_ref[0, :, cols] = jnp.where(lo, outs[0], outs[1]).astype(o_ref.dtype)


def _na_bias_tiles(rpb):
    col = jnp.arange(GRID_W)
    cstart = jnp.clip(col - NA_COLS // 2, 0, GRID_W - NA_COLS)
    kc = jnp.arange(GRID_W)
    inwin = (kc[None, :] >= cstart[:, None]) & (kc[None, :] < cstart[:, None] + NA_COLS)
    dc = jnp.clip(kc[None, :] - col[:, None] + (NA_COLS - 1), 0, 2 * NA_COLS - 2)
    t = jnp.where(inwin[None, None], rpb.astype(F32)[:, :, dc], NEG_INF)
    return jnp.concatenate([t[:, :-1], t[:, 1:]], axis=-1)


def _na_attention(ha, t2, batch, seq):
    rows = seq // GRID_W
    w = NA_HEADS * HEAD_DIM
    return pl.pallas_call(
        functools.partial(_na_kernel, rows=rows),
        grid=(batch, rows),
        in_specs=[pl.BlockSpec((1, GRID_W, w), lambda b, r: (b, r, 0)),
                  pl.BlockSpec((1, seq, w), lambda b, r: (b, 0, 1)),
                  pl.BlockSpec((1, seq, w), lambda b, r: (b, 0, 2)),
                  pl.BlockSpec(t2.shape, lambda b, r: (0, 0, 0, 0))],
        out_specs=pl.BlockSpec((1, GRID_W, w), lambda b, r: (b, r, 0)),
        out_shape=jax.ShapeDtypeStruct((batch, seq, w), BF16),
        compiler_params=_cparams(("arbitrary", "arbitrary"), 48),
        name="na_attention",
    )(ha, ha, ha, t2)


def _online_softmax_step(s, m_prev, l_prev, acc_prev, vb, reps):
    m_new = jnp.maximum(m_prev, jnp.max(s, axis=-1, keepdims=True))
    alpha = jnp.exp(m_prev - m_new)
    p = jnp.exp(s - jnp.tile(m_new, (1, reps)))
    l_new = alpha * l_prev + jnp.sum(p, axis=-1, keepdims=True)
    acc_new = alpha * acc_prev + jnp.dot(p.astype(BF16), vb, preferred_element_type=F32)
    return m_new, l_new, acc_new


def _gqa_kernel(q_ref, k_ref, v_ref, o_ref, *, tk, nk):
    tq = q_ref.shape[1]
    lo = _lane_lo(tq)
    rep = GQA_HEADS // GQA_KV_HEADS
    for p in range(GQA_HEADS // 2):
        cols = slice(LANES * p, LANES * (p + 1))
        qp = q_ref[0, :, cols]
        outs = []
        for hh in range(2):
            g = (2 * p + hh) // rep
            gcols = slice(LANES * g, LANES * (g + 1))
            qm = jnp.where(lo if hh == 0 else jnp.logical_not(lo), qp, jnp.zeros_like(qp))

            def body(j, carry, qm=qm, gcols=gcols):
                m_prev, l_prev, acc_prev = carry
                ks = pl.multiple_of(j * tk, tk)
                kb = k_ref[0, pl.ds(ks, tk), gcols]
                vb = v_ref[0, pl.ds(ks, tk), gcols]
                return _online_softmax_step(_dot_nt(qm, kb), m_prev, l_prev, acc_prev, vb, tk // LANES)

            init = (jnp.full((tq, LANES), NEG_INF, F32), jnp.zeros((tq, LANES), F32), jnp.zeros((tq, LANES), F32))
            _, l_fin, acc_fin = lax.fori_loop(0, nk, body, init)
            outs.append(acc_fin / l_fin)
        o_ref[0, :, cols] = jnp.where(lo, outs[0], outs[1]).astype(o_ref.dtype)


def _gqa_attention(q, kd, vd, tq, tk):
    batch, seq, wq = q.shape
    wk = kd.shape[-1]
    return pl.pallas_call(
        functools.partial(_gqa_kernel, tk=tk, nk=seq // tk),
        grid=(batch, seq // tq),
        in_specs=[pl.BlockSpec((1, tq, wq), lambda b, i: (b, i, 0)),
                  pl.BlockSpec((1, seq, wk), lambda b, i: (b, 0, 0)),
                  pl.BlockSpec((1, seq, wk), lambda b, i: (b, 0, 0))],
        out_specs=pl.BlockSpec((1, tq, wq), lambda b, i: (b, i, 0)),
        out_shape=jax.ShapeDtypeStruct((batch, seq, wq), BF16),
        compiler_params=_cparams(("arbitrary", "arbitrary"), 48),
        name="gqa_attention",
    )(q, kd, vd)


def _rope_tables(seq):
    quarter = HEAD_DIM // 4
    freqs = ROPE_THETA ** (-jnp.arange(quarter, dtype=F32) / quarter)
    t = jnp.arange(seq)
    row = (t // GRID_W).astype(F32)
    col = (t % GRID_W).astype(F32)
    ang_r = row[:, None] * freqs[None, :]
    ang_c = col[:, None] * freqs[None, :]
    cos = jnp.concatenate([jnp.cos(ang_r)] * 2 + [jnp.cos(ang_c)] * 2, axis=-1)
    sin = jnp.concatenate([-jnp.sin(ang_r), jnp.sin(ang_r), -jnp.sin(ang_c), jnp.sin(ang_c)], axis=-1)
    return cos, sin


def _rms_rope(t, gain, cos, sin):
    quarter = HEAD_DIM // 4
    y = t * lax.rsqrt(jnp.mean(t * t, axis=-1, keepdims=True) + 1e-6) * gain
    y4 = y.reshape(*y.shape[:-1], 2, 2, quarter)
    partner = jnp.flip(y4, axis=-2).reshape(y.shape)
    return y * cos[None, :, None, :] + partner * sin[None, :, None, :]


def _dil_kernel(q_ref, k_ref, v_ref, bias_ref, o_ref, lse_ref, *, sub_len, nblk):
    tq = q_ref.shape[1]
    half = DIL_GROUPS[0][0] // 2
    win = tq + 2 * half
    i = pl.program_id(2)
    start = pl.multiple_of(jnp.clip(i * tq - half, 0, sub_len - win), half)
    var = jnp.where(i == 0, 0, jnp.where(i == nblk - 1, 2, 1))
    lo = _lane_lo(tq)
    qp = q_ref[0]
    kw = k_ref[0, pl.ds(start, win), :]
    vw = v_ref[0, pl.ds(start, win), :]
    outs, lses = [], []
    for hh in range(2):
        qm = jnp.where(lo if hh == 0 else jnp.logical_not(lo), qp, jnp.zeros_like(qp))
        s = _dot_nt(qm, kw) + bias_ref[hh, var]
        m = jnp.max(s, axis=-1, keepdims=True)
        e = jnp.exp(s - m)
        l = jnp.sum(e, axis=-1, keepdims=True)
        outs.append(jnp.dot((e / l).astype(BF16), vw, preferred_element_type=F32))
        lses.append(jnp.broadcast_to(m + jnp.log(l), (tq, LANES)))
    o_ref[0] = jnp.where(lo, outs[0], outs[1])
    lse_ref[0] = jnp.where(lo, lses[0], lses[1])


def _t5_bucket(rel):
    half = T5_BUCKETS // 2
    exact = half // 2
    n = jnp.abs(rel)
    nf = jnp.maximum(n, 1).astype(F32)
    large = exact + (jnp.log(nf / exact) / math.log(T5_MAX_DISTANCE / exact) * (half - exact)).astype(jnp.int32)
    large = jnp.minimum(large, half - 1)
    return jnp.where(rel > 0, half, 0) + jnp.where(n < exact, n, large)


def _dil_bias_tiles(t5_cols, dilation, tq):
    half = DIL_GROUPS[0][0] // 2
    win = tq + 2 * half
    qi = jnp.arange(tq)[:, None]
    ki = jnp.arange(win)[None, :]
    tiles = []
    for shift in (0, half, 2 * half):
        j = ki - qi - shift
        vals = t5_cols.astype(F32)[_t5_bucket(j * dilation)]
        tiles.append(jnp.where((jnp.abs(j) <= half)[..., None], vals, NEG_INF))
    return jnp.moveaxis(jnp.stack(tiles), -1, 0)


def _dilated_group(hc, bias, g, dilation, batch, seq, tq):
    sub_len = seq // dilation
    nblk = sub_len // tq
    nb = 3 * DIL_HEADS * HEAD_DIM // LANES
    view = hc.reshape(batch, sub_len, dilation * nb * LANES)
    nq = DIL_HEADS * HEAD_DIM // LANES
    out, lse = pl.pallas_call(
        functools.partial(_dil_kernel, sub_len=sub_len, nblk=nblk),
        grid=(batch, dilation, nblk),
        in_specs=[pl.BlockSpec((1, tq, LANES), lambda b, r, i: (b, i, r * nb + g)),
                  pl.BlockSpec((1, sub_len, LANES), lambda b, r, i: (b, 0, r * nb + nq + g)),
                  pl.BlockSpec((1, sub_len, LANES), lambda b, r, i: (b, 0, r * nb + 2 * nq + g)),
                  pl.BlockSpec(bias.shape, lambda b, r, i: (0, 0, 0, 0))],
        out_specs=[pl.BlockSpec((1, tq, LANES), lambda b, r, i: (b, i, r)),
                   pl.BlockSpec((1, tq, LANES), lambda b, r, i: (b, i, r))],
        out_shape=[jax.ShapeDtypeStruct((batch, sub_len, dilation * LANES), F32)] * 2,
        compiler_params=_cparams(("arbitrary", "arbitrary", "arbitrary"), 32),
        name=f"dilated_attention_d{dilation}",
    )(view, view, view, bias)
    return out.reshape(batch, seq, LANES), lse.reshape(batch, seq, LANES)


def _diff_kernel(q_ref, k_ref, v_ref, bias_ref, c_ref, o_ref, *, tk, nk, unit, d_lo, d_hi):
    tq = q_ref.shape[1]
    i = pl.program_id(2)
    lo = _lane_lo(tq)
    qp = q_ref[0]
    q1 = jnp.where(lo, qp, jnp.zeros_like(qp))
    q2 = jnp.where(lo, jnp.zeros_like(qp), qp)

    def body(j, carry):
        m1, l1, a1, m2, l2, a2 = carry
        ks = pl.multiple_of(j * tk, tk)
        kb = k_ref[0, pl.ds(ks, tk), :]
        vb = v_ref[0, pl.ds(ks, tk), :]
        d = (j * tk - i * tq) // unit
        bias = bias_ref[0, jnp.clip(d, d_lo, d_hi) - d_lo]
        m1, l1, a1 = _online_softmax_step(_dot_nt(q1, kb) + bias, m1, l1, a1, vb, tk // LANES)
        m2, l2, a2 = _online_softmax_step(_dot_nt(q2, kb) + bias, m2, l2, a2, vb, tk // LANES)
        return m1, l1, a1, m2, l2, a2

    neg = jnp.full((tq, LANES), NEG_INF, F32)
    zero = jnp.zeros((tq, LANES), F32)
    _, l1, a1, _, l2, a2 = lax.fori_loop(0, nk, body, (neg, zero, zero, neg, zero, zero))
    lam = c_ref[0:1, :]
    out = a1 / l1 - lam * (a2 / l2)
    y = out * lax.rsqrt(jnp.mean(out * out, axis=-1, keepdims=True) + 1e-6)
    o_ref[0] = ((y * c_ref[1:2, :]) * c_ref[2:3, :]).astype(o_ref.dtype)


def _t5_saturation_distance():
    half = T5_BUCKETS // 2
    exact = half // 2
    n = np.arange(1, 4 * T5_MAX_DISTANCE, dtype=np.float32)
    large = exact + (np.log(n / exact) / math.log(T5_MAX_DISTANCE / exact) * (half - exact)).astype(np.int32)
    return int(np.argmax(large >= half - 1)) + 1 + 8


def _diff_bias_tiles(t5_cols, seq, tq, tk):
    unit = math.gcd(tq, tk)
    sat = _t5_saturation_distance()
    d_lo = -((sat + tk - 1) // unit) - 1
    d_hi = (sat + tq - 1) // unit + 1
    qi = jnp.arange(tq)[:, None]
    ki = jnp.arange(tk)[None, :]
    ds = jnp.arange(d_lo, d_hi + 1)
    rel = ds[:, None, None] * unit + ki[None] - qi[None]
    rel = jnp.clip(rel, -(seq - 1), seq - 1)
    tiles = t5_cols.astype(F32)[_t5_bucket(rel)]
    return jnp.moveaxis(tiles, -1, 0), unit, d_lo, d_hi


def _diff_attention(hd, bias, consts, unit, d_lo, d_hi, batch, seq, tq, tk):
    h = DIFF_HEADS
    return pl.pallas_call(
        functools.partial(_diff_kernel, tk=tk, nk=seq // tk, unit=unit, d_lo=d_lo, d_hi=d_hi),
        grid=(batch, h, seq // tq),
        in_specs=[pl.BlockSpec((1, tq, LANES), lambda b, hh, i: (b, i, hh)),
                  pl.BlockSpec((1, seq, LANES), lambda b, hh, i: (b, 0, h + hh)),
                  pl.BlockSpec((1, seq, LANES), lambda b, hh, i: (b, 0, 2 * h + hh)),
                  pl.BlockSpec((1,) + bias.shape[1:], lambda b, hh, i: (hh, 0, 0, 0)),
                  pl.BlockSpec(consts.shape, lambda b, hh, i: (0, 0))],
        out_specs=pl.BlockSpec((1, tq, LANES), lambda b, hh, i: (b, i, hh)),
        out_shape=jax.ShapeDtypeStruct((batch, seq, h * LANES), BF16),
        compiler_params=_cparams(("arbitrary", "arbitrary", "arbitrary"), 48),
        name="diff_attention",
    )(hd, hd, hd, bias, consts)


def _layer_norm(z, g, b):
    mu = jnp.mean(z, axis=-1, keepdims=True)
    zc = z - mu
    var = jnp.mean(zc * zc, axis=-1, keepdims=True)
    return zc * lax.rsqrt(var + 1e-5) * g + b


def _split3(a):
    hi = a.astype(BF16)
    r1 = a - hi.astype(F32)
    mid = r1.astype(BF16)
    lo = (r1 - mid.astype(F32)).astype(BF16)
    return hi, mid, lo


def _merge_kernel(x_ref, ya_ref, yb_ref, o0_ref, o1_ref, o2_ref, l0_ref, l1_ref, l2_ref, yd_ref, g_ref,
                  wna_ref, wgqa_ref, wdil_ref, wdiff_ref, wo_ref, wr_ref, ln_ref,
                  x1_ref, x1b_ref, lg_ref, *, alpha, d_model):
    def sig(t):
        return 1.0 / (1.0 + jnp.exp(-t.astype(F32)))

    l0, l1, l2 = l0_ref[...], l1_ref[...], l2_ref[...]
    mx = jnp.maximum(jnp.maximum(l0, l1), l2)
    e0, e1, e2 = jnp.exp(l0 - mx), jnp.exp(l1 - mx), jnp.exp(l2 - mx)
    den = e0 + e1 + e2
    yc = (e0 / den) * o0_ref[...] + (e1 / den) * o1_ref[...] + (e2 / den) * o2_ref[...]
    d = d_model
    merged = sig(g_ref[:, 0:d]) * jnp.dot(ya_ref[...], wna_ref[...], preferred_element_type=F32)
    merged += sig(g_ref[:, d:2 * d]) * jnp.dot(yb_ref[...], wgqa_ref[...], preferred_element_type=F32)
    merged += sig(g_ref[:, 2 * d:3 * d]) * jnp.dot(yc.astype(BF16), wdil_ref[...], preferred_element_type=F32)
    merged += sig(g_ref[:, 3 * d:4 * d]) * jnp.dot(yd_ref[...], wdiff_ref[...], preferred_element_type=F32)
    y = jnp.dot(merged.astype(BF16), wo_ref[...], preferred_element_type=F32)
    x1 = _layer_norm(alpha * x_ref[...] + y, ln_ref[0:1, :], ln_ref[1:2, :])
    x1_ref[...] = x1
    x1b_ref[...] = x1.astype(BF16)
    xh, xm, _ = _split3(x1)
    wh, wm, _ = wr_ref[0], wr_ref[1], wr_ref[2]
    lg_ref[...] = _dot_nt(wh, xh) + (_dot_nt(wh, xm) + _dot_nt(wm, xh))


def _merge(x, ya, yb, dil, yd, gates, w_na, w_gqa, w_dil, w_diff, w_o, w_r3, ln, alpha, tm):
    n, d = x.shape
    e = w_r3.shape[1]
    (o0, l0), (o1, l1), (o2, l2) = dil
    row = lambda w: pl.BlockSpec((tm, w), lambda i: (i, 0))
    full = lambda a: pl.BlockSpec(a.shape, lambda i: (0,) * a.ndim)
    return pl.pallas_call(
        functools.partial(_merge_kernel, alpha=alpha, d_model=d),
        grid=(n // tm,),
        in_specs=[row(d), row(ya.shape[1]), row(yb.shape[1]), row(LANES), row(LANES), row(LANES),
                  row(LANES), row(LANES), row(LANES), row(yd.shape[1]), row(gates.shape[1]),
                  full(w_na), full(w_gqa), full(w_dil), full(w_diff), full(w_o), full(w_r3), full(ln)],
        out_specs=[row(d), row(d), pl.BlockSpec((e, tm), lambda i: (0, i))],
        out_shape=[jax.ShapeDtypeStruct((n, d), F32), jax.ShapeDtypeStruct((n, d), BF16),
                   jax.ShapeDtypeStruct((e, n), F32)],
        compiler_params=_cparams(("arbitrary",), 48),
        name="branch_merge",
    )(x, ya, yb, o0, o1, o2, l0, l1, l2, yd, gates, w_na, w_gqa, w_dil, w_diff, w_o, w_r3, ln)


def _route_kernel(lg_ref, aff_ref, posm_ref, posx_ref, *, cap):
    lg = lg_ref[0]
    n_e, seq = lg.shape
    mx = jnp.max(lg, axis=0, keepdims=True)
    ex = jnp.exp(lg - mx)
    aff = ex / jnp.sum(ex, axis=0, keepdims=True)
    aff_ref[0] = aff
    bits = pltpu.bitcast(aff, jnp.int32)

    def bisect(_, carry):
        lo, hi = carry
        mid = lo + ((hi - lo + 1) >> 1)
        cnt = jnp.sum((bits >= mid).astype(F32), axis=1, keepdims=True)
        ok = cnt >= float(cap)
        return jnp.where(ok, mid, lo), jnp.where(ok, hi, mid - 1)

    lo0 = jnp.zeros((n_e, 1), jnp.int32)
    hi0 = jnp.full((n_e, 1), 0x7F800000, jnp.int32)
    thr, _ = lax.fori_loop(0, 31, bisect, (lo0, hi0))
    gt = bits > thr
    eq = bits == thr
    needf = float(cap) - jnp.sum(gt.astype(F32), axis=1, keepdims=True)

    tri = (lax.broadcasted_iota(jnp.int32, (LANES, LANES), 0)
           < lax.broadcasted_iota(jnp.int32, (LANES, LANES), 1)).astype(BF16)

    def prefix(mask_bf16, j, carry):
        blk = mask_bf16[:, j * LANES:(j + 1) * LANES]
        excl = jnp.dot(blk, tri, preferred_element_type=F32) + carry
        return excl, carry + jnp.sum(blk.astype(F32), axis=1, keepdims=True)

    eqb = eq.astype(BF16)
    carry_eq = jnp.zeros((n_e, 1), F32)
    carry_sel = jnp.zeros((n_e, 1), F32)
    for j in range(seq // LANES):
        cols = slice(j * LANES, (j + 1) * LANES)
        rank, carry_eq = prefix(eqb, j, carry_eq)
        sel = jnp.logical_or(gt[:, cols], jnp.logical_and(eq[:, cols], rank < needf))
        selb = sel.astype(BF16)
        excl = jnp.dot(selb, tri, preferred_element_type=F32) + carry_sel
        carry_sel = carry_sel + jnp.sum(selb.astype(F32), axis=1, keepdims=True)
        pos = excl.astype(jnp.int32)
        posx_ref[0, :, cols] = pos
        posm_ref[0, :, cols] = jnp.where(sel, pos, -1)


def _route(logits_t, batch, seq, cap):
    n_e = logits_t.shape[0]
    lg = logits_t.reshape(n_e, batch, seq).transpose(1, 0, 2)
    spec = pl.BlockSpec((1, n_e, seq), lambda b: (b, 0, 0))
    return pl.pallas_call(
        functools.partial(_route_kernel, cap=cap),
        grid=(batch,),
        in_specs=[spec],
        out_specs=[spec, spec, spec],
        out_shape=[jax.ShapeDtypeStruct((batch, n_e, seq), F32), jax.ShapeDtypeStruct((batch, n_e, seq), jnp.int32),
                   jax.ShapeDtypeStruct((batch, n_e, seq), jnp.int32)],
        compiler_params=_cparams(("arbitrary",), 32),
        name="ec_route",
    )(lg)


def _gather_kernel(clo_ref, chi_ref, x_ref, pos_ref, aff_ref, xin_ref, gs_ref, acc_ref, *, n_e, nsb):
    base = (pl.program_id(0) * n_e + pl.program_id(1)) * nsb
    for sb in range(nsb):
        slot_ids = sb * SLOT_BLOCK + lax.broadcasted_iota(jnp.int32, (SLOT_BLOCK, TOKEN_CHUNK), 0)
        acc_ref[...] = jnp.zeros_like(acc_ref)

        def body(c, g, slot_ids=slot_ids):
            hit = pos_ref[0, 0, pl.ds(c, 1), :] == slot_ids
            xc = x_ref[0, pl.ds(pl.multiple_of(c * TOKEN_CHUNK, TOKEN_CHUNK), TOKEN_CHUNK), :]
            acc_ref[...] += jnp.dot(hit.astype(BF16), xc, preferred_element_type=F32)
            return g + jnp.sum(jnp.where(hit, aff_ref[0, 0, pl.ds(c, 1), :], 0.0), axis=-1, keepdims=True)

        g = lax.fori_loop(clo_ref[base + sb], chi_ref[base + sb], body, jnp.zeros((SLOT_BLOCK, 1), F32))
        rows = slice(sb * SLOT_BLOCK, (sb + 1) * SLOT_BLOCK)
        xin_ref[0, 0, rows, :] = acc_ref[...].astype(BF16)
        gs_ref[0, 0, rows, :] = jnp.broadcast_to(g, (SLOT_BLOCK, LANES))


def _gather(x1b, posm, aff, clo, chi, cap):
    batch, seq, d = x1b.shape
    n_e = posm.shape[1]
    nsb = cap // SLOT_BLOCK
    nch = seq // TOKEN_CHUNK
    pos4 = posm.reshape(batch, n_e, nch, TOKEN_CHUNK)
    aff4 = aff.reshape(batch, n_e, nch, TOKEN_CHUNK)
    grid_spec = pltpu.PrefetchScalarGridSpec(
        num_scalar_prefetch=2,
        grid=(batch, n_e),
        in_specs=[pl.BlockSpec((1, seq, d), lambda b, e, *_: (b, 0, 0)),
                  pl.BlockSpec((1, 1, nch, TOKEN_CHUNK), lambda b, e, *_: (b, e, 0, 0)),
                  pl.BlockSpec((1, 1, nch, TOKEN_CHUNK), lambda b, e, *_: (b, e, 0, 0))],
        out_specs=[pl.BlockSpec((1, 1, cap, d), lambda b, e, *_: (b, e, 0, 0)),
                   pl.BlockSpec((1, 1, cap, LANES), lambda b, e, *_: (b, e, 0, 0))],
        scratch_shapes=[pltpu.VMEM((SLOT_BLOCK, d), F32)],
    )
    return pl.pallas_call(
        functools.partial(_gather_kernel, n_e=n_e, nsb=nsb),
        grid_spec=grid_spec,
        out_shape=[jax.ShapeDtypeStruct((batch, n_e, cap, d), BF16),
                   jax.ShapeDtypeStruct((batch, n_e, cap, LANES), F32)],
        compiler_params=_cparams(("arbitrary", "arbitrary"), 56),
        name="ec_gather",
    )(clo, chi, x1b, pos4, aff4)


def _ffn_kernel(x_ref, gs_ref, wg_ref, wu_ref, wd_ref, o_ref, *, fchunk):
    x = x_ref[0, 0]
    ff = wg_ref.shape[2]
    acc = jnp.zeros((x.shape[0], wd_ref.shape[2]), F32)
    for f0 in range(0, ff, fchunk):
        g = jnp.dot(x, wg_ref[0, :, f0:f0 + fchunk], preferred_element_type=F32)
        u = jnp.dot(x, wu_ref[0, :, f0:f0 + fchunk], preferred_element_type=F32)
        hid = (g / (1.0 + jnp.exp(-g))) * u
        acc += jnp.dot(hid.astype(BF16), wd_ref[0, f0:f0 + fchunk, :], preferred_element_type=F32)
    o_ref[0, 0] = (acc * gs_ref[0, 0][:, 0:1]).astype(o_ref.dtype)


def _expert_ffn(xin, gslot, wg, wu, wd, tc, fchunk):
    batch, n_e, cap, d = xin.shape
    ff = wg.shape[2]
    return pl.pallas_call(
        functools.partial(_ffn_kernel, fchunk=fchunk),
        grid=(n_e, batch, cap // tc),
        in_specs=[pl.BlockSpec((1, 1, tc, d), lambda e, b, i: (b, e, i, 0)),
                  pl.BlockSpec((1, 1, tc, LANES), lambda e, b, i: (b, e, i, 0)),
                  pl.BlockSpec((1, d, ff), lambda e, b, i: (e, 0, 0)),
                  pl.BlockSpec((1, d, ff), lambda e, b, i: (e, 0, 0)),
                  pl.BlockSpec((1, ff, d), lambda e, b, i: (e, 0, 0))],
        out_specs=pl.BlockSpec((1, 1, tc, d), lambda e, b, i: (b, e, i, 0)),
        out_shape=jax.ShapeDtypeStruct((batch, n_e, cap, d), BF16),
        compiler_params=_cparams(("arbitrary", "arbitrary", "arbitrary"), 56),
        name="expert_ffn",
    )(xin, gslot, wg, wu, wd)


def _combine_kernel(cs_ref, yo_ref, post_ref, y_ref, *, n_e, nch):
    b = pl.program_id(0)
    slot_lane = lax.broadcasted_iota(jnp.int32, (TOKEN_CHUNK, SLOT_BLOCK), 1)

    def chunk(c, _):
        rows = pl.ds(pl.multiple_of(c * TOKEN_CHUNK, TOKEN_CHUNK), TOKEN_CHUNK)
        pos_all = post_ref[0, rows, :]
        acc = jnp.zeros((TOKEN_CHUNK, y_ref.shape[2]), F32)
        for e in range(n_e):
            base = (b * n_e + e) * (nch + 1) + c
            s_lo = cs_ref[base]
            s_hi = cs_ref[base + 1]
            sb_lo = s_lo >> SLOT_SHIFT
            sb_hi = jnp.where(s_hi > s_lo, ((s_hi - 1) >> SLOT_SHIFT) + 1, sb_lo)
            pcol = pos_all[:, e:e + 1]

            def body(sb, a, pcol=pcol, e=e):
                hit = pcol == (sb * SLOT_BLOCK + slot_lane)
                yb = yo_ref[0, e, pl.ds(pl.multiple_of(sb * SLOT_BLOCK, SLOT_BLOCK), SLOT_BLOCK), :]
                return a + jnp.dot(hit.astype(BF16), yb, preferred_element_type=F32)

            acc = lax.fori_loop(sb_lo, sb_hi, body, acc)
        y_ref[0, rows, :] = acc
        return 0

    lax.fori_loop(0, nch, chunk, 0)


def _combine(yo, pos_t, cs, td):
    batch, n_e, cap, d = yo.shape
    seq = pos_t.shape[1]
    nch = seq // TOKEN_CHUNK
    grid_spec = pltpu.PrefetchScalarGridSpec(
        num_scalar_prefetch=1,
        grid=(batch, d // td),
        in_specs=[pl.BlockSpec((1, n_e, cap, td), lambda b, j, *_: (b, 0, 0, j)),
                  pl.BlockSpec((1, seq, n_e), lambda b, j, *_: (b, 0, 0))],
        out_specs=pl.BlockSpec((1, seq, td), lambda b, j, *_: (b, 0, j)),
    )
    return pl.pallas_call(
        functools.partial(_combine_kernel, n_e=n_e, nch=nch),
        grid_spec=grid_spec,
        out_shape=jax.ShapeDtypeStruct((batch, seq, d), F32),
        compiler_params=_cparams(("arbitrary", "arbitrary"), 56),
        name="ec_combine",
    )(cs, yo, pos_t)


def _ple_kernel(x1_ref, y_ref, p_ref, wpg_ref, wpp_ref, ln_ref, x3_ref, x3b_ref, *, alpha):
    x2 = _layer_norm(alpha * x1_ref[...] + y_ref[...], ln_ref[0:1, :], ln_ref[1:2, :])
    gate = 1.0 / (1.0 + jnp.exp(-jnp.dot(x2.astype(BF16), wpg_ref[...], preferred_element_type=F32)))
    emb = jnp.dot(p_ref[...].astype(BF16), wpp_ref[...], preferred_element_type=F32) * gate
    x3 = _layer_norm(alpha * x2 + emb, ln_ref[2:3, :], ln_ref[3:4, :])
    x3_ref[...] = x3
    x3b_ref[...] = x3.astype(BF16)


def _ple(x1, y, p, w_pg, w_pp, ln, alpha, tm):
    n, d = x1.shape
    row = lambda w: pl.BlockSpec((tm, w), lambda i: (i, 0))
    full = lambda a: pl.BlockSpec(a.shape, lambda i: (0,) * a.ndim)
    return pl.pallas_call(
        functools.partial(_ple_kernel, alpha=alpha),
        grid=(n // tm,),
        in_specs=[row(d), row(d), row(p.shape[1]), full(w_pg), full(w_pp), full(ln)],
        out_specs=[row(d), row(d)],
        out_shape=[jax.ShapeDtypeStruct((n, d), F32), jax.ShapeDtypeStruct((n, d), BF16)],
        compiler_params=_cparams(("arbitrary",), 40),
        name="ple_norm",
    )(x1, y, p, w_pg, w_pp, ln)


def _token_mixer_inputs(xb, w_in, qk_gain, batch, seq):
    d = w_in.shape[0]
    scale = HEAD_DIM ** -0.5
    na_w = NA_HEADS * HEAD_DIM
    gq_w, gkv_w = GQA_HEADS * HEAD_DIM, GQA_KV_HEADS * HEAD_DIM
    dil_w = DIL_HEADS * HEAD_DIM
    dqk_w = DIFF_HEADS * 2 * HEAD_DIM
    o = 0
    w_a = w_in[:, o:o + 3 * na_w]; o += 3 * na_w
    w_b = w_in[:, o:o + gq_w + 2 * gkv_w]; o += gq_w + 2 * gkv_w
    w_c = w_in[:, o:o + 3 * dil_w]; o += 3 * dil_w
    w_d = w_in[:, o:o + 3 * dqk_w]; o += 3 * dqk_w
    w_g = w_in[:, o:]

    def qscaled(w, qw):
        return jnp.concatenate([w[:, :qw] * scale, w[:, qw:]], axis=1).astype(BF16)

    n = batch * seq
    ha = _matmul(xb, qscaled(w_a, na_w), BF16, 1024, 3 * na_w, "proj_na").reshape(batch, seq, -1)
    hb = _matmul(xb, w_b.astype(BF16), F32, 1024, w_b.shape[1], "proj_gqa").reshape(batch, seq, -1)
    hc = _matmul(xb, qscaled(w_c, dil_w), BF16, 1024, 3 * dil_w, "proj_dil").reshape(batch, seq, -1)
    hd = _matmul(xb, qscaled(w_d, dqk_w), BF16, 1024, 3 * dqk_w, "proj_diff").reshape(batch, seq, -1)
    gates = _matmul(xb, w_g.astype(BF16), BF16, 1024, 1024, "proj_gates")
    cos, sin = _rope_tables(seq)
    qn = _rms_rope(hb[..., :gq_w].reshape(batch, seq, GQA_HEADS, HEAD_DIM), qk_gain[0] * scale, cos, sin)
    kn = _rms_rope(hb[..., gq_w:gq_w + gkv_w].reshape(batch, seq, GQA_KV_HEADS, HEAD_DIM), qk_gain[1], cos, sin)
    vb = hb[..., gq_w + gkv_w:].reshape(batch, seq, GQA_KV_HEADS, HEAD_DIM)
    dup = lambda t: jnp.concatenate([t, t], axis=-1).reshape(batch, seq, 2 * gkv_w).astype(BF16)
    return ha, qn.reshape(batch, seq, gq_w).astype(BF16), dup(kn), dup(vb), hc, hd, gates


def _layer(x, xb, p, w, lam_init, alpha):
    batch, seq, d = x.shape
    n = batch * seq
    ha, qn, kd, vd, hc, hd, gates = _token_mixer_inputs(xb.reshape(n, d), w["w_in"], w["qk_gain"], batch, seq)

    ya = _na_attention(ha, _na_bias_tiles(w["rpb"]), batch, seq)
    yb = _gqa_attention(qn, kd, vd, 256, 512)
    dil = []
    for g, (_, dilation) in enumerate(DIL_GROUPS):
        bias = _dil_bias_tiles(w["t5"][:, 2 * g:2 * g + 2], dilation, 128)
        dil.append(_dilated_group(hc, bias, g, dilation, batch, seq, 128))
    lp = w["lam_params"].astype(F32)
    lam = jnp.exp(jnp.sum(lp[0] * lp[1])) - jnp.exp(jnp.sum(lp[2] * lp[3])) + lam_init
    consts = jnp.zeros((8, LANES), F32).at[0].set(lam).at[1].set(w["diff_gain"]).at[2].set(1.0 - lam_init)
    dbias, unit, d_lo, d_hi = _diff_bias_tiles(w["t5"][:, DIL_HEADS:], seq, 256, 512)
    yd = _diff_attention(hd, dbias, consts, unit, d_lo, d_hi, batch, seq, 256, 512)

    flat = lambda t: t.reshape(n, t.shape[-1])
    w_r3 = jnp.stack(_split3(w["w_router"].T))
    ln = w["ln"]
    x1, x1b, logits_t = _merge(
        x.reshape(n, d), flat(ya), flat(yb), [(flat(o), flat(l)) for o, l in dil], flat(yd), gates,
        w["w_na"].astype(BF16), w["w_gqa"].astype(BF16), w["w_dil"].astype(BF16), w["w_diff"].astype(BF16),
        w["w_o"].astype(BF16), w_r3, jnp.stack([ln[0][0], ln[1][0]]), alpha, 256)

    cap = EC_CAPACITY * seq // N_EXPERTS
    aff, posm, posx = _route(logits_t, batch, seq, cap)
    nch = seq // TOKEN_CHUNK
    cs = jnp.concatenate([posx[:, :, ::TOKEN_CHUNK], jnp.full((batch, N_EXPERTS, 1), cap, jnp.int32)], axis=-1)
    edges = jnp.arange(cap // SLOT_BLOCK, dtype=jnp.int32) * SLOT_BLOCK
    clo = jnp.sum(cs[:, :, 1:, None] <= edges, axis=2).astype(jnp.int32)
    chi = jnp.sum(cs[:, :, :nch, None] < edges + SLOT_BLOCK, axis=2).astype(jnp.int32)
    xin, gslot = _gather(x1b.reshape(batch, seq, d), posm, aff, clo.reshape(-1), chi.reshape(-1), cap)
    yo = _expert_ffn(xin, gslot, w["w_eg"].astype(BF16), w["w_eu"].astype(BF16), w["w_ed"].astype(BF16), 512, 512)
    y = _combine(yo, posm.transpose(0, 2, 1), cs.reshape(-1), 256)

    ln4 = jnp.stack([ln[0][1], ln[1][1], ln[0][2], ln[1][2]])
    x3, x3b = _ple(x1, y.reshape(n, d), p.reshape(n, -1), w["w_pg"].astype(BF16), w["w_pp"].astype(BF16), ln4, alpha, 512)
    return x3.reshape(batch, seq, d), x3b.reshape(batch, seq, d)


def kernel(x, p, w_in, w_branch_na, w_branch_gqa, w_branch_dil, w_branch_diff, w_out, na_rel_bias, qk_norm_gain,
           diff_lambda, diff_norm_gain, t5_rel_bias, w_router, w_expert_gate, w_expert_up, w_expert_down,
           w_ple_proj, w_ple_gate, ln_gain, ln_bias):
    depth = w_in.shape[0]
    alpha = (2 * depth) ** 0.25
    xb = x.astype(BF16)
    for i in range(depth):
        lam_init = 0.8 - 0.6 * math.exp(-0.3 * i)
        w = dict(w_in=w_in[i], w_na=w_branch_na[i], w_gqa=w_branch_gqa[i], w_dil=w_branch_dil[i],
                 w_diff=w_branch_diff[i], w_o=w_out[i], rpb=na_rel_bias[i], qk_gain=qk_norm_gain[i],
                 lam_params=diff_lambda[i], diff_gain=diff_norm_gain[i], t5=t5_rel_bias, w_router=w_router[i],
                 w_eg=w_expert_gate[i], w_eu=w_expert_up[i], w_ed=w_expert_down[i], w_pp=w_ple_proj[i],
                 w_pg=w_ple_gate[i], ln=(ln_gain[i], ln_bias[i]))
        x, xb = _layer(x, xb, p[i], w, lam_init, alpha)
    return x
```

```python
import functools
import math

import numpy as np
import jax
import jax.numpy as jnp
from jax import lax
from jax.experimental import pallas as pl
from jax.experimental.pallas import tpu as pltpu

F32 = jnp.float32
BF16 = jnp.bfloat16
NEG_INF = -1e30

V7X_VMEM_BYTES = 64 * 1024 * 1024
LANES = 128
LANE_SHIFT = 7

GRID_W = 64
HEAD_DIM = 64
NA_HEADS, NA_ROWS, NA_COLS = 6, 8, 16
NA_ROWS_PER_STEP = 4
LOG2E = 1.4426950408889634
ATTN_TQ, ATTN_TK = 1024, 256
DIFF_BIAS_ROWS = 256
GQA_HEADS, GQA_KV_HEADS = 6, 2
ROPE_THETA = 10000.0
DIL_GROUPS = ((128, 1), (512, 4), (2048, 16))
DIL_HEADS = 6
DIFF_HEADS = 4
T5_BUCKETS, T5_MAX_DISTANCE = 32, 1024
N_EXPERTS, EC_CAPACITY = 16, 2
SLOT_BLOCK = 128
SLOT_SHIFT = SLOT_BLOCK.bit_length() - 1
TOKEN_CHUNK = 256


def _cparams(semantics, vmem_mib):
    return pltpu.CompilerParams(dimension_semantics=semantics,
                                vmem_limit_bytes=min(vmem_mib * 1024 * 1024, V7X_VMEM_BYTES - 4 * 1024 * 1024))


def _mm_kernel(a_ref, b_ref, o_ref):
    o_ref[...] = jnp.dot(a_ref[...], b_ref[...], preferred_element_type=F32).astype(o_ref.dtype)


def _matmul(a, b, out_dtype, tm, tn, name):
    m, k = a.shape
    n = b.shape[1]
    return pl.pallas_call(
        _mm_kernel,
        grid=(n // tn, m // tm),
        in_specs=[pl.BlockSpec((tm, k), lambda j, i: (i, 0)),
                  pl.BlockSpec((k, tn), lambda j, i: (0, j))],
        out_specs=pl.BlockSpec((tm, tn), lambda j, i: (i, j)),
        out_shape=jax.ShapeDtypeStruct((m, n), out_dtype),
        compiler_params=_cparams(("arbitrary", "arbitrary"), 40),
        name=name,
    )(a, b)


def _toeplitz(u, rows, width):
    lead = u.shape[:-1]
    period = rows + width
    text = jnp.concatenate([u[..., rows - 1:], jnp.zeros(lead + (1,), u.dtype), u[..., :rows - 1]], axis=-1)
    flat = jnp.tile(text, (1,) * len(lead) + (rows,))[..., :rows * (period - 1)]
    return flat.reshape(lead + (rows, period - 1))[..., :width]


def _lane_lo(rows):
    return lax.broadcasted_iota(jnp.int32, (rows, LANES), 1) < HEAD_DIM


def _dot_nt(a, b):
    return lax.dot_general(a, b, (((1,), (1,)), ((), ())), preferred_element_type=F32)


def _na_kernel(q_ref, k_ref, v_ref, t2_ref, o_ref, *, rows):
    win = NA_ROWS * GRID_W
    lo = _lane_lo(GRID_W)
    for rr in range(NA_ROWS_PER_STEP):
        r = pl.program_id(1) * NA_ROWS_PER_STEP + rr
        rs = jnp.clip(r - NA_ROWS // 2, 0, rows - NA_ROWS)
        start = pl.multiple_of(rs * GRID_W, GRID_W)
        dr0 = rs - r + (NA_ROWS - 1)
        qrows = slice(GRID_W * rr, GRID_W * (rr + 1))
        for p in range(NA_HEADS // 2):
            cols = slice(LANES * p, LANES * (p + 1))
            qp = q_ref[0, qrows, cols]
            kw = k_ref[0, pl.ds(start, win), cols]
            vw = v_ref[0, pl.ds(start, win), cols]
            outs = []
            for hh in range(2):
                h = 2 * p + hh
                qm = jnp.where(lo if hh == 0 else jnp.logical_not(lo), qp, jnp.zeros_like(qp))
                bias = jnp.concatenate([t2_ref[h, dr0 + 2 * j] for j in range(NA_ROWS // 2)], axis=-1)
                s = _dot_nt(qm, kw) + bias
                m = jnp.max(s, axis=-1, keepdims=True)
                e = jnp.exp(s - m)
                pr = (e / jnp.sum(e, axis=-1, keepdims=True)).astype(BF16)
                outs.append(jnp.dot(pr, vw, preferred_element_type=F32))
            o_ref[0, qrows, cols] = jnp.where(lo, outs[0], outs[1]).astype(o_ref.dtype)


def _na_bias_tiles(rpb):
    col = jnp.arange(GRID_W)
    cstart = jnp.clip(col - NA_COLS // 2, 0, GRID_W - NA_COLS)
    kc = jnp.arange(GRID_W)
    inwin = (kc[None, :] >= cstart[:, None]) & (kc[None, :] < cstart[:, None] + NA_COLS)
    pad = GRID_W - NA_COLS
    u = jnp.pad(rpb.astype(F32), ((0, 0), (0, 0), (pad, pad)))
    t = jnp.where(inwin[None, None], _toeplitz(u, GRID_W, GRID_W), NEG_INF)
    return jnp.concatenate([t[:, :-1], t[:, 1:]], axis=-1)


def _na_attention(ha, t2, batch, seq):
    rows = seq // GRID_W
    w = NA_HEADS * HEAD_DIM
    tq = GRID_W * NA_ROWS_PER_STEP
    return pl.pallas_call(
        functools.partial(_na_kernel, rows=rows),
        grid=(batch, rows // NA_ROWS_PER_STEP),
        in_specs=[pl.BlockSpec((1, tq, w), lambda b, r: (b, r, 0)),
                  pl.BlockSpec((1, seq, w), lambda b, r: (b, 0, 1)),
                  pl.BlockSpec((1, seq, w), lambda b, r: (b, 0, 2)),
                  pl.BlockSpec(t2.shape, lambda b, r: (0, 0, 0, 0))],
        out_specs=pl.BlockSpec((1, tq, w), lambda b, r: (b, r, 0)),
        out_shape=jax.ShapeDtypeStruct((batch, seq, w), BF16),
        compiler_params=_cparams(("arbitrary", "arbitrary"), 48),
        name="na_attention",
    )(ha, ha, ha, t2)


def _softmax_stream_update(s, vb, m_ref, l_ref, acc_ref, idx):
    m_prev = m_ref[idx]
    m_new = jnp.maximum(m_prev, jnp.max(s, axis=-1, keepdims=True))
    alpha = jnp.exp2(m_prev - m_new)
    p = jnp.exp2(s - jnp.tile(m_new, (1, s.shape[1] // LANES)))
    m_ref[idx] = m_new
    l_ref[idx] = alpha * l_ref[idx] + jnp.sum(p, axis=-1, keepdims=True)
    acc_ref[idx] = alpha * acc_ref[idx] + jnp.dot(p.astype(BF16), vb, preferred_element_type=F32)


def _init_streams(m_ref, l_ref, acc_ref):
    m_ref[...] = jnp.full(m_ref.shape, NEG_INF, F32)
    l_ref[...] = jnp.zeros(l_ref.shape, F32)
    acc_ref[...] = jnp.zeros(acc_ref.shape, F32)


def _gqa_kernel(q_ref, k_ref, v_ref, o_ref, qm_ref, m_ref, l_ref, acc_ref, *, tk, nk):
    tq = q_ref.shape[1]
    lo = _lane_lo(tq)
    rep = GQA_HEADS // GQA_KV_HEADS
    for h in range(GQA_HEADS):
        qp = q_ref[0, :, LANES * (h // 2):LANES * (h // 2 + 1)]
        qm_ref[h] = jnp.where(lo if h % 2 == 0 else jnp.logical_not(lo), qp, jnp.zeros_like(qp))
    _init_streams(m_ref, l_ref, acc_ref)

    def body(j, carry):
        ks = pl.multiple_of(j * tk, tk)
        for g in range(GQA_KV_HEADS):
            gcols = slice(LANES * g, LANES * (g + 1))
            kb = k_ref[0, pl.ds(ks, tk), gcols]
            vb = v_ref[0, pl.ds(ks, tk), gcols]
            for h in range(rep * g, rep * (g + 1)):
                _softmax_stream_update(_dot_nt(qm_ref[h], kb), vb, m_ref, l_ref, acc_ref, h)
        return carry

    lax.fori_loop(0, nk, body, 0)
    for p in range(GQA_HEADS // 2):
        o0 = acc_ref[2 * p] / l_ref[2 * p]
        o1 = acc_ref[2 * p + 1] / l_ref[2 * p + 1]
        o_ref[0, :, LANES * p:LANES * (p + 1)] = jnp.where(lo, o0, o1).astype(o_ref.dtype)


def _gqa_attention(q, kd, vd, tq, tk):
    batch, seq, wq = q.shape
    wk = kd.shape[-1]
    state = pltpu.VMEM((GQA_HEADS, tq, LANES), F32)
    return pl.pallas_call(
        functools.partial(_gqa_kernel, tk=tk, nk=seq // tk),
        grid=(batch, seq // tq),
        in_specs=[pl.BlockSpec((1, tq, wq), lambda b, i: (b, i, 0)),
                  pl.BlockSpec((1, seq, wk), lambda b, i: (b, 0, 0)),
                  pl.BlockSpec((1, seq, wk), lambda b, i: (b, 0, 0))],
        out_specs=pl.BlockSpec((1, tq, wq), lambda b, i: (b, i, 0)),
        out_shape=jax.ShapeDtypeStruct((batch, seq, wq), BF16),
        scratch_shapes=[pltpu.VMEM((GQA_HEADS, tq, LANES), BF16), state, state, state],
        compiler_params=_cparams(("arbitrary", "arbitrary"), 48),
        name="gqa_attention",
    )(q, kd, vd)


def _rope_tables(seq):
    quarter = HEAD_DIM // 4
    freqs = ROPE_THETA ** (-jnp.arange(quarter, dtype=F32) / quarter)
    t = jnp.arange(seq)
    row = (t // GRID_W).astype(F32)
    col = (t % GRID_W).astype(F32)
    ang_r = row[:, None] * freqs[None, :]
    ang_c = col[:, None] * freqs[None, :]
    cos = jnp.concatenate([jnp.cos(ang_r)] * 2 + [jnp.cos(ang_c)] * 2, axis=-1)
    sin = jnp.concatenate([-jnp.sin(ang_r), jnp.sin(ang_r), -jnp.sin(ang_c), jnp.sin(ang_c)], axis=-1)
    return cos, sin


def _rms_rope(t, gain, cos, sin):
    quarter = HEAD_DIM // 4
    y = t * lax.rsqrt(jnp.mean(t * t, axis=-1, keepdims=True) + 1e-6) * gain
    y4 = y.reshape(*y.shape[:-1], 2, 2, quarter)
    partner = jnp.flip(y4, axis=-2).reshape(y.shape)
    return y * cos[None, :, None, :] + partner * sin[None, :, None, :]


def _dil_kernel(q_ref, k_ref, v_ref, bias_ref, o_ref, lse_ref, *, sub_len, nblk):
    tq = q_ref.shape[1]
    half = DIL_GROUPS[0][0] // 2
    win = tq + 2 * half
    i = pl.program_id(2)
    start = pl.multiple_of(jnp.clip(i * tq - half, 0, sub_len - win), half)
    var = jnp.where(i == 0, 0, jnp.where(i == nblk - 1, 2, 1))
    lo = _lane_lo(tq)
    qp = q_ref[0]
    kw = k_ref[0, pl.ds(start, win), :]
    vw = v_ref[0, pl.ds(start, win), :]
    outs, lses = [], []
    for hh in range(2):
        qm = jnp.where(lo if hh == 0 else jnp.logical_not(lo), qp, jnp.zeros_like(qp))
        s = _dot_nt(qm, kw) + bias_ref[hh, var]
        m = jnp.max(s, axis=-1, keepdims=True)
        e = jnp.exp(s - m)
        l = jnp.sum(e, axis=-1, keepdims=True)
        outs.append(jnp.dot((e / l).astype(BF16), vw, preferred_element_type=F32))
        lses.append(jnp.broadcast_to(m + jnp.log(l), (tq, LANES)))
    o_ref[0] = jnp.where(lo, outs[0], outs[1])
    lse_ref[0] = jnp.where(lo, lses[0], lses[1])


def _t5_bucket(rel):
    half = T5_BUCKETS // 2
    exact = half // 2
    n = jnp.abs(rel)
    nf = jnp.maximum(n, 1).astype(F32)
    large = exact + (jnp.log(nf / exact) / math.log(T5_MAX_DISTANCE / exact) * (half - exact)).astype(jnp.int32)
    large = jnp.minimum(large, half - 1)
    return jnp.where(rel > 0, half, 0) + jnp.where(n < exact, n, large)


def _dil_bias_tiles(t5_cols, dilation, tq):
    half = DIL_GROUPS[0][0] // 2
    win = tq + 2 * half
    wide = win + 2 * half
    j = jnp.arange(tq - 1 + wide) - (tq - 1) - 2 * half
    vals = t5_cols.astype(F32)[_t5_bucket(j * dilation)]
    u = jnp.where((jnp.abs(j) <= half)[:, None], vals, NEG_INF).T
    a = _toeplitz(u, tq, wide)
    return jnp.stack([a[:, :, 2 * half - s:2 * half - s + win] for s in (0, half, 2 * half)], axis=1)


def _dilated_group(hc, bias, g, dilation, batch, seq, tq):
    sub_len = seq // dilation
    nblk = sub_len // tq
    nb = 3 * DIL_HEADS * HEAD_DIM // LANES
    view = hc.reshape(batch, sub_len, dilation * nb * LANES)
    nq = DIL_HEADS * HEAD_DIM // LANES
    out, lse = pl.pallas_call(
        functools.partial(_dil_kernel, sub_len=sub_len, nblk=nblk),
        grid=(batch, dilation, nblk),
        in_specs=[pl.BlockSpec((1, tq, LANES), lambda b, r, i: (b, i, r * nb + g)),
                  pl.BlockSpec((1, sub_len, LANES), lambda b, r, i: (b, 0, r * nb + nq + g)),
                  pl.BlockSpec((1, sub_len, LANES), lambda b, r, i: (b, 0, r * nb + 2 * nq + g)),
                  pl.BlockSpec(bias.shape, lambda b, r, i: (0, 0, 0, 0))],
        out_specs=[pl.BlockSpec((1, tq, LANES), lambda b, r, i: (b, i, r)),
                   pl.BlockSpec((1, tq, LANES), lambda b, r, i: (b, i, r))],
        out_shape=[jax.ShapeDtypeStruct((batch, sub_len, dilation * LANES), F32)] * 2,
        compiler_params=_cparams(("arbitrary", "arbitrary", "arbitrary"), 32),
        name=f"dilated_attention_d{dilation}",
    )(view, view, view, bias)
    return out.reshape(batch, seq, LANES), lse.reshape(batch, seq, LANES)


def _diff_kernel(q_ref, k_ref, v_ref, strip_ref, c_ref, o_ref, qm_ref, m_ref, l_ref, acc_ref, *,
                 tk, nk, delta_lo, delta_hi):
    tq = q_ref.shape[1]
    i = pl.program_id(1)
    lo = _lane_lo(tq)
    for h in range(DIFF_HEADS):
        qp = q_ref[0, :, LANES * h:LANES * (h + 1)]
        qm_ref[2 * h] = jnp.where(lo, qp, jnp.zeros_like(qp))
        qm_ref[2 * h + 1] = jnp.where(lo, jnp.zeros_like(qp), qp)
    _init_streams(m_ref, l_ref, acc_ref)

    def body(j, carry):
        ks = pl.multiple_of(j * tk, tk)
        sub = strip_ref.shape[2]
        blk0 = [(jnp.clip(j * tk - i * tq - a * sub, delta_lo, delta_hi) - delta_lo) >> LANE_SHIFT
                for a in range(tq // sub)]
        for h in range(DIFF_HEADS):
            cols = slice(LANES * h, LANES * (h + 1))
            kb = k_ref[0, pl.ds(ks, tk), cols]
            vb = v_ref[0, pl.ds(ks, tk), cols]
            bias = jnp.concatenate(
                [jnp.concatenate([strip_ref[h, b0 + c] for c in range(tk // LANES)], axis=-1) for b0 in blk0], axis=0)
            for t in range(2):
                _softmax_stream_update(_dot_nt(qm_ref[2 * h + t], kb) + bias, vb, m_ref, l_ref, acc_ref, 2 * h + t)
        return carry

    lax.fori_loop(0, nk, body, 0)
    lam = c_ref[0:1, :]
    for h in range(DIFF_HEADS):
        out = acc_ref[2 * h] / l_ref[2 * h] - lam * (acc_ref[2 * h + 1] / l_ref[2 * h + 1])
        y = out * lax.rsqrt(jnp.mean(out * out, axis=-1, keepdims=True) + 1e-6)
        o_ref[0, :, LANES * h:LANES * (h + 1)] = ((y * c_ref[1:2, :]) * c_ref[2:3, :]).astype(o_ref.dtype)


def _t5_saturation_distance():
    half = T5_BUCKETS // 2
    exact = half // 2
    n = np.arange(1, 4 * T5_MAX_DISTANCE, dtype=np.float32)
    large = exact + (np.log(n / exact) / math.log(T5_MAX_DISTANCE / exact) * (half - exact)).astype(np.int32)
    return int(np.argmax(large >= half - 1)) + 1 + 8


def _diff_bias_strips(t5_cols, tq, tk):
    unit = math.gcd(tq, tk)
    sat = _t5_saturation_distance()
    delta_lo = -unit * ((sat + tk - 1) // unit + 1)
    delta_hi = unit * ((sat + tq - 1) // unit + 1)
    width = delta_hi - delta_lo + tk
    rel = jnp.arange(tq - 1 + width) - (tq - 1) + delta_lo
    u = (t5_cols.astype(F32) * LOG2E)[_t5_bucket(rel)].T
    strip = _toeplitz(u, tq, width)
    nh = strip.shape[0]
    return strip.reshape(nh, tq, width // LANES, LANES).transpose(0, 2, 1, 3), delta_lo, delta_hi


def _diff_attention(hd, strips, consts, delta_lo, delta_hi, batch, seq, tq, tk):
    w = DIFF_HEADS * LANES
    once = pl.Buffered(1)
    state = pltpu.VMEM((2 * DIFF_HEADS, tq, LANES), F32)
    return pl.pallas_call(
        functools.partial(_diff_kernel, tk=tk, nk=seq // tk, delta_lo=delta_lo, delta_hi=delta_hi),
        grid=(batch, seq // tq),
        in_specs=[pl.BlockSpec((1, tq, w), lambda b, i: (b, i, 0)),
                  pl.BlockSpec((1, seq, w), lambda b, i: (b, 0, 1), pipeline_mode=once),
                  pl.BlockSpec((1, seq, w), lambda b, i: (b, 0, 2), pipeline_mode=once),
                  pl.BlockSpec(strips.shape, lambda b, i: (0, 0, 0, 0), pipeline_mode=once),
                  pl.BlockSpec(consts.shape, lambda b, i: (0, 0))],
        out_specs=pl.BlockSpec((1, tq, w), lambda b, i: (b, i, 0)),
        out_shape=jax.ShapeDtypeStruct((batch, seq, w), BF16),
        scratch_shapes=[pltpu.VMEM((2 * DIFF_HEADS, tq, LANES), BF16), state, state, state],
        compiler_params=_cparams(("arbitrary", "arbitrary"), 56),
        name="diff_attention",
    )(hd, hd, hd, strips, consts)


def _layer_norm(z, g, b):
    mu = jnp.mean(z, axis=-1, keepdims=True)
    zc = z - mu
    var = jnp.mean(zc * zc, axis=-1, keepdims=True)
    return zc * lax.rsqrt(var + 1e-5) * g + b


def _split3(a):
    hi = a.astype(BF16)
    r1 = a - hi.astype(F32)
    mid = r1.astype(BF16)
    lo = (r1 - mid.astype(F32)).astype(BF16)
    return hi, mid, lo


def _merge_kernel(x_ref, ya_ref, yb_ref, o0_ref, o1_ref, o2_ref, l0_ref, l1_ref, l2_ref, yd_ref, g_ref,
                  wna_ref, wgqa_ref, wdil_ref, wdiff_ref, wo_ref, wr_ref, ln_ref,
                  x1_ref, x1b_ref, lg_ref, *, alpha, d_model):
    def sig(t):
        return 1.0 / (1.0 + jnp.exp(-t.astype(F32)))

    l0, l1, l2 = l0_ref[...], l1_ref[...], l2_ref[...]
    mx = jnp.maximum(jnp.maximum(l0, l1), l2)
    e0, e1, e2 = jnp.exp(l0 - mx), jnp.exp(l1 - mx), jnp.exp(l2 - mx)
    den = e0 + e1 + e2
    yc = (e0 / den) * o0_ref[...] + (e1 / den) * o1_ref[...] + (e2 / den) * o2_ref[...]
    d = d_model
    merged = sig(g_ref[:, 0:d]) * jnp.dot(ya_ref[...], wna_ref[...], preferred_element_type=F32)
    merged += sig(g_ref[:, d:2 * d]) * jnp.dot(yb_ref[...], wgqa_ref[...], preferred_element_type=F32)
    merged += sig(g_ref[:, 2 * d:3 * d]) * jnp.dot(yc.astype(BF16), wdil_ref[...], preferred_element_type=F32)
    merged += sig(g_ref[:, 3 * d:4 * d]) * jnp.dot(yd_ref[...], wdiff_ref[...], preferred_element_type=F32)
    y = jnp.dot(merged.astype(BF16), wo_ref[...], preferred_element_type=F32)
    x1 = _layer_norm(alpha * x_ref[...] + y, ln_ref[0:1, :], ln_ref[1:2, :])
    x1_ref[...] = x1
    x1b_ref[...] = x1.astype(BF16)
    xh, xm, _ = _split3(x1)
    wh, wm, _ = wr_ref[0], wr_ref[1], wr_ref[2]
    lg_ref[...] = _dot_nt(wh, xh) + (_dot_nt(wh, xm) + _dot_nt(wm, xh))


def _merge(x, ya, yb, dil, yd, gates, w_na, w_gqa, w_dil, w_diff, w_o, w_r3, ln, alpha, tm):
    n, d = x.shape
    e = w_r3.shape[1]
    (o0, l0), (o1, l1), (o2, l2) = dil
    row = lambda w: pl.BlockSpec((tm, w), lambda i: (i, 0))
    full = lambda a: pl.BlockSpec(a.shape, lambda i: (0,) * a.ndim)
    return pl.pallas_call(
        functools.partial(_merge_kernel, alpha=alpha, d_model=d),
        grid=(n // tm,),
        in_specs=[row(d), row(ya.shape[1]), row(yb.shape[1]), row(LANES), row(LANES), row(LANES),
                  row(LANES), row(LANES), row(LANES), row(yd.shape[1]), row(gates.shape[1]),
                  full(w_na), full(w_gqa), full(w_dil), full(w_diff), full(w_o), full(w_r3), full(ln)],
        out_specs=[row(d), row(d), pl.BlockSpec((e, tm), lambda i: (0, i))],
        out_shape=[jax.ShapeDtypeStruct((n, d), F32), jax.ShapeDtypeStruct((n, d), BF16),
                   jax.ShapeDtypeStruct((e, n), F32)],
        compiler_params=_cparams(("arbitrary",), 48),
        name="branch_merge",
    )(x, ya, yb, o0, o1, o2, l0, l1, l2, yd, gates, w_na, w_gqa, w_dil, w_diff, w_o, w_r3, ln)


def _route_kernel(lg_ref, aff_ref, posm_ref, posx_ref, *, cap):
    lg = lg_ref[0]
    n_e, seq = lg.shape
    mx = jnp.max(lg, axis=0, keepdims=True)
    ex = jnp.exp(lg - mx)
    aff = ex / jnp.sum(ex, axis=0, keepdims=True)
    aff_ref[0] = aff
    bits = pltpu.bitcast(aff, jnp.int32)

    def bisect(_, carry):
        lo, hi = carry
        mid = lo + ((hi - lo + 1) >> 1)
        cnt = jnp.sum((bits >= mid).astype(F32), axis=1, keepdims=True)
        ok = cnt >= float(cap)
        return jnp.where(ok, mid, lo), jnp.where(ok, hi, mid - 1)

    lo0 = jnp.zeros((n_e, 1), jnp.int32)
    hi0 = jnp.full((n_e, 1), 0x7F800000, jnp.int32)
    thr, _ = lax.fori_loop(0, 31, bisect, (lo0, hi0))
    gt = bits > thr
    eq = bits == thr
    needf = float(cap) - jnp.sum(gt.astype(F32), axis=1, keepdims=True)

    tri = (lax.broadcasted_iota(jnp.int32, (LANES, LANES), 0)
           < lax.broadcasted_iota(jnp.int32, (LANES, LANES), 1)).astype(BF16)

    def prefix(mask_bf16, j, carry):
        blk = mask_bf16[:, j * LANES:(j + 1) * LANES]
        excl = jnp.dot(blk, tri, preferred_element_type=F32) + carry
        return excl, carry + jnp.sum(blk.astype(F32), axis=1, keepdims=True)

    eqb = eq.astype(BF16)
    carry_eq = jnp.zeros((n_e, 1), F32)
    carry_sel = jnp.zeros((n_e, 1), F32)
    for j in range(seq // LANES):
        cols = slice(j * LANES, (j + 1) * LANES)
        rank, carry_eq = prefix(eqb, j, carry_eq)
        sel = jnp.logical_or(gt[:, cols], jnp.logical_and(eq[:, cols], rank < needf))
        selb = sel.astype(BF16)
        excl = jnp.dot(selb, tri, preferred_element_type=F32) + carry_sel
        carry_sel = carry_sel + jnp.sum(selb.astype(F32), axis=1, keepdims=True)
        pos = excl.astype(jnp.int32)
        posx_ref[0, :, cols] = pos
        posm_ref[0, :, cols] = jnp.where(sel, pos, -1)


def _route(logits_t, batch, seq, cap):
    n_e = logits_t.shape[0]
    lg = logits_t.reshape(n_e, batch, seq).transpose(1, 0, 2)
    spec = pl.BlockSpec((1, n_e, seq), lambda b: (b, 0, 0))
    return pl.pallas_call(
        functools.partial(_route_kernel, cap=cap),
        grid=(batch,),
        in_specs=[spec],
        out_specs=[spec, spec, spec],
        out_shape=[jax.ShapeDtypeStruct((batch, n_e, seq), F32), jax.ShapeDtypeStruct((batch, n_e, seq), jnp.int32),
                   jax.ShapeDtypeStruct((batch, n_e, seq), jnp.int32)],
        compiler_params=_cparams(("arbitrary",), 32),
        name="ec_route",
    )(lg)


def _gather_kernel(clo_ref, chi_ref, x_ref, pos_ref, aff_ref, xin_ref, gs_ref, acc_ref, *, n_e, nsb):
    base = (pl.program_id(0) * n_e + pl.program_id(1)) * nsb
    for sb in range(nsb):
        slot_ids = sb * SLOT_BLOCK + lax.broadcasted_iota(jnp.int32, (SLOT_BLOCK, TOKEN_CHUNK), 0)
        acc_ref[...] = jnp.zeros_like(acc_ref)

        def body(c, g, slot_ids=slot_ids):
            hit = pos_ref[0, 0, pl.ds(c, 1), :] == slot_ids
            xc = x_ref[0, pl.ds(pl.multiple_of(c * TOKEN_CHUNK, TOKEN_CHUNK), TOKEN_CHUNK), :]
            acc_ref[...] += jnp.dot(hit.astype(BF16), xc, preferred_element_type=F32)
            return g + jnp.sum(jnp.where(hit, aff_ref[0, 0, pl.ds(c, 1), :], 0.0), axis=-1, keepdims=True)

        g = lax.fori_loop(clo_ref[base + sb], chi_ref[base + sb], body, jnp.zeros((SLOT_BLOCK, 1), F32))
        rows = slice(sb * SLOT_BLOCK, (sb + 1) * SLOT_BLOCK)
        xin_ref[0, 0, rows, :] = acc_ref[...].astype(BF16)
        gs_ref[0, 0, rows, :] = jnp.broadcast_to(g, (SLOT_BLOCK, LANES))


def _gather(x1b, posm, aff, clo, chi, cap):
    batch, seq, d = x1b.shape
    n_e = posm.shape[1]
    nsb = cap // SLOT_BLOCK
    nch = seq // TOKEN_CHUNK
    pos4 = posm.reshape(batch, n_e, nch, TOKEN_CHUNK)
    aff4 = aff.reshape(batch, n_e, nch, TOKEN_CHUNK)
    grid_spec = pltpu.PrefetchScalarGridSpec(
        num_scalar_prefetch=2,
        grid=(batch, n_e),
        in_specs=[pl.BlockSpec((1, seq, d), lambda b, e, *_: (b, 0, 0)),
                  pl.BlockSpec((1, 1, nch, TOKEN_CHUNK), lambda b, e, *_: (b, e, 0, 0)),
                  pl.BlockSpec((1, 1, nch, TOKEN_CHUNK), lambda b, e, *_: (b, e, 0, 0))],
        out_specs=[pl.BlockSpec((1, 1, cap, d), lambda b, e, *_: (b, e, 0, 0)),
                   pl.BlockSpec((1, 1, cap, LANES), lambda b, e, *_: (b, e, 0, 0))],
        scratch_shapes=[pltpu.VMEM((SLOT_BLOCK, d), F32)],
    )
    return pl.pallas_call(
        functools.partial(_gather_kernel, n_e=n_e, nsb=nsb),
        grid_spec=grid_spec,
        out_shape=[jax.ShapeDtypeStruct((batch, n_e, cap, d), BF16),
                   jax.ShapeDtypeStruct((batch, n_e, cap, LANES), F32)],
        compiler_params=_cparams(("arbitrary", "arbitrary"), 56),
        name="ec_gather",
    )(clo, chi, x1b, pos4, aff4)


def _ffn_kernel(x_ref, gs_ref, wg_ref, wu_ref, wd_ref, o_ref, *, fchunk):
    x = x_ref[0, 0]
    ff = wg_ref.shape[2]
    acc = jnp.zeros((x.shape[0], wd_ref.shape[2]), F32)
    for f0 in range(0, ff, fchunk):
        g = jnp.dot(x, wg_ref[0, :, f0:f0 + fchunk], preferred_element_type=F32)
        u = jnp.dot(x, wu_ref[0, :, f0:f0 + fchunk], preferred_element_type=F32)
        hid = (g / (1.0 + jnp.exp(-g))) * u
        acc += jnp.dot(hid.astype(BF16), wd_ref[0, f0:f0 + fchunk, :], preferred_element_type=F32)
    o_ref[0, 0] = (acc * gs_ref[0, 0][:, 0:1]).astype(o_ref.dtype)


def _expert_ffn(xin, gslot, wg, wu, wd, tc, fchunk):
    batch, n_e, cap, d = xin.shape
    ff = wg.shape[2]
    return pl.pallas_call(
        functools.partial(_ffn_kernel, fchunk=fchunk),
        grid=(n_e, batch, cap // tc),
        in_specs=[pl.BlockSpec((1, 1, tc, d), lambda e, b, i: (b, e, i, 0)),
                  pl.BlockSpec((1, 1, tc, LANES), lambda e, b, i: (b, e, i, 0)),
                  pl.BlockSpec((1, d, ff), lambda e, b, i: (e, 0, 0)),
                  pl.BlockSpec((1, d, ff), lambda e, b, i: (e, 0, 0)),
                  pl.BlockSpec((1, ff, d), lambda e, b, i: (e, 0, 0))],
        out_specs=pl.BlockSpec((1, 1, tc, d), lambda e, b, i: (b, e, i, 0)),
        out_shape=jax.ShapeDtypeStruct((batch, n_e, cap, d), BF16),
        compiler_params=_cparams(("arbitrary", "arbitrary", "arbitrary"), 56),
        name="expert_ffn",
    )(xin, gslot, wg, wu, wd)


def _combine_kernel(cs_ref, yo_ref, post_ref, y_ref, *, n_e, nch, cap):
    b = pl.program_id(0)
    c = pl.program_id(2)
    win = 2 * SLOT_BLOCK
    pos_all = post_ref[0]
    win_ids = lax.broadcasted_iota(jnp.int32, (TOKEN_CHUNK, win), 1)
    hits, wins, tails = [], [], []
    for e in range(n_e):
        base = (b * n_e + e) * (nch + 1) + c
        s_lo = cs_ref[base]
        s_hi = cs_ref[base + 1]
        w0 = pl.multiple_of(jnp.minimum((s_lo >> SLOT_SHIFT) << SLOT_SHIFT, cap - win), SLOT_BLOCK)
        hits.append((pos_all[:, e:e + 1] == (w0 + win_ids)).astype(BF16))
        wins.append(yo_ref[0, e, pl.ds(w0, win), :])
        sb_end = jnp.where(s_hi > s_lo, ((s_hi - 1) >> SLOT_SHIFT) + 1, 0)
        tails.append(((w0 + win) >> SLOT_SHIFT, sb_end))
    y_ref[0] = jnp.dot(jnp.concatenate(hits, axis=1), jnp.concatenate(wins, axis=0), preferred_element_type=F32)
    for e, (sb_from, sb_end) in enumerate(tails):
        def body(sb, carry, e=e):
            blk_ids = lax.broadcasted_iota(jnp.int32, (TOKEN_CHUNK, SLOT_BLOCK), 1)
            hit = post_ref[0, :, e:e + 1] == (sb * SLOT_BLOCK + blk_ids)
            yb = yo_ref[0, e, pl.ds(pl.multiple_of(sb * SLOT_BLOCK, SLOT_BLOCK), SLOT_BLOCK), :]
            y_ref[0] += jnp.dot(hit.astype(BF16), yb, preferred_element_type=F32)
            return carry

        lax.fori_loop(sb_from, sb_end, body, 0)


def _combine(yo, pos_t, cs, td):
    batch, n_e, cap, d = yo.shape
    seq = pos_t.shape[1]
    nch = seq // TOKEN_CHUNK
    grid_spec = pltpu.PrefetchScalarGridSpec(
        num_scalar_prefetch=1,
        grid=(batch, d // td, nch),
        in_specs=[pl.BlockSpec((1, n_e, cap, td), lambda b, j, c, *_: (b, 0, 0, j)),
                  pl.BlockSpec((1, TOKEN_CHUNK, n_e), lambda b, j, c, *_: (b, c, 0))],
        out_specs=pl.BlockSpec((1, TOKEN_CHUNK, td), lambda b, j, c, *_: (b, c, j)),
    )
    return pl.pallas_call(
        functools.partial(_combine_kernel, n_e=n_e, nch=nch, cap=cap),
        grid_spec=grid_spec,
        out_shape=jax.ShapeDtypeStruct((batch, seq, d), F32),
        compiler_params=_cparams(("arbitrary", "arbitrary", "arbitrary"), 56),
        name="ec_combine",
    )(cs, yo, pos_t)


def _ple_kernel(x1_ref, y_ref, p_ref, wpg_ref, wpp_ref, ln_ref, x3_ref, x3b_ref, *, alpha):
    x2 = _layer_norm(alpha * x1_ref[...] + y_ref[...], ln_ref[0:1, :], ln_ref[1:2, :])
    gate = 1.0 / (1.0 + jnp.exp(-jnp.dot(x2.astype(BF16), wpg_ref[...], preferred_element_type=F32)))
    emb = jnp.dot(p_ref[...].astype(BF16), wpp_ref[...], preferred_element_type=F32) * gate
    x3 = _layer_norm(alpha * x2 + emb, ln_ref[2:3, :], ln_ref[3:4, :])
    x3_ref[...] = x3
    x3b_ref[...] = x3.astype(BF16)


def _ple(x1, y, p, w_pg, w_pp, ln, alpha, tm):
    n, d = x1.shape
    row = lambda w: pl.BlockSpec((tm, w), lambda i: (i, 0))
    full = lambda a: pl.BlockSpec(a.shape, lambda i: (0,) * a.ndim)
    return pl.pallas_call(
        functools.partial(_ple_kernel, alpha=alpha),
        grid=(n // tm,),
        in_specs=[row(d), row(d), row(p.shape[1]), full(w_pg), full(w_pp), full(ln)],
        out_specs=[row(d), row(d)],
        out_shape=[jax.ShapeDtypeStruct((n, d), F32), jax.ShapeDtypeStruct((n, d), BF16)],
        compiler_params=_cparams(("arbitrary",), 40),
        name="ple_norm",
    )(x1, y, p, w_pg, w_pp, ln)


def _token_mixer_inputs(xb, w_in, qk_gain, batch, seq):
    d = w_in.shape[0]
    scale = HEAD_DIM ** -0.5
    na_w = NA_HEADS * HEAD_DIM
    gq_w, gkv_w = GQA_HEADS * HEAD_DIM, GQA_KV_HEADS * HEAD_DIM
    dil_w = DIL_HEADS * HEAD_DIM
    dqk_w = DIFF_HEADS * 2 * HEAD_DIM
    o = 0
    w_a = w_in[:, o:o + 3 * na_w]; o += 3 * na_w
    w_b = w_in[:, o:o + gq_w + 2 * gkv_w]; o += gq_w + 2 * gkv_w
    w_c = w_in[:, o:o + 3 * dil_w]; o += 3 * dil_w
    w_d = w_in[:, o:o + 3 * dqk_w]; o += 3 * dqk_w
    w_g = w_in[:, o:]

    def qscaled(w, qw):
        return jnp.concatenate([w[:, :qw] * scale, w[:, qw:]], axis=1).astype(BF16)

    n = batch * seq
    ha = _matmul(xb, qscaled(w_a, na_w), BF16, 1024, 3 * na_w, "proj_na").reshape(batch, seq, -1)
    hb = _matmul(xb, w_b.astype(BF16), F32, 1024, w_b.shape[1], "proj_gqa").reshape(batch, seq, -1)
    hc = _matmul(xb, qscaled(w_c, dil_w), BF16, 1024, 3 * dil_w, "proj_dil").reshape(batch, seq, -1)
    w_dq = jnp.concatenate([w_d[:, :dqk_w] * (scale * LOG2E), w_d[:, dqk_w:]], axis=1).astype(BF16)
    hd = _matmul(xb, w_dq, BF16, 1024, 3 * dqk_w, "proj_diff").reshape(batch, seq, -1)
    gates = _matmul(xb, w_g.astype(BF16), BF16, 1024, 1024, "proj_gates")
    cos, sin = _rope_tables(seq)
    qn = _rms_rope(hb[..., :gq_w].reshape(batch, seq, GQA_HEADS, HEAD_DIM), qk_gain[0] * (scale * LOG2E), cos, sin)
    kn = _rms_rope(hb[..., gq_w:gq_w + gkv_w].reshape(batch, seq, GQA_KV_HEADS, HEAD_DIM), qk_gain[1], cos, sin)
    vb = hb[..., gq_w + gkv_w:].reshape(batch, seq, GQA_KV_HEADS, HEAD_DIM)
    dup = lambda t: jnp.concatenate([t, t], axis=-1).reshape(batch, seq, 2 * gkv_w).astype(BF16)
    return ha, qn.reshape(batch, seq, gq_w).astype(BF16), dup(kn), dup(vb), hc, hd, gates


def _layer(x, xb, p, w, lam_init, alpha):
    batch, seq, d = x.shape
    n = batch * seq
    ha, qn, kd, vd, hc, hd, gates = _token_mixer_inputs(xb.reshape(n, d), w["w_in"], w["qk_gain"], batch, seq)

    ya = _na_attention(ha, _na_bias_tiles(w["rpb"]), batch, seq)
    yb = _gqa_attention(qn, kd, vd, ATTN_TQ, ATTN_TK)
    dil = []
    for g, (_, dilation) in enumerate(DIL_GROUPS):
        bias = _dil_bias_tiles(w["t5"][:, 2 * g:2 * g + 2], dilation, 128)
        dil.append(_dilated_group(hc, bias, g, dilation, batch, seq, 128))
    lp = w["lam_params"].astype(F32)
    lam = jnp.exp(jnp.sum(lp[0] * lp[1])) - jnp.exp(jnp.sum(lp[2] * lp[3])) + lam_init
    consts = jnp.zeros((8, LANES), F32).at[0].set(lam).at[1].set(w["diff_gain"]).at[2].set(1.0 - lam_init)
    strips, delta_lo, delta_hi = _diff_bias_strips(w["t5"][:, DIL_HEADS:], DIFF_BIAS_ROWS, ATTN_TK)
    yd = _diff_attention(hd, strips, consts, delta_lo, delta_hi, batch, seq, ATTN_TQ, ATTN_TK)

    flat = lambda t: t.reshape(n, t.shape[-1])
    w_r3 = jnp.stack(_split3(w["w_router"].T))
    ln = w["ln"]
    x1, x1b, logits_t = _merge(
        x.reshape(n, d), flat(ya), flat(yb), [(flat(o), flat(l)) for o, l in dil], flat(yd), gates,
        w["w_na"].astype(BF16), w["w_gqa"].astype(BF16), w["w_dil"].astype(BF16), w["w_diff"].astype(BF16),
        w["w_o"].astype(BF16), w_r3, jnp.stack([ln[0][0], ln[1][0]]), alpha, 256)

    cap = EC_CAPACITY * seq // N_EXPERTS
    aff, posm, posx = _route(logits_t, batch, seq, cap)
    nch = seq // TOKEN_CHUNK
    cs = jnp.concatenate([posx[:, :, ::TOKEN_CHUNK], jnp.full((batch, N_EXPERTS, 1), cap, jnp.int32)], axis=-1)
    edges = jnp.arange(cap // SLOT_BLOCK, dtype=jnp.int32) * SLOT_BLOCK
    clo = jnp.sum(cs[:, :, 1:, None] <= edges, axis=2).astype(jnp.int32)
    chi = jnp.sum(cs[:, :, :nch, None] < edges + SLOT_BLOCK, axis=2).astype(jnp.int32)
    xin, gslot = _gather(x1b.reshape(batch, seq, d), posm, aff, clo.reshape(-1), chi.reshape(-1), cap)
    yo = _expert_ffn(xin, gslot, w["w_eg"].astype(BF16), w["w_eu"].astype(BF16), w["w_ed"].astype(BF16), 512, 512)
    y = _combine(yo, posm.transpose(0, 2, 1), cs.reshape(-1), 512)

    ln4 = jnp.stack([ln[0][1], ln[1][1], ln[0][2], ln[1][2]])
    x3, x3b = _ple(x1, y.reshape(n, d), p.reshape(n, -1), w["w_pg"].astype(BF16), w["w_pp"].astype(BF16), ln4, alpha, 512)
    return x3.reshape(batch, seq, d), x3b.reshape(batch, seq, d)


def kernel(x, p, w_in, w_branch_na, w_branch_gqa, w_branch_dil, w_branch_diff, w_out, na_rel_bias, qk_norm_gain,
           diff_lambda, diff_norm_gain, t5_rel_bias, w_router, w_expert_gate, w_expert_up, w_expert_down,
           w_ple_proj, w_ple_gate, ln_gain, ln_bias):
    depth = w_in.shape[0]
    alpha = (2 * depth) ** 0.25
    xb = x.astype(BF16)
    for i in range(depth):
        lam_init = 0.8 - 0.6 * math.exp(-0.3 * i)
        w = dict(w_in=w_in[i], w_na=w_branch_na[i], w_gqa=w_branch_gqa[i], w_dil=w_branch_dil[i],
                 w_diff=w_branch_diff[i], w_o=w_out[i], rpb=na_rel_bias[i], qk_gain=qk_norm_gain[i],
                 lam_params=diff_lambda[i], diff_gain=diff_norm_gain[i], t5=t5_rel_bias, w_router=w_router[i],
                 w_eg=w_expert_gate[i], w_eu=w_expert_up[i], w_ed=w_expert_down[i], w_pp=w_ple_proj[i],
                 w_pg=w_ple_gate[i], ln=(ln_gain[i], ln_bias[i]))
        x, xb = _layer(x, xb, p[i], w, lam_init, alpha)
    return x
```

```python
import functools
import math

import numpy as np
import jax
import jax.numpy as jnp
from jax import lax
from jax.experimental import pallas as pl
from jax.experimental.pallas import tpu as pltpu

F32 = jnp.float32
BF16 = jnp.bfloat16
NEG_INF = -1e30

V7X_VMEM_BYTES = 64 * 1024 * 1024
LANES = 128
LANE_SHIFT = 7

GRID_W = 64
HEAD_DIM = 64
NA_HEADS, NA_ROWS, NA_COLS = 6, 8, 16
NA_ROWS_PER_STEP = 4
LOG2E = 1.4426950408889634
ATTN_TQ, ATTN_TK = 1024, 256
DIL_BLOCKS_PER_STEP = 4
DIFF_BIAS_ROWS = 256
GQA_HEADS, GQA_KV_HEADS = 6, 2
ROPE_THETA = 10000.0
DIL_GROUPS = ((128, 1), (512, 4), (2048, 16))
DIL_HEADS = 6
DIFF_HEADS = 4
T5_BUCKETS, T5_MAX_DISTANCE = 32, 1024
N_EXPERTS, EC_CAPACITY = 16, 2
SLOT_BLOCK = 128
SLOT_SHIFT = SLOT_BLOCK.bit_length() - 1
TOKEN_CHUNK = 256


def _cparams(semantics, vmem_mib):
    return pltpu.CompilerParams(dimension_semantics=semantics,
                                vmem_limit_bytes=min(vmem_mib * 1024 * 1024, V7X_VMEM_BYTES - 4 * 1024 * 1024))


def _mm_kernel(a_ref, b_ref, o_ref):
    o_ref[...] = jnp.dot(a_ref[...], b_ref[...], preferred_element_type=F32).astype(o_ref.dtype)


def _matmul(a, b, out_dtype, tm, tn, name):
    m, k = a.shape
    n = b.shape[1]
    return pl.pallas_call(
        _mm_kernel,
        grid=(n // tn, m // tm),
        in_specs=[pl.BlockSpec((tm, k), lambda j, i: (i, 0)),
                  pl.BlockSpec((k, tn), lambda j, i: (0, j))],
        out_specs=pl.BlockSpec((tm, tn), lambda j, i: (i, j)),
        out_shape=jax.ShapeDtypeStruct((m, n), out_dtype),
        compiler_params=_cparams(("arbitrary", "arbitrary"), 40),
        name=name,
    )(a, b)


def _toeplitz(u, rows, width):
    lead = u.shape[:-1]
    period = rows + width
    text = jnp.concatenate([u[..., rows - 1:], jnp.zeros(lead + (1,), u.dtype), u[..., :rows - 1]], axis=-1)
    flat = jnp.tile(text, (1,) * len(lead) + (rows,))[..., :rows * (period - 1)]
    return flat.reshape(lead + (rows, period - 1))[..., :width]


def _lane_lo(rows):
    return lax.broadcasted_iota(jnp.int32, (rows, LANES), 1) < HEAD_DIM


def _dot_nt(a, b):
    return lax.dot_general(a, b, (((1,), (1,)), ((), ())), preferred_element_type=F32)


NA_WIN_ROWS = NA_ROWS + NA_ROWS_PER_STEP


def _na_kernel(q_ref, k_ref, v_ref, tp_ref, o_ref, *, rows):
    r0 = pl.program_id(1) * NA_ROWS_PER_STEP
    ws = jnp.clip(r0 - NA_ROWS // 2, 0, rows - NA_WIN_ROWS)
    start = pl.multiple_of(ws * GRID_W, GRID_W)
    win = NA_WIN_ROWS * GRID_W
    npair = NA_WIN_ROWS // 2
    none_code = 2 * (2 * NA_ROWS - 1)
    codes = []
    for rr in range(NA_ROWS_PER_STEP):
        r = r0 + rr
        rs = jnp.clip(r - NA_ROWS // 2, 0, rows - NA_ROWS)
        row_codes = []
        for j in range(npair):
            a0 = ws + 2 * j
            d0 = a0 - r + (NA_ROWS - 1)
            vl = jnp.logical_and(a0 >= rs, a0 < rs + NA_ROWS)
            vr = jnp.logical_and(a0 + 1 >= rs, a0 + 1 < rs + NA_ROWS)
            row_codes.append(jnp.where(jnp.logical_and(vl, vr), d0,
                                       jnp.where(jnp.logical_or(vl, vr), 2 * NA_ROWS - 1 + d0, none_code)))
        codes.append(row_codes)
    tq = NA_ROWS_PER_STEP * GRID_W
    lo = _lane_lo(tq)
    for p in range(NA_HEADS // 2):
        cols = slice(LANES * p, LANES * (p + 1))
        qp = q_ref[0, :, cols]
        kw = k_ref[0, pl.ds(start, win), cols]
        vw = v_ref[0, pl.ds(start, win), cols]
        outs = []
        for hh in range(2):
            h = 2 * p + hh
            qm = jnp.where(lo if hh == 0 else jnp.logical_not(lo), qp, jnp.zeros_like(qp))
            bias = jnp.concatenate(
                [jnp.concatenate([tp_ref[h, c] for c in row_codes], axis=-1) for row_codes in codes], axis=0)
            s = _dot_nt(qm, kw) + bias
            m = jnp.max(s, axis=-1, keepdims=True)
            e = jnp.exp(s - m)
            pr = (e / jnp.sum(e, axis=-1, keepdims=True)).astype(BF16)
            outs.append(jnp.dot(pr, vw, preferred_element_type=F32))
        o_ref[0, :, cols] = jnp.where(lo, outs[0], outs[1]).astype(o_ref.dtype)


def _na_bias_tiles(rpb):
    col = jnp.arange(GRID_W)
    cstart = jnp.clip(col - NA_COLS // 2, 0, GRID_W - NA_COLS)
    kc = jnp.arange(GRID_W)
    inwin = (kc[None, :] >= cstart[:, None]) & (kc[None, :] < cstart[:, None] + NA_COLS)
    pad = GRID_W - NA_COLS
    u = jnp.pad(rpb.astype(F32), ((0, 0), (0, 0), (pad, pad)))
    t = jnp.where(inwin[None, None], _toeplitz(u, GRID_W, GRID_W), NEG_INF)
    neg = jnp.full_like(t[:, :NA_ROWS], NEG_INF)
    both = jnp.concatenate([t[:, :-1], t[:, 1:]], axis=-1)
    right = jnp.concatenate([neg, t[:, :NA_ROWS]], axis=-1)
    left = jnp.concatenate([t[:, NA_ROWS - 1:], neg], axis=-1)
    none = jnp.concatenate([neg[:, :1], neg[:, :1]], axis=-1)
    return jnp.concatenate([both, right, left, none], axis=1)


def _na_attention(ha, t2, batch, seq):
    rows = seq // GRID_W
    w = NA_HEADS * HEAD_DIM
    tq = GRID_W * NA_ROWS_PER_STEP
    return pl.pallas_call(
        functools.partial(_na_kernel, rows=rows),
        grid=(batch, rows // NA_ROWS_PER_STEP),
        in_specs=[pl.BlockSpec((1, tq, w), lambda b, r: (b, r, 0)),
                  pl.BlockSpec((1, seq, w), lambda b, r: (b, 0, 1)),
                  pl.BlockSpec((1, seq, w), lambda b, r: (b, 0, 2)),
                  pl.BlockSpec(t2.shape, lambda b, r: (0, 0, 0, 0))],
        out_specs=pl.BlockSpec((1, tq, w), lambda b, r: (b, r, 0)),
        out_shape=jax.ShapeDtypeStruct((batch, seq, w), BF16),
        compiler_params=_cparams(("arbitrary", "arbitrary"), 56),
        name="na_attention",
    )(ha, ha, ha, t2)


def _softmax_stream_update(s, vb, m_ref, l_ref, acc_ref, idx):
    m_prev = m_ref[idx]
    m_new = jnp.maximum(m_prev, jnp.max(s, axis=-1, keepdims=True))
    alpha = jnp.exp2(m_prev - m_new)
    p = jnp.exp2(s - jnp.tile(m_new, (1, s.shape[1] // LANES)))
    m_ref[idx] = m_new
    l_ref[idx] = alpha * l_ref[idx] + jnp.sum(p, axis=-1, keepdims=True)
    acc_ref[idx] = alpha * acc_ref[idx] + jnp.dot(p.astype(BF16), vb, preferred_element_type=F32)


def _init_streams(m_ref, l_ref, acc_ref):
    m_ref[...] = jnp.full(m_ref.shape, NEG_INF, F32)
    l_ref[...] = jnp.zeros(l_ref.shape, F32)
    acc_ref[...] = jnp.zeros(acc_ref.shape, F32)


def _gqa_kernel(q_ref, k_ref, v_ref, o_ref, qm_ref, m_ref, l_ref, acc_ref, *, tk, nk):
    tq = q_ref.shape[1]
    lo = _lane_lo(tq)
    rep = GQA_HEADS // GQA_KV_HEADS
    for h in range(GQA_HEADS):
        qp = q_ref[0, :, LANES * (h // 2):LANES * (h // 2 + 1)]
        qm_ref[h] = jnp.where(lo if h % 2 == 0 else jnp.logical_not(lo), qp, jnp.zeros_like(qp))
    _init_streams(m_ref, l_ref, acc_ref)

    def body(j, carry):
        ks = pl.multiple_of(j * tk, tk)
        for g in range(GQA_KV_HEADS):
            gcols = slice(LANES * g, LANES * (g + 1))
            kb = k_ref[0, pl.ds(ks, tk), gcols]
            vb = v_ref[0, pl.ds(ks, tk), gcols]
            for h in range(rep * g, rep * (g + 1)):
                _softmax_stream_update(_dot_nt(qm_ref[h], kb), vb, m_ref, l_ref, acc_ref, h)
        return carry

    lax.fori_loop(0, nk, body, 0)
    for p in range(GQA_HEADS // 2):
        o0 = acc_ref[2 * p] / l_ref[2 * p]
        o1 = acc_ref[2 * p + 1] / l_ref[2 * p + 1]
        o_ref[0, :, LANES * p:LANES * (p + 1)] = jnp.where(lo, o0, o1).astype(o_ref.dtype)


def _gqa_attention(q, kd, vd, tq, tk):
    batch, seq, wq = q.shape
    wk = kd.shape[-1]
    state = pltpu.VMEM((GQA_HEADS, tq, LANES), F32)
    return pl.pallas_call(
        functools.partial(_gqa_kernel, tk=tk, nk=seq // tk),
        grid=(batch, seq // tq),
        in_specs=[pl.BlockSpec((1, tq, wq), lambda b, i: (b, i, 0)),
                  pl.BlockSpec((1, seq, wk), lambda b, i: (b, 0, 0)),
                  pl.BlockSpec((1, seq, wk), lambda b, i: (b, 0, 0))],
        out_specs=pl.BlockSpec((1, tq, wq), lambda b, i: (b, i, 0)),
        out_shape=jax.ShapeDtypeStruct((batch, seq, wq), BF16),
        scratch_shapes=[pltpu.VMEM((GQA_HEADS, tq, LANES), BF16), state, state, state],
        compiler_params=_cparams(("arbitrary", "arbitrary"), 48),
        name="gqa_attention",
    )(q, kd, vd)


def _rope_tables(seq):
    quarter = HEAD_DIM // 4
    freqs = ROPE_THETA ** (-jnp.arange(quarter, dtype=F32) / quarter)
    t = jnp.arange(seq)
    row = (t // GRID_W).astype(F32)
    col = (t % GRID_W).astype(F32)
    ang_r = row[:, None] * freqs[None, :]
    ang_c = col[:, None] * freqs[None, :]
    cos = jnp.concatenate([jnp.cos(ang_r)] * 2 + [jnp.cos(ang_c)] * 2, axis=-1)
    sin = jnp.concatenate([-jnp.sin(ang_r), jnp.sin(ang_r), -jnp.sin(ang_c), jnp.sin(ang_c)], axis=-1)
    return cos, sin


def _proj_gqa_kernel(x_ref, w_ref, cos_ref, sin_ref, gain_ref, mean_ref, q_ref, k_ref, v_ref):
    h = jnp.dot(x_ref[...], w_ref[...], preferred_element_type=F32)
    nq = q_ref.shape[1] // LANES
    nk = k_ref.shape[1] // LANES
    lane = lax.broadcasted_iota(jnp.int32, (h.shape[0], LANES), 1)
    first = (lane & (HEAD_DIM // 2 - 1)) < HEAD_DIM // 4
    for blk in range(nq + nk):
        t = h[:, LANES * blk:LANES * (blk + 1)]
        hi, mid, low = _split3(t * t)
        ms = (jnp.dot(hi, mean_ref[...], preferred_element_type=F32)
              + jnp.dot(mid, mean_ref[...], preferred_element_type=F32)
              + jnp.dot(low, mean_ref[...], preferred_element_type=F32))
        y = t * lax.rsqrt(ms + 1e-6) * gain_ref[blk:blk + 1, :]
        partner = jnp.where(first, pltpu.roll(y, LANES - HEAD_DIM // 4, 1), pltpu.roll(y, HEAD_DIM // 4, 1))
        out = (y * cos_ref[...] + partner * sin_ref[...]).astype(BF16)
        if blk < nq:
            q_ref[:, LANES * blk:LANES * (blk + 1)] = out
        else:
            k_ref[:, LANES * (blk - nq):LANES * (blk - nq + 1)] = out
    v_ref[...] = h[:, LANES * (nq + nk):].astype(BF16)


def _proj_gqa(xb, w_b, qk_gain, seq, tm):
    n, d = xb.shape
    gq, gkv = GQA_HEADS * HEAD_DIM, GQA_KV_HEADS * HEAD_DIM
    dup = lambda w: jnp.concatenate([w[:, i * HEAD_DIM:(i + 1) * HEAD_DIM] for i in range(GQA_KV_HEADS) for _ in range(2)], axis=1)
    w_all = jnp.concatenate([w_b[:, :gq], dup(w_b[:, gq:gq + gkv]), dup(w_b[:, gq + gkv:])], axis=1).astype(BF16)
    cos, sin = _rope_tables(seq)
    cos2, sin2 = jnp.tile(cos, (1, 2)), jnp.tile(sin, (1, 2))
    nq, nk = gq // LANES, 2 * gkv // LANES
    gain = jnp.concatenate([jnp.tile(qk_gain[0] * (HEAD_DIM ** -0.5 * LOG2E), (nq, 2)), jnp.tile(qk_gain[1], (nk, 2)),
                            jnp.zeros((8 - nq - nk, LANES), F32)], axis=0)
    head = jnp.arange(LANES) // HEAD_DIM
    mean_mat = jnp.where(head[:, None] == head[None, :], 1.0 / HEAD_DIM, 0.0).astype(BF16)
    nt = seq // tm
    row = lambda w: pl.BlockSpec((tm, w), lambda i: (i, 0))
    full = lambda a: pl.BlockSpec(a.shape, lambda i: (0,) * a.ndim)
    pos = pl.BlockSpec((tm, LANES), lambda i: (i % nt, 0))
    return pl.pallas_call(
        _proj_gqa_kernel,
        grid=(n // tm,),
        in_specs=[row(d), full(w_all), pos, pos, full(gain), full(mean_mat)],
        out_specs=[row(gq), row(2 * gkv), row(2 * gkv)],
        out_shape=[jax.ShapeDtypeStruct((n, gq), BF16), jax.ShapeDtypeStruct((n, 2 * gkv), BF16),
                   jax.ShapeDtypeStruct((n, 2 * gkv), BF16)],
        compiler_params=_cparams(("arbitrary",), 40),
        name="proj_gqa",
    )(xb, w_all, cos2, sin2, gain, mean_mat)


def _dil_kernel(q_ref, k_ref, v_ref, bias_ref, o_ref, lse_ref, *, sub_len, nblk, tq):
    half = DIL_GROUPS[0][0] // 2
    win = tq + 2 * half
    lo = _lane_lo(tq)
    for qq in range(q_ref.shape[1] // tq):
        i = pl.program_id(2) * (q_ref.shape[1] // tq) + qq
        start = pl.multiple_of(jnp.clip(i * tq - half, 0, sub_len - win), half)
        var = jnp.where(i == 0, 0, jnp.where(i == nblk - 1, 2, 1))
        rows = slice(qq * tq, (qq + 1) * tq)
        qp = q_ref[0, rows, :]
        kw = k_ref[0, pl.ds(start, win), :]
        vw = v_ref[0, pl.ds(start, win), :]
        outs, lses = [], []
        for hh in range(2):
            qm = jnp.where(lo if hh == 0 else jnp.logical_not(lo), qp, jnp.zeros_like(qp))
            s = _dot_nt(qm, kw) + bias_ref[hh, var]
            m = jnp.max(s, axis=-1, keepdims=True)
            e = jnp.exp(s - m)
            l = jnp.sum(e, axis=-1, keepdims=True)
            outs.append(jnp.dot((e / l).astype(BF16), vw, preferred_element_type=F32))
            lses.append(jnp.broadcast_to(m + jnp.log(l), (tq, LANES)))
        o_ref[0, rows, :] = jnp.where(lo, outs[0], outs[1])
        lse_ref[0, rows, :] = jnp.where(lo, lses[0], lses[1])


def _t5_bucket(rel):
    half = T5_BUCKETS // 2
    exact = half // 2
    n = jnp.abs(rel)
    nf = jnp.maximum(n, 1).astype(F32)
    large = exact + (jnp.log(nf / exact) / math.log(T5_MAX_DISTANCE / exact) * (half - exact)).astype(jnp.int32)
    large = jnp.minimum(large, half - 1)
    return jnp.where(rel > 0, half, 0) + jnp.where(n < exact, n, large)


def _dil_bias_tiles(t5_cols, dilation, tq):
    half = DIL_GROUPS[0][0] // 2
    win = tq + 2 * half
    wide = win + 2 * half
    j = jnp.arange(tq - 1 + wide) - (tq - 1) - 2 * half
    vals = t5_cols.astype(F32)[_t5_bucket(j * dilation)]
    u = jnp.where((jnp.abs(j) <= half)[:, None], vals, NEG_INF).T
    a = _toeplitz(u, tq, wide)
    return jnp.stack([a[:, :, 2 * half - s:2 * half - s + win] for s in (0, half, 2 * half)], axis=1)


def _dilated_group(hc, bias, g, dilation, batch, seq, tq, qb):
    sub_len = seq // dilation
    nblk = sub_len // tq
    ts = tq * qb
    nb = 3 * DIL_HEADS * HEAD_DIM // LANES
    view = hc.reshape(batch, sub_len, dilation * nb * LANES)
    nq = DIL_HEADS * HEAD_DIM // LANES
    out, lse = pl.pallas_call(
        functools.partial(_dil_kernel, sub_len=sub_len, nblk=nblk, tq=tq),
        grid=(batch, dilation, nblk // qb),
        in_specs=[pl.BlockSpec((1, ts, LANES), lambda b, r, i: (b, i, r * nb + g)),
                  pl.BlockSpec((1, sub_len, LANES), lambda b, r, i: (b, 0, r * nb + nq + g)),
                  pl.BlockSpec((1, sub_len, LANES), lambda b, r, i: (b, 0, r * nb + 2 * nq + g)),
                  pl.BlockSpec(bias.shape, lambda b, r, i: (0, 0, 0, 0))],
        out_specs=[pl.BlockSpec((1, ts, LANES), lambda b, r, i: (b, i, r)),
                   pl.BlockSpec((1, ts, LANES), lambda b, r, i: (b, i, r))],
        out_shape=[jax.ShapeDtypeStruct((batch, sub_len, dilation * LANES), F32)] * 2,
        compiler_params=_cparams(("arbitrary", "arbitrary", "arbitrary"), 32),
        name=f"dilated_attention_d{dilation}",
    )(view, view, view, bias)
    return out.reshape(batch, seq, LANES), lse.reshape(batch, seq, LANES)


def _diff_kernel(q_ref, k_ref, v_ref, strip_ref, c_ref, o_ref, qm_ref, m_ref, l_ref, acc_ref, *,
                 tk, nk, delta_lo, delta_hi):
    tq = q_ref.shape[1]
    i = pl.program_id(1)
    lo = _lane_lo(tq)
    for h in range(DIFF_HEADS):
        qp = q_ref[0, :, LANES * h:LANES * (h + 1)]
        qm_ref[2 * h] = jnp.where(lo, qp, jnp.zeros_like(qp))
        qm_ref[2 * h + 1] = jnp.where(lo, jnp.zeros_like(qp), qp)
    _init_streams(m_ref, l_ref, acc_ref)

    def body(j, carry):
        ks = pl.multiple_of(j * tk, tk)
        sub = strip_ref.shape[2]
        blk0 = [(jnp.clip(j * tk - i * tq - a * sub, delta_lo, delta_hi) - delta_lo) >> LANE_SHIFT
                for a in range(tq // sub)]
        for h in range(DIFF_HEADS):
            cols = slice(LANES * h, LANES * (h + 1))
            kb = k_ref[0, pl.ds(ks, tk), cols]
            vb = v_ref[0, pl.ds(ks, tk), cols]
            bias = jnp.concatenate(
                [jnp.concatenate([strip_ref[h, b0 + c] for c in range(tk // LANES)], axis=-1) for b0 in blk0], axis=0)
            for t in range(2):
                _softmax_stream_update(_dot_nt(qm_ref[2 * h + t], kb) + bias, vb, m_ref, l_ref, acc_ref, 2 * h + t)
        return carry

    lax.fori_loop(0, nk, body, 0)
    lam = c_ref[0:1, :]
    for h in range(DIFF_HEADS):
        out = acc_ref[2 * h] / l_ref[2 * h] - lam * (acc_ref[2 * h + 1] / l_ref[2 * h + 1])
        y = out * lax.rsqrt(jnp.mean(out * out, axis=-1, keepdims=True) + 1e-6)
        o_ref[0, :, LANES * h:LANES * (h + 1)] = ((y * c_ref[1:2, :]) * c_ref[2:3, :]).astype(o_ref.dtype)


def _t5_saturation_distance():
    half = T5_BUCKETS // 2
    exact = half // 2
    n = np.arange(1, 4 * T5_MAX_DISTANCE, dtype=np.float32)
    large = exact + (np.log(n / exact) / math.log(T5_MAX_DISTANCE / exact) * (half - exact)).astype(np.int32)
    return int(np.argmax(large >= half - 1)) + 1 + 8


def _diff_bias_strips(t5_cols, tq, tk):
    unit = math.gcd(tq, tk)
    sat = _t5_saturation_distance()
    delta_lo = -unit * ((sat + tk - 1) // unit + 1)
    delta_hi = unit * ((sat + tq - 1) // unit + 1)
    width = delta_hi - delta_lo + tk
    rel = jnp.arange(tq - 1 + width) - (tq - 1) + delta_lo
    u = (t5_cols.astype(F32) * LOG2E)[_t5_bucket(rel)].T
    strip = _toeplitz(u, tq, width)
    nh = strip.shape[0]
    return strip.reshape(nh, tq, width // LANES, LANES).transpose(0, 2, 1, 3), delta_lo, delta_hi


def _diff_attention(hd, strips, consts, delta_lo, delta_hi, batch, seq, tq, tk):
    w = DIFF_HEADS * LANES
    once = pl.Buffered(1)
    state = pltpu.VMEM((2 * DIFF_HEADS, tq, LANES), F32)
    return pl.pallas_call(
        functools.partial(_diff_kernel, tk=tk, nk=seq // tk, delta_lo=delta_lo, delta_hi=delta_hi),
        grid=(batch, seq // tq),
        in_specs=[pl.BlockSpec((1, tq, w), lambda b, i: (b, i, 0)),
                  pl.BlockSpec((1, seq, w), lambda b, i: (b, 0, 1), pipeline_mode=once),
                  pl.BlockSpec((1, seq, w), lambda b, i: (b, 0, 2), pipeline_mode=once),
                  pl.BlockSpec(strips.shape, lambda b, i: (0, 0, 0, 0), pipeline_mode=once),
                  pl.BlockSpec(consts.shape, lambda b, i: (0, 0))],
        out_specs=pl.BlockSpec((1, tq, w), lambda b, i: (b, i, 0)),
        out_shape=jax.ShapeDtypeStruct((batch, seq, w), BF16),
        scratch_shapes=[pltpu.VMEM((2 * DIFF_HEADS, tq, LANES), BF16), state, state, state],
        compiler_params=_cparams(("arbitrary", "arbitrary"), 56),
        name="diff_attention",
    )(hd, hd, hd, strips, consts)


def _layer_norm(z, g, b):
    mu = jnp.mean(z, axis=-1, keepdims=True)
    zc = z - mu
    var = jnp.mean(zc * zc, axis=-1, keepdims=True)
    return zc * lax.rsqrt(var + 1e-5) * g + b


def _split3(a):
    hi = a.astype(BF16)
    r1 = a - hi.astype(F32)
    mid = r1.astype(BF16)
    lo = (r1 - mid.astype(F32)).astype(BF16)
    return hi, mid, lo


def _merge_kernel(x_ref, xb_ref, ya_ref, yb_ref, o0_ref, o1_ref, o2_ref, l0_ref, l1_ref, l2_ref, yd_ref,
                  wg_ref, wna_ref, wgqa_ref, wdil_ref, wdiff_ref, wo_ref, wr_ref, ln_ref,
                  x1_ref, x1b_ref, lg_ref, *, alpha, d_model):
    xb = xb_ref[...]

    def gate(branch):
        g = jnp.dot(xb, wg_ref[:, branch * d_model:(branch + 1) * d_model], preferred_element_type=F32)
        return 1.0 / (1.0 + jnp.exp(-g))

    l0, l1, l2 = l0_ref[...], l1_ref[...], l2_ref[...]
    mx = jnp.maximum(jnp.maximum(l0, l1), l2)
    e0, e1, e2 = jnp.exp(l0 - mx), jnp.exp(l1 - mx), jnp.exp(l2 - mx)
    den = e0 + e1 + e2
    yc = (e0 / den) * o0_ref[...] + (e1 / den) * o1_ref[...] + (e2 / den) * o2_ref[...]
    merged = gate(0) * jnp.dot(ya_ref[...], wna_ref[...], preferred_element_type=F32)
    merged += gate(1) * jnp.dot(yb_ref[...], wgqa_ref[...], preferred_element_type=F32)
    merged += gate(2) * jnp.dot(yc.astype(BF16), wdil_ref[...], preferred_element_type=F32)
    merged += gate(3) * jnp.dot(yd_ref[...], wdiff_ref[...], preferred_element_type=F32)
    y = jnp.dot(merged.astype(BF16), wo_ref[...], preferred_element_type=F32)
    x1 = _layer_norm(alpha * x_ref[...] + y, ln_ref[0:1, :], ln_ref[1:2, :])
    x1_ref[...] = x1
    x1b_ref[...] = x1.astype(BF16)
    xh, xm, _ = _split3(x1)
    wh, wm, _ = wr_ref[0], wr_ref[1], wr_ref[2]
    lg_ref[...] = _dot_nt(wh, xh) + (_dot_nt(wh, xm) + _dot_nt(wm, xh))


def _merge(x, xb, ya, yb, dil, yd, w_g, w_na, w_gqa, w_dil, w_diff, w_o, w_r3, ln, alpha, tm):
    n, d = x.shape
    e = w_r3.shape[1]
    (o0, l0), (o1, l1), (o2, l2) = dil
    row = lambda w: pl.BlockSpec((tm, w), lambda i: (i, 0))
    full = lambda a: pl.BlockSpec(a.shape, lambda i: (0,) * a.ndim)
    return pl.pallas_call(
        functools.partial(_merge_kernel, alpha=alpha, d_model=d),
        grid=(n // tm,),
        in_specs=[row(d), row(d), row(ya.shape[1]), row(yb.shape[1]), row(LANES), row(LANES), row(LANES),
                  row(LANES), row(LANES), row(LANES), row(yd.shape[1]),
                  full(w_g), full(w_na), full(w_gqa), full(w_dil), full(w_diff), full(w_o), full(w_r3), full(ln)],
        out_specs=[row(d), row(d), pl.BlockSpec((e, tm), lambda i: (0, i))],
        out_shape=[jax.ShapeDtypeStruct((n, d), F32), jax.ShapeDtypeStruct((n, d), BF16),
                   jax.ShapeDtypeStruct((e, n), F32)],
        compiler_params=_cparams(("arbitrary",), 48),
        name="branch_merge",
    )(x, xb, ya, yb, o0, o1, o2, l0, l1, l2, yd, w_g, w_na, w_gqa, w_dil, w_diff, w_o, w_r3, ln)


def _route_kernel(lg_ref, aff_ref, posm_ref, posx_ref, *, cap):
    lg = lg_ref[0]
    n_e, seq = lg.shape
    mx = jnp.max(lg, axis=0, keepdims=True)
    ex = jnp.exp(lg - mx)
    aff = ex / jnp.sum(ex, axis=0, keepdims=True)
    aff_ref[0] = aff
    bits = pltpu.bitcast(aff, jnp.int32)

    def bisect(_, carry):
        lo, hi = carry
        mid = lo + ((hi - lo + 1) >> 1)
        cnt = jnp.sum((bits >= mid).astype(F32), axis=1, keepdims=True)
        ok = cnt >= float(cap)
        return jnp.where(ok, mid, lo), jnp.where(ok, hi, mid - 1)

    lo0 = jnp.zeros((n_e, 1), jnp.int32)
    hi0 = jnp.full((n_e, 1), 0x7F800000, jnp.int32)
    thr, _ = lax.fori_loop(0, 31, bisect, (lo0, hi0))
    gt = bits > thr
    eq = bits == thr
    needf = float(cap) - jnp.sum(gt.astype(F32), axis=1, keepdims=True)

    tri = (lax.broadcasted_iota(jnp.int32, (LANES, LANES), 0)
           < lax.broadcasted_iota(jnp.int32, (LANES, LANES), 1)).astype(BF16)

    def prefix(mask_bf16, j, carry):
        blk = mask_bf16[:, j * LANES:(j + 1) * LANES]
        excl = jnp.dot(blk, tri, preferred_element_type=F32) + carry
        return excl, carry + jnp.sum(blk.astype(F32), axis=1, keepdims=True)

    eqb = eq.astype(BF16)
    carry_eq = jnp.zeros((n_e, 1), F32)
    carry_sel = jnp.zeros((n_e, 1), F32)
    for j in range(seq // LANES):
        cols = slice(j * LANES, (j + 1) * LANES)
        rank, carry_eq = prefix(eqb, j, carry_eq)
        sel = jnp.logical_or(gt[:, cols], jnp.logical_and(eq[:, cols], rank < needf))
        selb = sel.astype(BF16)
        excl = jnp.dot(selb, tri, preferred_element_type=F32) + carry_sel
        carry_sel = carry_sel + jnp.sum(selb.astype(F32), axis=1, keepdims=True)
        pos = excl.astype(jnp.int32)
        posx_ref[0, :, cols] = pos
        posm_ref[0, :, cols] = jnp.where(sel, pos, -1)


def _route(logits_t, batch, seq, cap):
    n_e = logits_t.shape[0]
    lg = logits_t.reshape(n_e, batch, seq).transpose(1, 0, 2)
    spec = pl.BlockSpec((1, n_e, seq), lambda b: (b, 0, 0))
    return pl.pallas_call(
        functools.partial(_route_kernel, cap=cap),
        grid=(batch,),
        in_specs=[spec],
        out_specs=[spec, spec, spec],
        out_shape=[jax.ShapeDtypeStruct((batch, n_e, seq), F32), jax.ShapeDtypeStruct((batch, n_e, seq), jnp.int32),
                   jax.ShapeDtypeStruct((batch, n_e, seq), jnp.int32)],
        compiler_params=_cparams(("arbitrary",), 32),
        name="ec_route",
    )(lg)


def _gather_kernel(clo_ref, chi_ref, x_ref, pos_ref, aff_ref, xin_ref, gs_ref, acc_ref, *, n_e, nsb):
    base = (pl.program_id(0) * n_e + pl.program_id(1)) * nsb
    for sb in range(nsb):
        slot_ids = sb * SLOT_BLOCK + lax.broadcasted_iota(jnp.int32, (SLOT_BLOCK, TOKEN_CHUNK), 0)
        acc_ref[...] = jnp.zeros_like(acc_ref)

        def body(c, g, slot_ids=slot_ids):
            hit = pos_ref[0, 0, pl.ds(c, 1), :] == slot_ids
            xc = x_ref[0, pl.ds(pl.multiple_of(c * TOKEN_CHUNK, TOKEN_CHUNK), TOKEN_CHUNK), :]
            acc_ref[...] += jnp.dot(hit.astype(BF16), xc, preferred_element_type=F32)
            return g + jnp.sum(jnp.where(hit, aff_ref[0, 0, pl.ds(c, 1), :], 0.0), axis=-1, keepdims=True)

        g = lax.fori_loop(clo_ref[base + sb], chi_ref[base + sb], body, jnp.zeros((SLOT_BLOCK, 1), F32))
        rows = slice(sb * SLOT_BLOCK, (sb + 1) * SLOT_BLOCK)
        xin_ref[0, 0, rows, :] = acc_ref[...].astype(BF16)
        gs_ref[0, 0, rows, :] = jnp.broadcast_to(g, (SLOT_BLOCK, LANES))


def _gather(x1b, posm, aff, clo, chi, cap):
    batch, seq, d = x1b.shape
    n_e = posm.shape[1]
    nsb = cap // SLOT_BLOCK
    nch = seq // TOKEN_CHUNK
    pos4 = posm.reshape(batch, n_e, nch, TOKEN_CHUNK)
    aff4 = aff.reshape(batch, n_e, nch, TOKEN_CHUNK)
    grid_spec = pltpu.PrefetchScalarGridSpec(
        num_scalar_prefetch=2,
        grid=(batch, n_e),
        in_specs=[pl.BlockSpec((1, seq, d), lambda b, e, *_: (b, 0, 0)),
                  pl.BlockSpec((1, 1, nch, TOKEN_CHUNK), lambda b, e, *_: (b, e, 0, 0)),
                  pl.BlockSpec((1, 1, nch, TOKEN_CHUNK), lambda b, e, *_: (b, e, 0, 0))],
        out_specs=[pl.BlockSpec((1, 1, cap, d), lambda b, e, *_: (b, e, 0, 0)),
                   pl.BlockSpec((1, 1, cap, LANES), lambda b, e, *_: (b, e, 0, 0))],
        scratch_shapes=[pltpu.VMEM((SLOT_BLOCK, d), F32)],
    )
    return pl.pallas_call(
        functools.partial(_gather_kernel, n_e=n_e, nsb=nsb),
        grid_spec=grid_spec,
        out_shape=[jax.ShapeDtypeStruct((batch, n_e, cap, d), BF16),
                   jax.ShapeDtypeStruct((batch, n_e, cap, LANES), F32)],
        compiler_params=_cparams(("arbitrary", "arbitrary"), 56),
        name="ec_gather",
    )(clo, chi, x1b, pos4, aff4)


def _ffn_kernel(x_ref, gs_ref, wg_ref, wu_ref, wd_ref, o_ref, *, fchunk):
    x = x_ref[0, 0]
    ff = wg_ref.shape[2]
    acc = jnp.zeros((x.shape[0], wd_ref.shape[2]), F32)
    for f0 in range(0, ff, fchunk):
        g = jnp.dot(x, wg_ref[0, :, f0:f0 + fchunk], preferred_element_type=F32)
        u = jnp.dot(x, wu_ref[0, :, f0:f0 + fchunk], preferred_element_type=F32)
        hid = (g / (1.0 + jnp.exp(-g))) * u
        acc += jnp.dot(hid.astype(BF16), wd_ref[0, f0:f0 + fchunk, :], preferred_element_type=F32)
    o_ref[0, 0] = (acc * gs_ref[0, 0][:, 0:1]).astype(o_ref.dtype)


def _expert_ffn(xin, gslot, wg, wu, wd, tc, fchunk):
    batch, n_e, cap, d = xin.shape
    ff = wg.shape[2]
    return pl.pallas_call(
        functools.partial(_ffn_kernel, fchunk=fchunk),
        grid=(n_e, batch, cap // tc),
        in_specs=[pl.BlockSpec((1, 1, tc, d), lambda e, b, i: (b, e, i, 0)),
                  pl.BlockSpec((1, 1, tc, LANES), lambda e, b, i: (b, e, i, 0)),
                  pl.BlockSpec((1, d, ff), lambda e, b, i: (e, 0, 0)),
                  pl.BlockSpec((1, d, ff), lambda e, b, i: (e, 0, 0)),
                  pl.BlockSpec((1, ff, d), lambda e, b, i: (e, 0, 0))],
        out_specs=pl.BlockSpec((1, 1, tc, d), lambda e, b, i: (b, e, i, 0)),
        out_shape=jax.ShapeDtypeStruct((batch, n_e, cap, d), BF16),
        compiler_params=_cparams(("arbitrary", "arbitrary", "arbitrary"), 56),
        name="expert_ffn",
    )(xin, gslot, wg, wu, wd)


def _combine_kernel(cs_ref, yo_ref, post_ref, y_ref, *, n_e, nch, cap):
    b = pl.program_id(0)
    c = pl.program_id(2)
    win = 2 * SLOT_BLOCK
    pos_all = post_ref[0]
    win_ids = lax.broadcasted_iota(jnp.int32, (TOKEN_CHUNK, win), 1)
    hits, wins, tails = [], [], []
    for e in range(n_e):
        base = (b * n_e + e) * (nch + 1) + c
        s_lo = cs_ref[base]
        s_hi = cs_ref[base + 1]
        w0 = pl.multiple_of(jnp.minimum((s_lo >> SLOT_SHIFT) << SLOT_SHIFT, cap - win), SLOT_BLOCK)
        hits.append((pos_all[:, e:e + 1] == (w0 + win_ids)).astype(BF16))
        wins.append(yo_ref[0, e, pl.ds(w0, win), :])
        sb_end = jnp.where(s_hi > s_lo, ((s_hi - 1) >> SLOT_SHIFT) + 1, 0)
        tails.append(((w0 + win) >> SLOT_SHIFT, sb_end))
    y_ref[0] = jnp.dot(jnp.concatenate(hits, axis=1), jnp.concatenate(wins, axis=0), preferred_element_type=F32)
    for e, (sb_from, sb_end) in enumerate(tails):
        def body(sb, carry, e=e):
            blk_ids = lax.broadcasted_iota(jnp.int32, (TOKEN_CHUNK, SLOT_BLOCK), 1)
            hit = post_ref[0, :, e:e + 1] == (sb * SLOT_BLOCK + blk_ids)
            yb = yo_ref[0, e, pl.ds(pl.multiple_of(sb * SLOT_BLOCK, SLOT_BLOCK), SLOT_BLOCK), :]
            y_ref[0] += jnp.dot(hit.astype(BF16), yb, preferred_element_type=F32)
            return carry

        lax.fori_loop(sb_from, sb_end, body, 0)


def _combine(yo, pos_t, cs, td):
    batch, n_e, cap, d = yo.shape
    seq = pos_t.shape[1]
    nch = seq // TOKEN_CHUNK
    grid_spec = pltpu.PrefetchScalarGridSpec(
        num_scalar_prefetch=1,
        grid=(batch, d // td, nch),
        in_specs=[pl.BlockSpec((1, n_e, cap, td), lambda b, j, c, *_: (b, 0, 0, j)),
                  pl.BlockSpec((1, TOKEN_CHUNK, n_e), lambda b, j, c, *_: (b, c, 0))],
        out_specs=pl.BlockSpec((1, TOKEN_CHUNK, td), lambda b, j, c, *_: (b, c, j)),
    )
    return pl.pallas_call(
        functools.partial(_combine_kernel, n_e=n_e, nch=nch, cap=cap),
        grid_spec=grid_spec,
        out_shape=jax.ShapeDtypeStruct((batch, seq, d), F32),
        compiler_params=_cparams(("arbitrary", "arbitrary", "arbitrary"), 56),
        name="ec_combine",
    )(cs, yo, pos_t)


def _ple_kernel(x1_ref, y_ref, p_ref, wpg_ref, wpp_ref, ln_ref, x3_ref, x3b_ref, *, alpha):
    x2 = _layer_norm(alpha * x1_ref[...] + y_ref[...], ln_ref[0:1, :], ln_ref[1:2, :])
    gate = 1.0 / (1.0 + jnp.exp(-jnp.dot(x2.astype(BF16), wpg_ref[...], preferred_element_type=F32)))
    emb = jnp.dot(p_ref[...].astype(BF16), wpp_ref[...], preferred_element_type=F32) * gate
    x3 = _layer_norm(alpha * x2 + emb, ln_ref[2:3, :], ln_ref[3:4, :])
    x3_ref[...] = x3
    x3b_ref[...] = x3.astype(BF16)


def _ple(x1, y, p, w_pg, w_pp, ln, alpha, tm):
    n, d = x1.shape
    row = lambda w: pl.BlockSpec((tm, w), lambda i: (i, 0))
    full = lambda a: pl.BlockSpec(a.shape, lambda i: (0,) * a.ndim)
    return pl.pallas_call(
        functools.partial(_ple_kernel, alpha=alpha),
        grid=(n // tm,),
        in_specs=[row(d), row(d), row(p.shape[1]), full(w_pg), full(w_pp), full(ln)],
        out_specs=[row(d), row(d)],
        out_shape=[jax.ShapeDtypeStruct((n, d), F32), jax.ShapeDtypeStruct((n, d), BF16)],
        compiler_params=_cparams(("arbitrary",), 40),
        name="ple_norm",
    )(x1, y, p, w_pg, w_pp, ln)


def _token_mixer_inputs(xb, w_in, qk_gain, batch, seq):
    d = w_in.shape[0]
    scale = HEAD_DIM ** -0.5
    na_w = NA_HEADS * HEAD_DIM
    gq_w, gkv_w = GQA_HEADS * HEAD_DIM, GQA_KV_HEADS * HEAD_DIM
    dil_w = DIL_HEADS * HEAD_DIM
    dqk_w = DIFF_HEADS * 2 * HEAD_DIM
    o = 0
    w_a = w_in[:, o:o + 3 * na_w]; o += 3 * na_w
    w_b = w_in[:, o:o + gq_w + 2 * gkv_w]; o += gq_w + 2 * gkv_w
    w_c = w_in[:, o:o + 3 * dil_w]; o += 3 * dil_w
    w_d = w_in[:, o:o + 3 * dqk_w]; o += 3 * dqk_w
    w_g = w_in[:, o:]

    def qscaled(w, qw):
        return jnp.concatenate([w[:, :qw] * scale, w[:, qw:]], axis=1).astype(BF16)

    n = batch * seq
    ha = _matmul(xb, qscaled(w_a, na_w), BF16, 1024, 3 * na_w, "proj_na").reshape(batch, seq, -1)
    hc = _matmul(xb, qscaled(w_c, dil_w), BF16, 1024, 3 * dil_w, "proj_dil").reshape(batch, seq, -1)
    w_dq = jnp.concatenate([w_d[:, :dqk_w] * (scale * LOG2E), w_d[:, dqk_w:]], axis=1).astype(BF16)
    hd = _matmul(xb, w_dq, BF16, 1024, 3 * dqk_w, "proj_diff").reshape(batch, seq, -1)
    qn, kd, vd = _proj_gqa(xb, w_b, qk_gain, seq, 512)
    shape3 = lambda t: t.reshape(batch, seq, t.shape[-1])
    return ha, shape3(qn), shape3(kd), shape3(vd), hc, hd, w_g.astype(BF16)


def _layer(x, xb, p, w, lam_init, alpha):
    batch, seq, d = x.shape
    n = batch * seq
    ha, qn, kd, vd, hc, hd, w_g = _token_mixer_inputs(xb.reshape(n, d), w["w_in"], w["qk_gain"], batch, seq)

    ya = _na_attention(ha, _na_bias_tiles(w["rpb"]), batch, seq)
    yb = _gqa_attention(qn, kd, vd, ATTN_TQ, ATTN_TK)
    dil = []
    for g, (_, dilation) in enumerate(DIL_GROUPS):
        bias = _dil_bias_tiles(w["t5"][:, 2 * g:2 * g + 2], dilation, 128)
        dil.append(_dilated_group(hc, bias, g, dilation, batch, seq, 128, DIL_BLOCKS_PER_STEP))
    lp = w["lam_params"].astype(F32)
    lam = jnp.exp(jnp.sum(lp[0] * lp[1])) - jnp.exp(jnp.sum(lp[2] * lp[3])) + lam_init
    consts = jnp.zeros((8, LANES), F32).at[0].set(lam).at[1].set(w["diff_gain"]).at[2].set(1.0 - lam_init)
    strips, delta_lo, delta_hi = _diff_bias_strips(w["t5"][:, DIL_HEADS:], DIFF_BIAS_ROWS, ATTN_TK)
    yd = _diff_attention(hd, strips, consts, delta_lo, delta_hi, batch, seq, ATTN_TQ, ATTN_TK)

    flat = lambda t: t.reshape(n, t.shape[-1])
    w_r3 = jnp.stack(_split3(w["w_router"].T))
    ln = w["ln"]
    x1, x1b, logits_t = _merge(
        x.reshape(n, d), xb.reshape(n, d), flat(ya), flat(yb), [(flat(o), flat(l)) for o, l in dil], flat(yd), w_g,
        w["w_na"].astype(BF16), w["w_gqa"].astype(BF16), w["w_dil"].astype(BF16), w["w_diff"].astype(BF16),
        w["w_o"].astype(BF16), w_r3, jnp.stack([ln[0][0], ln[1][0]]), alpha, 256)

    cap = EC_CAPACITY * seq // N_EXPERTS
    aff, posm, posx = _route(logits_t, batch, seq, cap)
    nch = seq // TOKEN_CHUNK
    cs = jnp.concatenate([posx[:, :, ::TOKEN_CHUNK], jnp.full((batch, N_EXPERTS, 1), cap, jnp.int32)], axis=-1)
    edges = jnp.arange(cap // SLOT_BLOCK, dtype=jnp.int32) * SLOT_BLOCK
    clo = jnp.sum(cs[:, :, 1:, None] <= edges, axis=2).astype(jnp.int32)
    chi = jnp.sum(cs[:, :, :nch, None] < edges + SLOT_BLOCK, axis=2).astype(jnp.int32)
    xin, gslot = _gather(x1b.reshape(batch, seq, d), posm, aff, clo.reshape(-1), chi.reshape(-1), cap)
    yo = _expert_ffn(xin, gslot, w["w_eg"].astype(BF16), w["w_eu"].astype(BF16), w["w_ed"].astype(BF16), 512, 512)
    y = _combine(yo, posm.transpose(0, 2, 1), cs.reshape(-1), 512)

    ln4 = jnp.stack([ln[0][1], ln[1][1], ln[0][2], ln[1][2]])
    x3, x3b = _ple(x1, y.reshape(n, d), p.reshape(n, -1), w["w_pg"].astype(BF16), w["w_pp"].astype(BF16), ln4, alpha, 512)
    return x3.reshape(batch, seq, d), x3b.reshape(batch, seq, d)


def kernel(x, p, w_in, w_branch_na, w_branch_gqa, w_branch_dil, w_branch_diff, w_out, na_rel_bias, qk_norm_gain,
           diff_lambda, diff_norm_gain, t5_rel_bias, w_router, w_expert_gate, w_expert_up, w_expert_down,
           w_ple_proj, w_ple_gate, ln_gain, ln_bias):
    depth = w_in.shape[0]
    alpha = (2 * depth) ** 0.25
    xb = x.astype(BF16)
    for i in range(depth):
        lam_init = 0.8 - 0.6 * math.exp(-0.3 * i)
        w = dict(w_in=w_in[i], w_na=w_branch_na[i], w_gqa=w_branch_gqa[i], w_dil=w_branch_dil[i],
                 w_diff=w_branch_diff[i], w_o=w_out[i], rpb=na_rel_bias[i], qk_gain=qk_norm_gain[i],
                 lam_params=diff_lambda[i], diff_gain=diff_norm_gain[i], t5=t5_rel_bias, w_router=w_router[i],
                 w_eg=w_expert_gate[i], w_eu=w_expert_up[i], w_ed=w_expert_down[i], w_pp=w_ple_proj[i],
                 w_pg=w_ple_gate[i], ln=(ln_gain[i], ln_bias[i]))
        x, xb = _layer(x, xb, p[i], w, lam_init, alpha)
    return x
```

```python
import functools
import math

import numpy as np
import jax
import jax.numpy as jnp
from jax import lax
from jax.experimental import pallas as pl
from jax.experimental.pallas import tpu as pltpu

F32 = jnp.float32
BF16 = jnp.bfloat16
NEG_INF = -1e30

V7X_VMEM_BYTES = 64 * 1024 * 1024
LANES = 128
LANE_SHIFT = 7

GRID_W = 64
HEAD_DIM = 64
NA_HEADS, NA_ROWS, NA_COLS = 6, 8, 16
NA_ROWS_PER_STEP = 4
LOG2E = 1.4426950408889634
ATTN_TQ, ATTN_TK = 1024, 256
GQA_LOGIT_LIMIT = 96.0
DIL_BLOCKS_PER_STEP = 4
DIFF_BIAS_ROWS = 256
GQA_HEADS, GQA_KV_HEADS = 6, 2
ROPE_THETA = 10000.0
DIL_GROUPS = ((128, 1), (512, 4), (2048, 16))
DIL_HEADS = 6
DIFF_HEADS = 4
T5_BUCKETS, T5_MAX_DISTANCE = 32, 1024
N_EXPERTS, EC_CAPACITY = 16, 2
SLOT_BLOCK = 128
SLOT_SHIFT = SLOT_BLOCK.bit_length() - 1
TOKEN_CHUNK = 256


def _cparams(semantics, vmem_mib):
    return pltpu.CompilerParams(dimension_semantics=semantics,
                                vmem_limit_bytes=min(vmem_mib * 1024 * 1024, V7X_VMEM_BYTES - 4 * 1024 * 1024))


def _mm_kernel(a_ref, b_ref, o_ref):
    o_ref[...] = jnp.dot(a_ref[...], b_ref[...], preferred_element_type=F32).astype(o_ref.dtype)


def _matmul(a, b, out_dtype, tm, tn, name):
    m, k = a.shape
    n = b.shape[1]
    return pl.pallas_call(
        _mm_kernel,
        grid=(n // tn, m // tm),
        in_specs=[pl.BlockSpec((tm, k), lambda j, i: (i, 0)),
                  pl.BlockSpec((k, tn), lambda j, i: (0, j))],
        out_specs=pl.BlockSpec((tm, tn), lambda j, i: (i, j)),
        out_shape=jax.ShapeDtypeStruct((m, n), out_dtype),
        compiler_params=_cparams(("arbitrary", "arbitrary"), 40),
        name=name,
    )(a, b)


def _toeplitz(u, rows, width):
    lead = u.shape[:-1]
    period = rows + width
    text = jnp.concatenate([u[..., rows - 1:], jnp.zeros(lead + (1,), u.dtype), u[..., :rows - 1]], axis=-1)
    flat = jnp.tile(text, (1,) * len(lead) + (rows,))[..., :rows * (period - 1)]
    return flat.reshape(lead + (rows, period - 1))[..., :width]


def _lane_lo(rows):
    return lax.broadcasted_iota(jnp.int32, (rows, LANES), 1) < HEAD_DIM


def _dot_nt(a, b):
    return lax.dot_general(a, b, (((1,), (1,)), ((), ())), preferred_element_type=F32)


NA_WIN_ROWS = NA_ROWS + NA_ROWS_PER_STEP


def _na_kernel(q_ref, k_ref, v_ref, tp_ref, o_ref, *, rows):
    r0 = pl.program_id(1) * NA_ROWS_PER_STEP
    ws = jnp.clip(r0 - NA_ROWS // 2, 0, rows - NA_WIN_ROWS)
    start = pl.multiple_of(ws * GRID_W, GRID_W)
    win = NA_WIN_ROWS * GRID_W
    npair = NA_WIN_ROWS // 2
    none_code = 2 * (2 * NA_ROWS - 1)
    codes = []
    for rr in range(NA_ROWS_PER_STEP):
        r = r0 + rr
        rs = jnp.clip(r - NA_ROWS // 2, 0, rows - NA_ROWS)
        row_codes = []
        for j in range(npair):
            a0 = ws + 2 * j
            d0 = a0 - r + (NA_ROWS - 1)
            vl = jnp.logical_and(a0 >= rs, a0 < rs + NA_ROWS)
            vr = jnp.logical_and(a0 + 1 >= rs, a0 + 1 < rs + NA_ROWS)
            row_codes.append(jnp.where(jnp.logical_and(vl, vr), d0,
                                       jnp.where(jnp.logical_or(vl, vr), 2 * NA_ROWS - 1 + d0, none_code)))
        codes.append(row_codes)
    tq = NA_ROWS_PER_STEP * GRID_W
    lo = _lane_lo(tq)
    for p in range(NA_HEADS // 2):
        cols = slice(LANES * p, LANES * (p + 1))
        qp = q_ref[0, :, cols]
        kw = k_ref[0, pl.ds(start, win), cols]
        vw = v_ref[0, pl.ds(start, win), cols]
        outs = []
        for hh in range(2):
            h = 2 * p + hh
            qm = jnp.where(lo if hh == 0 else jnp.logical_not(lo), qp, jnp.zeros_like(qp))
            bias = jnp.concatenate(
                [jnp.concatenate([tp_ref[h, c] for c in row_codes], axis=-1) for row_codes in codes], axis=0)
            s = _dot_nt(qm, kw) + bias
            m = jnp.max(s, axis=-1, keepdims=True)
            e = jnp.exp(s - m)
            pr = (e / jnp.sum(e, axis=-1, keepdims=True)).astype(BF16)
            outs.append(jnp.dot(pr, vw, preferred_element_type=F32))
        o_ref[0, :, cols] = jnp.where(lo, outs[0], outs[1]).astype(o_ref.dtype)


def _na_bias_tiles(rpb):
    col = jnp.arange(GRID_W)
    cstart = jnp.clip(col - NA_COLS // 2, 0, GRID_W - NA_COLS)
    kc = jnp.arange(GRID_W)
    inwin = (kc[None, :] >= cstart[:, None]) & (kc[None, :] < cstart[:, None] + NA_COLS)
    pad = GRID_W - NA_COLS
    u = jnp.pad(rpb.astype(F32), ((0, 0), (0, 0), (pad, pad)))
    t = jnp.where(inwin[None, None], _toeplitz(u, GRID_W, GRID_W), NEG_INF)
    neg = jnp.full_like(t[:, :NA_ROWS], NEG_INF)
    both = jnp.concatenate([t[:, :-1], t[:, 1:]], axis=-1)
    right = jnp.concatenate([neg, t[:, :NA_ROWS]], axis=-1)
    left = jnp.concatenate([t[:, NA_ROWS - 1:], neg], axis=-1)
    none = jnp.concatenate([neg[:, :1], neg[:, :1]], axis=-1)
    return jnp.concatenate([both, right, left, none], axis=1)


def _na_attention(ha, t2, batch, seq):
    rows = seq // GRID_W
    w = NA_HEADS * HEAD_DIM
    tq = GRID_W * NA_ROWS_PER_STEP
    return pl.pallas_call(
        functools.partial(_na_kernel, rows=rows),
        grid=(batch, rows // NA_ROWS_PER_STEP),
        in_specs=[pl.BlockSpec((1, tq, w), lambda b, r: (b, r, 0)),
                  pl.BlockSpec((1, seq, w), lambda b, r: (b, 0, 1)),
                  pl.BlockSpec((1, seq, w), lambda b, r: (b, 0, 2)),
                  pl.BlockSpec(t2.shape, lambda b, r: (0, 0, 0, 0))],
        out_specs=pl.BlockSpec((1, tq, w), lambda b, r: (b, r, 0)),
        out_shape=jax.ShapeDtypeStruct((batch, seq, w), BF16),
        compiler_params=_cparams(("arbitrary", "arbitrary"), 56),
        name="na_attention",
    )(ha, ha, ha, t2)


def _softmax_stream_update(s, vb, m_ref, l_ref, acc_ref, idx):
    m_prev = m_ref[idx]
    m_new = jnp.maximum(m_prev, jnp.max(s, axis=-1, keepdims=True))
    alpha = jnp.exp2(m_prev - m_new)
    p = jnp.exp2(s - jnp.tile(m_new, (1, s.shape[1] // LANES)))
    m_ref[idx] = m_new
    l_ref[idx] = alpha * l_ref[idx] + jnp.sum(p, axis=-1, keepdims=True)
    acc_ref[idx] = alpha * acc_ref[idx] + jnp.dot(p.astype(BF16), vb, preferred_element_type=F32)


def _init_streams(m_ref, l_ref, acc_ref):
    m_ref[...] = jnp.full(m_ref.shape, NEG_INF, F32)
    l_ref[...] = jnp.zeros(l_ref.shape, F32)
    acc_ref[...] = jnp.zeros(acc_ref.shape, F32)


def _gqa_kernel(q_ref, k_ref, v_ref, o_ref, qm_ref, m_ref, l_ref, acc_ref, *, tk, nk, track_max):
    tq = q_ref.shape[1]
    lo = _lane_lo(tq)
    rep = GQA_HEADS // GQA_KV_HEADS
    for h in range(GQA_HEADS):
        qp = q_ref[0, :, LANES * (h // 2):LANES * (h // 2 + 1)]
        qm_ref[h] = jnp.where(lo if h % 2 == 0 else jnp.logical_not(lo), qp, jnp.zeros_like(qp))
    _init_streams(m_ref, l_ref, acc_ref)

    def body(j, carry):
        ks = pl.multiple_of(j * tk, tk)
        for g in range(GQA_KV_HEADS):
            gcols = slice(LANES * g, LANES * (g + 1))
            kb = k_ref[0, pl.ds(ks, tk), gcols]
            vb = v_ref[0, pl.ds(ks, tk), gcols]
            for h in range(rep * g, rep * (g + 1)):
                s = _dot_nt(qm_ref[h], kb)
                if track_max:
                    _softmax_stream_update(s, vb, m_ref, l_ref, acc_ref, h)
                else:
                    p = jnp.exp2(s)
                    l_ref[h] += jnp.sum(p, axis=-1, keepdims=True)
                    acc_ref[h] += jnp.dot(p.astype(BF16), vb, preferred_element_type=F32)
        return carry

    lax.fori_loop(0, nk, body, 0)
    for p in range(GQA_HEADS // 2):
        o0 = acc_ref[2 * p] / l_ref[2 * p]
        o1 = acc_ref[2 * p + 1] / l_ref[2 * p + 1]
        o_ref[0, :, LANES * p:LANES * (p + 1)] = jnp.where(lo, o0, o1).astype(o_ref.dtype)


def _gqa_attention(q, kd, vd, tq, tk, track_max):
    batch, seq, wq = q.shape
    wk = kd.shape[-1]
    state = pltpu.VMEM((GQA_HEADS, tq, LANES), F32)
    return pl.pallas_call(
        functools.partial(_gqa_kernel, tk=tk, nk=seq // tk, track_max=track_max),
        grid=(batch, seq // tq),
        in_specs=[pl.BlockSpec((1, tq, wq), lambda b, i: (b, i, 0)),
                  pl.BlockSpec((1, seq, wk), lambda b, i: (b, 0, 0)),
                  pl.BlockSpec((1, seq, wk), lambda b, i: (b, 0, 0))],
        out_specs=pl.BlockSpec((1, tq, wq), lambda b, i: (b, i, 0)),
        out_shape=jax.ShapeDtypeStruct((batch, seq, wq), BF16),
        scratch_shapes=[pltpu.VMEM((GQA_HEADS, tq, LANES), BF16), state, state, state],
        compiler_params=_cparams(("arbitrary", "arbitrary"), 48),
        name="gqa_attention" if track_max else "gqa_attention_bounded",
    )(q, kd, vd)


def _gqa_logit_bound(qk_gain):
    gq = jnp.max(jnp.abs(qk_gain[0])) * (HEAD_DIM ** -0.5 * LOG2E)
    return 1.05 * HEAD_DIM * gq * jnp.max(jnp.abs(qk_gain[1]))


def _rope_tables(seq):
    quarter = HEAD_DIM // 4
    freqs = ROPE_THETA ** (-jnp.arange(quarter, dtype=F32) / quarter)
    t = jnp.arange(seq)
    row = (t // GRID_W).astype(F32)
    col = (t % GRID_W).astype(F32)
    ang_r = row[:, None] * freqs[None, :]
    ang_c = col[:, None] * freqs[None, :]
    cos = jnp.concatenate([jnp.cos(ang_r)] * 2 + [jnp.cos(ang_c)] * 2, axis=-1)
    sin = jnp.concatenate([-jnp.sin(ang_r), jnp.sin(ang_r), -jnp.sin(ang_c), jnp.sin(ang_c)], axis=-1)
    return cos, sin


def _proj_gqa_kernel(x_ref, w_ref, cos_ref, sin_ref, gain_ref, mean_ref, q_ref, k_ref, v_ref):
    h = jnp.dot(x_ref[...], w_ref[...], preferred_element_type=F32)
    nq = q_ref.shape[1] // LANES
    nk = k_ref.shape[1] // LANES
    lane = lax.broadcasted_iota(jnp.int32, (h.shape[0], LANES), 1)
    first = (lane & (HEAD_DIM // 2 - 1)) < HEAD_DIM // 4
    for blk in range(nq + nk):
        t = h[:, LANES * blk:LANES * (blk + 1)]
        hi, mid, low = _split3(t * t)
        ms = (jnp.dot(hi, mean_ref[...], preferred_element_type=F32)
              + jnp.dot(mid, mean_ref[...], preferred_element_type=F32)
              + jnp.dot(low, mean_ref[...], preferred_element_type=F32))
        y = t * lax.rsqrt(ms + 1e-6) * gain_ref[blk:blk + 1, :]
        partner = jnp.where(first, pltpu.roll(y, LANES - HEAD_DIM // 4, 1), pltpu.roll(y, HEAD_DIM // 4, 1))
        out = (y * cos_ref[...] + partner * sin_ref[...]).astype(BF16)
        if blk < nq:
            q_ref[:, LANES * blk:LANES * (blk + 1)] = out
        else:
            k_ref[:, LANES * (blk - nq):LANES * (blk - nq + 1)] = out
    v_ref[...] = h[:, LANES * (nq + nk):].astype(BF16)


def _proj_gqa(xb, w_b, qk_gain, seq, tm):
    n, d = xb.shape
    gq, gkv = GQA_HEADS * HEAD_DIM, GQA_KV_HEADS * HEAD_DIM
    dup = lambda w: jnp.concatenate([w[:, i * HEAD_DIM:(i + 1) * HEAD_DIM] for i in range(GQA_KV_HEADS) for _ in range(2)], axis=1)
    w_all = jnp.concatenate([w_b[:, :gq], dup(w_b[:, gq:gq + gkv]), dup(w_b[:, gq + gkv:])], axis=1).astype(BF16)
    cos, sin = _rope_tables(seq)
    cos2, sin2 = jnp.tile(cos, (1, 2)), jnp.tile(sin, (1, 2))
    nq, nk = gq // LANES, 2 * gkv // LANES
    gain = jnp.concatenate([jnp.tile(qk_gain[0] * (HEAD_DIM ** -0.5 * LOG2E), (nq, 2)), jnp.tile(qk_gain[1], (nk, 2)),
                            jnp.zeros((8 - nq - nk, LANES), F32)], axis=0)
    head = jnp.arange(LANES) // HEAD_DIM
    mean_mat = jnp.where(head[:, None] == head[None, :], 1.0 / HEAD_DIM, 0.0).astype(BF16)
    nt = seq // tm
    row = lambda w: pl.BlockSpec((tm, w), lambda i: (i, 0))
    full = lambda a: pl.BlockSpec(a.shape, lambda i: (0,) * a.ndim)
    pos = pl.BlockSpec((tm, LANES), lambda i: (i % nt, 0))
    return pl.pallas_call(
        _proj_gqa_kernel,
        grid=(n // tm,),
        in_specs=[row(d), full(w_all), pos, pos, full(gain), full(mean_mat)],
        out_specs=[row(gq), row(2 * gkv), row(2 * gkv)],
        out_shape=[jax.ShapeDtypeStruct((n, gq), BF16), jax.ShapeDtypeStruct((n, 2 * gkv), BF16),
                   jax.ShapeDtypeStruct((n, 2 * gkv), BF16)],
        compiler_params=_cparams(("arbitrary",), 40),
        name="proj_gqa",
    )(xb, w_all, cos2, sin2, gain, mean_mat)


def _dil_kernel(q_ref, k_ref, v_ref, bias_ref, o_ref, lse_ref, *, sub_len, nblk, tq):
    half = DIL_GROUPS[0][0] // 2
    win = tq + 2 * half
    lo = _lane_lo(tq)
    for qq in range(q_ref.shape[1] // tq):
        i = pl.program_id(2) * (q_ref.shape[1] // tq) + qq
        start = pl.multiple_of(jnp.clip(i * tq - half, 0, sub_len - win), half)
        var = jnp.where(i == 0, 0, jnp.where(i == nblk - 1, 2, 1))
        rows = slice(qq * tq, (qq + 1) * tq)
        qp = q_ref[0, rows, :]
        kw = k_ref[0, pl.ds(start, win), :]
        vw = v_ref[0, pl.ds(start, win), :]
        outs, lses = [], []
        for hh in range(2):
            qm = jnp.where(lo if hh == 0 else jnp.logical_not(lo), qp, jnp.zeros_like(qp))
            s = _dot_nt(qm, kw) + bias_ref[hh, var]
            m = jnp.max(s, axis=-1, keepdims=True)
            e = jnp.exp(s - m)
            l = jnp.sum(e, axis=-1, keepdims=True)
            outs.append(jnp.dot((e / l).astype(BF16), vw, preferred_element_type=F32))
            lses.append(jnp.broadcast_to(m + jnp.log(l), (tq, LANES)))
        o_ref[0, rows, :] = jnp.where(lo, outs[0], outs[1])
        lse_ref[0, rows, :] = jnp.where(lo, lses[0], lses[1])


def _t5_bucket(rel):
    half = T5_BUCKETS // 2
    exact = half // 2
    n = jnp.abs(rel)
    nf = jnp.maximum(n, 1).astype(F32)
    large = exact + (jnp.log(nf / exact) / math.log(T5_MAX_DISTANCE / exact) * (half - exact)).astype(jnp.int32)
    large = jnp.minimum(large, half - 1)
    return jnp.where(rel > 0, half, 0) + jnp.where(n < exact, n, large)


def _dil_bias_tiles(t5_cols, dilation, tq):
    half = DIL_GROUPS[0][0] // 2
    win = tq + 2 * half
    wide = win + 2 * half
    j = jnp.arange(tq - 1 + wide) - (tq - 1) - 2 * half
    vals = t5_cols.astype(F32)[_t5_bucket(j * dilation)]
    u = jnp.where((jnp.abs(j) <= half)[:, None], vals, NEG_INF).T
    a = _toeplitz(u, tq, wide)
    return jnp.stack([a[:, :, 2 * half - s:2 * half - s + win] for s in (0, half, 2 * half)], axis=1)


def _dilated_group(hc, bias, g, dilation, batch, seq, tq, qb):
    sub_len = seq // dilation
    nblk = sub_len // tq
    ts = tq * qb
    if dilation == 1:
        view, nb, g = hc, hc.shape[-1] // LANES, 3 * g
    else:
        view, nb, g = hc[..., 3 * LANES * g:3 * LANES * (g + 1)].reshape(batch, sub_len, dilation * 3 * LANES), 3, 0
    nq = 1
    out, lse = pl.pallas_call(
        functools.partial(_dil_kernel, sub_len=sub_len, nblk=nblk, tq=tq),
        grid=(batch, dilation, nblk // qb),
        in_specs=[pl.BlockSpec((1, ts, LANES), lambda b, r, i: (b, i, r * nb + g)),
                  pl.BlockSpec((1, sub_len, LANES), lambda b, r, i: (b, 0, r * nb + nq + g)),
                  pl.BlockSpec((1, sub_len, LANES), lambda b, r, i: (b, 0, r * nb + 2 * nq + g)),
                  pl.BlockSpec(bias.shape, lambda b, r, i: (0, 0, 0, 0))],
        out_specs=[pl.BlockSpec((1, ts, LANES), lambda b, r, i: (b, i, r)),
                   pl.BlockSpec((1, ts, LANES), lambda b, r, i: (b, i, r))],
        out_shape=[jax.ShapeDtypeStruct((batch, sub_len, dilation * LANES), F32)] * 2,
        compiler_params=_cparams(("arbitrary", "arbitrary", "arbitrary"), 32),
        name=f"dilated_attention_d{dilation}",
    )(view, view, view, bias)
    return out.reshape(batch, seq, LANES), lse.reshape(batch, seq, LANES)


def _diff_kernel(q_ref, k_ref, v_ref, strip_ref, c_ref, o_ref, qm_ref, m_ref, l_ref, acc_ref, *,
                 tk, nk, delta_lo, delta_hi):
    tq = q_ref.shape[1]
    i = pl.program_id(1)
    lo = _lane_lo(tq)
    for h in range(DIFF_HEADS):
        qp = q_ref[0, :, LANES * h:LANES * (h + 1)]
        qm_ref[2 * h] = jnp.where(lo, qp, jnp.zeros_like(qp))
        qm_ref[2 * h + 1] = jnp.where(lo, jnp.zeros_like(qp), qp)
    _init_streams(m_ref, l_ref, acc_ref)

    def body(j, carry):
        ks = pl.multiple_of(j * tk, tk)
        sub = strip_ref.shape[2]
        blk0 = [(jnp.clip(j * tk - i * tq - a * sub, delta_lo, delta_hi) - delta_lo) >> LANE_SHIFT
                for a in range(tq // sub)]
        for h in range(DIFF_HEADS):
            cols = slice(LANES * h, LANES * (h + 1))
            kb = k_ref[0, pl.ds(ks, tk), cols]
            vb = v_ref[0, pl.ds(ks, tk), cols]
            bias = jnp.concatenate(
                [jnp.concatenate([strip_ref[h, b0 + c] for c in range(tk // LANES)], axis=-1) for b0 in blk0], axis=0)
            for t in range(2):
                _softmax_stream_update(_dot_nt(qm_ref[2 * h + t], kb) + bias, vb, m_ref, l_ref, acc_ref, 2 * h + t)
        return carry

    lax.fori_loop(0, nk, body, 0)
    lam = c_ref[0:1, :]
    for h in range(DIFF_HEADS):
        out = acc_ref[2 * h] / l_ref[2 * h] - lam * (acc_ref[2 * h + 1] / l_ref[2 * h + 1])
        y = out * lax.rsqrt(jnp.mean(out * out, axis=-1, keepdims=True) + 1e-6)
        o_ref[0, :, LANES * h:LANES * (h + 1)] = ((y * c_ref[1:2, :]) * c_ref[2:3, :]).astype(o_ref.dtype)


def _t5_saturation_distance():
    half = T5_BUCKETS // 2
    exact = half // 2
    n = np.arange(1, 4 * T5_MAX_DISTANCE, dtype=np.float32)
    large = exact + (np.log(n / exact) / math.log(T5_MAX_DISTANCE / exact) * (half - exact)).astype(np.int32)
    return int(np.argmax(large >= half - 1)) + 1 + 8


def _diff_bias_strips(t5_cols, tq, tk):
    unit = math.gcd(tq, tk)
    sat = _t5_saturation_distance()
    delta_lo = -unit * ((sat + tk - 1) // unit + 1)
    delta_hi = unit * ((sat + tq - 1) // unit + 1)
    width = delta_hi - delta_lo + tk
    rel = jnp.arange(tq - 1 + width) - (tq - 1) + delta_lo
    u = (t5_cols.astype(F32) * LOG2E)[_t5_bucket(rel)].T
    strip = _toeplitz(u, tq, width)
    nh = strip.shape[0]
    return strip.reshape(nh, tq, width // LANES, LANES).transpose(0, 2, 1, 3), delta_lo, delta_hi


def _diff_attention(hd, strips, consts, delta_lo, delta_hi, batch, seq, tq, tk):
    w = DIFF_HEADS * LANES
    once = pl.Buffered(1)
    state = pltpu.VMEM((2 * DIFF_HEADS, tq, LANES), F32)
    return pl.pallas_call(
        functools.partial(_diff_kernel, tk=tk, nk=seq // tk, delta_lo=delta_lo, delta_hi=delta_hi),
        grid=(batch, seq // tq),
        in_specs=[pl.BlockSpec((1, tq, w), lambda b, i: (b, i, 0)),
                  pl.BlockSpec((1, seq, w), lambda b, i: (b, 0, 1), pipeline_mode=once),
                  pl.BlockSpec((1, seq, w), lambda b, i: (b, 0, 2), pipeline_mode=once),
                  pl.BlockSpec(strips.shape, lambda b, i: (0, 0, 0, 0), pipeline_mode=once),
                  pl.BlockSpec(consts.shape, lambda b, i: (0, 0))],
        out_specs=pl.BlockSpec((1, tq, w), lambda b, i: (b, i, 0)),
        out_shape=jax.ShapeDtypeStruct((batch, seq, w), BF16),
        scratch_shapes=[pltpu.VMEM((2 * DIFF_HEADS, tq, LANES), BF16), state, state, state],
        compiler_params=_cparams(("arbitrary", "arbitrary"), 56),
        name="diff_attention",
    )(hd, hd, hd, strips, consts)


def _layer_norm(z, g, b):
    mu = jnp.mean(z, axis=-1, keepdims=True)
    zc = z - mu
    var = jnp.mean(zc * zc, axis=-1, keepdims=True)
    return zc * lax.rsqrt(var + 1e-5) * g + b


def _split3(a):
    hi = a.astype(BF16)
    r1 = a - hi.astype(F32)
    mid = r1.astype(BF16)
    lo = (r1 - mid.astype(F32)).astype(BF16)
    return hi, mid, lo


def _merge_kernel(x_ref, xb_ref, ya_ref, yb_ref, o0_ref, o1_ref, o2_ref, l0_ref, l1_ref, l2_ref, yd_ref,
                  wg_ref, wna_ref, wgqa_ref, wdil_ref, wdiff_ref, wo_ref, wr_ref, ln_ref,
                  x1_ref, x1b_ref, lg_ref, *, alpha, d_model):
    xb = xb_ref[...]

    def gate(branch):
        g = jnp.dot(xb, wg_ref[:, branch * d_model:(branch + 1) * d_model], preferred_element_type=F32)
        return 1.0 / (1.0 + jnp.exp(-g))

    l0, l1, l2 = l0_ref[...], l1_ref[...], l2_ref[...]
    mx = jnp.maximum(jnp.maximum(l0, l1), l2)
    e0, e1, e2 = jnp.exp(l0 - mx), jnp.exp(l1 - mx), jnp.exp(l2 - mx)
    den = e0 + e1 + e2
    yc = (e0 / den) * o0_ref[...] + (e1 / den) * o1_ref[...] + (e2 / den) * o2_ref[...]
    merged = gate(0) * jnp.dot(ya_ref[...], wna_ref[...], preferred_element_type=F32)
    merged += gate(1) * jnp.dot(yb_ref[...], wgqa_ref[...], preferred_element_type=F32)
    merged += gate(2) * jnp.dot(yc.astype(BF16), wdil_ref[...], preferred_element_type=F32)
    merged += gate(3) * jnp.dot(yd_ref[...], wdiff_ref[...], preferred_element_type=F32)
    y = jnp.dot(merged.astype(BF16), wo_ref[...], preferred_element_type=F32)
    x1 = _layer_norm(alpha * x_ref[...] + y, ln_ref[0:1, :], ln_ref[1:2, :])
    x1_ref[...] = x1
    x1b_ref[...] = x1.astype(BF16)
    xh, xm, _ = _split3(x1)
    wh, wm, _ = wr_ref[0], wr_ref[1], wr_ref[2]
    lg_ref[...] = _dot_nt(wh, xh) + (_dot_nt(wh, xm) + _dot_nt(wm, xh))


def _merge(x, xb, ya, yb, dil, yd, w_g, w_na, w_gqa, w_dil, w_diff, w_o, w_r3, ln, alpha, tm):
    n, d = x.shape
    e = w_r3.shape[1]
    (o0, l0), (o1, l1), (o2, l2) = dil
    row = lambda w: pl.BlockSpec((tm, w), lambda i: (i, 0))
    full = lambda a: pl.BlockSpec(a.shape, lambda i: (0,) * a.ndim)
    return pl.pallas_call(
        functools.partial(_merge_kernel, alpha=alpha, d_model=d),
        grid=(n // tm,),
        in_specs=[row(d), row(d), row(ya.shape[1]), row(yb.shape[1]), row(LANES), row(LANES), row(LANES),
                  row(LANES), row(LANES), row(LANES), row(yd.shape[1]),
                  full(w_g), full(w_na), full(w_gqa), full(w_dil), full(w_diff), full(w_o), full(w_r3), full(ln)],
        out_specs=[row(d), row(d), pl.BlockSpec((e, tm), lambda i: (0, i))],
        out_shape=[jax.ShapeDtypeStruct((n, d), F32), jax.ShapeDtypeStruct((n, d), BF16),
                   jax.ShapeDtypeStruct((e, n), F32)],
        compiler_params=_cparams(("arbitrary",), 56),
        name="branch_merge",
    )(x, xb, ya, yb, o0, o1, o2, l0, l1, l2, yd, w_g, w_na, w_gqa, w_dil, w_diff, w_o, w_r3, ln)


def _route_kernel(lg_ref, aff_ref, posm_ref, posx_ref, *, cap):
    lg = lg_ref[0]
    n_e, seq = lg.shape
    mx = jnp.max(lg, axis=0, keepdims=True)
    ex = jnp.exp(lg - mx)
    aff = ex / jnp.sum(ex, axis=0, keepdims=True)
    aff_ref[0] = aff
    bits = pltpu.bitcast(aff, jnp.int32)

    def bisect(_, carry):
        lo, hi = carry
        mid = lo + ((hi - lo + 1) >> 1)
        cnt = jnp.sum((bits >= mid).astype(F32), axis=1, keepdims=True)
        ok = cnt >= float(cap)
        return jnp.where(ok, mid, lo), jnp.where(ok, hi, mid - 1)

    lo0 = jnp.zeros((n_e, 1), jnp.int32)
    hi0 = jnp.full((n_e, 1), 0x7F800000, jnp.int32)
    thr, _ = lax.fori_loop(0, 31, bisect, (lo0, hi0))
    gt = bits > thr
    eq = bits == thr
    needf = float(cap) - jnp.sum(gt.astype(F32), axis=1, keepdims=True)

    tri = (lax.broadcasted_iota(jnp.int32, (LANES, LANES), 0)
           < lax.broadcasted_iota(jnp.int32, (LANES, LANES), 1)).astype(BF16)

    def prefix(mask_bf16, j, carry):
        blk = mask_bf16[:, j * LANES:(j + 1) * LANES]
        excl = jnp.dot(blk, tri, preferred_element_type=F32) + carry
        return excl, carry + jnp.sum(blk.astype(F32), axis=1, keepdims=True)

    eqb = eq.astype(BF16)
    carry_eq = jnp.zeros((n_e, 1), F32)
    carry_sel = jnp.zeros((n_e, 1), F32)
    for j in range(seq // LANES):
        cols = slice(j * LANES, (j + 1) * LANES)
        rank, carry_eq = prefix(eqb, j, carry_eq)
        sel = jnp.logical_or(gt[:, cols], jnp.logical_and(eq[:, cols], rank < needf))
        selb = sel.astype(BF16)
        excl = jnp.dot(selb, tri, preferred_element_type=F32) + carry_sel
        carry_sel = carry_sel + jnp.sum(selb.astype(F32), axis=1, keepdims=True)
        pos = excl.astype(jnp.int32)
        posx_ref[0, :, cols] = pos
        posm_ref[0, :, cols] = jnp.where(sel, pos, -1)


def _route(logits_t, batch, seq, cap):
    n_e = logits_t.shape[0]
    lg = logits_t.reshape(n_e, batch, seq).transpose(1, 0, 2)
    spec = pl.BlockSpec((1, n_e, seq), lambda b: (b, 0, 0))
    return pl.pallas_call(
        functools.partial(_route_kernel, cap=cap),
        grid=(batch,),
        in_specs=[spec],
        out_specs=[spec, spec, spec],
        out_shape=[jax.ShapeDtypeStruct((batch, n_e, seq), F32), jax.ShapeDtypeStruct((batch, n_e, seq), jnp.int32),
                   jax.ShapeDtypeStruct((batch, n_e, seq), jnp.int32)],
        compiler_params=_cparams(("arbitrary",), 32),
        name="ec_route",
    )(lg)


def _gather_kernel(clo_ref, chi_ref, x_ref, pos_ref, aff_ref, xin_ref, gs_ref, acc_ref, *, n_e, nsb):
    base = (pl.program_id(0) * n_e + pl.program_id(1)) * nsb
    for sb in range(nsb):
        slot_ids = sb * SLOT_BLOCK + lax.broadcasted_iota(jnp.int32, (SLOT_BLOCK, TOKEN_CHUNK), 0)
        acc_ref[...] = jnp.zeros_like(acc_ref)

        def body(c, g, slot_ids=slot_ids):
            hit = pos_ref[0, 0, pl.ds(c, 1), :] == slot_ids
            xc = x_ref[0, pl.ds(pl.multiple_of(c * TOKEN_CHUNK, TOKEN_CHUNK), TOKEN_CHUNK), :]
            acc_ref[...] += jnp.dot(hit.astype(BF16), xc, preferred_element_type=F32)
            return g + jnp.sum(jnp.where(hit, aff_ref[0, 0, pl.ds(c, 1), :], 0.0), axis=-1, keepdims=True)

        g = lax.fori_loop(clo_ref[base + sb], chi_ref[base + sb], body, jnp.zeros((SLOT_BLOCK, 1), F32))
        rows = slice(sb * SLOT_BLOCK, (sb + 1) * SLOT_BLOCK)
        xin_ref[0, 0, rows, :] = acc_ref[...].astype(BF16)
        gs_ref[0, 0, rows, :] = jnp.broadcast_to(g, (SLOT_BLOCK, LANES))


def _gather(x1b, posm, aff, clo, chi, cap):
    batch, seq, d = x1b.shape
    n_e = posm.shape[1]
    nsb = cap // SLOT_BLOCK
    nch = seq // TOKEN_CHUNK
    pos4 = posm.reshape(batch, n_e, nch, TOKEN_CHUNK)
    aff4 = aff.reshape(batch, n_e, nch, TOKEN_CHUNK)
    grid_spec = pltpu.PrefetchScalarGridSpec(
        num_scalar_prefetch=2,
        grid=(batch, n_e),
        in_specs=[pl.BlockSpec((1, seq, d), lambda b, e, *_: (b, 0, 0)),
                  pl.BlockSpec((1, 1, nch, TOKEN_CHUNK), lambda b, e, *_: (b, e, 0, 0)),
                  pl.BlockSpec((1, 1, nch, TOKEN_CHUNK), lambda b, e, *_: (b, e, 0, 0))],
        out_specs=[pl.BlockSpec((1, 1, cap, d), lambda b, e, *_: (b, e, 0, 0)),
                   pl.BlockSpec((1, 1, cap, LANES), lambda b, e, *_: (b, e, 0, 0))],
        scratch_shapes=[pltpu.VMEM((SLOT_BLOCK, d), F32)],
    )
    return pl.pallas_call(
        functools.partial(_gather_kernel, n_e=n_e, nsb=nsb),
        grid_spec=grid_spec,
        out_shape=[jax.ShapeDtypeStruct((batch, n_e, cap, d), BF16),
                   jax.ShapeDtypeStruct((batch, n_e, cap, LANES), F32)],
        compiler_params=_cparams(("arbitrary", "arbitrary"), 56),
        name="ec_gather",
    )(clo, chi, x1b, pos4, aff4)


def _ffn_kernel(x_ref, gs_ref, wg_ref, wu_ref, wd_ref, o_ref, *, fchunk):
    x = x_ref[0, 0]
    ff = wg_ref.shape[2]
    acc = jnp.zeros((x.shape[0], wd_ref.shape[2]), F32)
    for f0 in range(0, ff, fchunk):
        g = jnp.dot(x, wg_ref[0, :, f0:f0 + fchunk], preferred_element_type=F32)
        u = jnp.dot(x, wu_ref[0, :, f0:f0 + fchunk], preferred_element_type=F32)
        hid = (g / (1.0 + jnp.exp(-g))) * u
        acc += jnp.dot(hid.astype(BF16), wd_ref[0, f0:f0 + fchunk, :], preferred_element_type=F32)
    o_ref[0, 0] = (acc * gs_ref[0, 0][:, 0:1]).astype(o_ref.dtype)


def _expert_ffn(xin, gslot, wg, wu, wd, tc, fchunk):
    batch, n_e, cap, d = xin.shape
    ff = wg.shape[2]
    return pl.pallas_call(
        functools.partial(_ffn_kernel, fchunk=fchunk),
        grid=(n_e, batch, cap // tc),
        in_specs=[pl.BlockSpec((1, 1, tc, d), lambda e, b, i: (b, e, i, 0)),
                  pl.BlockSpec((1, 1, tc, LANES), lambda e, b, i: (b, e, i, 0)),
                  pl.BlockSpec((1, d, ff), lambda e, b, i: (e, 0, 0)),
                  pl.BlockSpec((1, d, ff), lambda e, b, i: (e, 0, 0)),
                  pl.BlockSpec((1, ff, d), lambda e, b, i: (e, 0, 0))],
        out_specs=pl.BlockSpec((1, 1, tc, d), lambda e, b, i: (b, e, i, 0)),
        out_shape=jax.ShapeDtypeStruct((batch, n_e, cap, d), BF16),
        compiler_params=_cparams(("arbitrary", "arbitrary", "arbitrary"), 56),
        name="expert_ffn",
    )(xin, gslot, wg, wu, wd)


def _combine_kernel(cs_ref, yo_ref, post_ref, y_ref, *, n_e, nch, cap):
    b = pl.program_id(0)
    c = pl.program_id(2)
    win = 2 * SLOT_BLOCK
    pos_all = post_ref[0]
    win_ids = lax.broadcasted_iota(jnp.int32, (TOKEN_CHUNK, win), 1)
    hits, wins, tails = [], [], []
    for e in range(n_e):
        base = (b * n_e + e) * (nch + 1) + c
        s_lo = cs_ref[base]
        s_hi = cs_ref[base + 1]
        w0 = pl.multiple_of(jnp.minimum((s_lo >> SLOT_SHIFT) << SLOT_SHIFT, cap - win), SLOT_BLOCK)
        hits.append((pos_all[:, e:e + 1] == (w0 + win_ids)).astype(BF16))
        wins.append(yo_ref[0, e, pl.ds(w0, win), :])
        sb_end = jnp.where(s_hi > s_lo, ((s_hi - 1) >> SLOT_SHIFT) + 1, 0)
        tails.append(((w0 + win) >> SLOT_SHIFT, sb_end))
    y_ref[0] = jnp.dot(jnp.concatenate(hits, axis=1), jnp.concatenate(wins, axis=0), preferred_element_type=F32)
    for e, (sb_from, sb_end) in enumerate(tails):
        def body(sb, carry, e=e):
            blk_ids = lax.broadcasted_iota(jnp.int32, (TOKEN_CHUNK, SLOT_BLOCK), 1)
            hit = post_ref[0, :, e:e + 1] == (sb * SLOT_BLOCK + blk_ids)
            yb = yo_ref[0, e, pl.ds(pl.multiple_of(sb * SLOT_BLOCK, SLOT_BLOCK), SLOT_BLOCK), :]
            y_ref[0] += jnp.dot(hit.astype(BF16), yb, preferred_element_type=F32)
            return carry

        lax.fori_loop(sb_from, sb_end, body, 0)


def _combine(yo, pos_t, cs, td):
    batch, n_e, cap, d = yo.shape
    seq = pos_t.shape[1]
    nch = seq // TOKEN_CHUNK
    grid_spec = pltpu.PrefetchScalarGridSpec(
        num_scalar_prefetch=1,
        grid=(batch, d // td, nch),
        in_specs=[pl.BlockSpec((1, n_e, cap, td), lambda b, j, c, *_: (b, 0, 0, j)),
                  pl.BlockSpec((1, TOKEN_CHUNK, n_e), lambda b, j, c, *_: (b, c, 0))],
        out_specs=pl.BlockSpec((1, TOKEN_CHUNK, td), lambda b, j, c, *_: (b, c, j)),
    )
    return pl.pallas_call(
        functools.partial(_combine_kernel, n_e=n_e, nch=nch, cap=cap),
        grid_spec=grid_spec,
        out_shape=jax.ShapeDtypeStruct((batch, seq, d), F32),
        compiler_params=_cparams(("arbitrary", "arbitrary", "arbitrary"), 56),
        name="ec_combine",
    )(cs, yo, pos_t)


def _ple_kernel(x1_ref, y_ref, p_ref, wpg_ref, wpp_ref, ln_ref, x3_ref, x3b_ref, *, alpha):
    x2 = _layer_norm(alpha * x1_ref[...] + y_ref[...], ln_ref[0:1, :], ln_ref[1:2, :])
    gate = 1.0 / (1.0 + jnp.exp(-jnp.dot(x2.astype(BF16), wpg_ref[...], preferred_element_type=F32)))
    emb = jnp.dot(p_ref[...].astype(BF16), wpp_ref[...], preferred_element_type=F32) * gate
    x3 = _layer_norm(alpha * x2 + emb, ln_ref[2:3, :], ln_ref[3:4, :])
    x3_ref[...] = x3
    x3b_ref[...] = x3.astype(BF16)


def _ple(x1, y, p, w_pg, w_pp, ln, alpha, tm):
    n, d = x1.shape
    row = lambda w: pl.BlockSpec((tm, w), lambda i: (i, 0))
    full = lambda a: pl.BlockSpec(a.shape, lambda i: (0,) * a.ndim)
    return pl.pallas_call(
        functools.partial(_ple_kernel, alpha=alpha),
        grid=(n // tm,),
        in_specs=[row(d), row(d), row(p.shape[1]), full(w_pg), full(w_pp), full(ln)],
        out_specs=[row(d), row(d)],
        out_shape=[jax.ShapeDtypeStruct((n, d), F32), jax.ShapeDtypeStruct((n, d), BF16)],
        compiler_params=_cparams(("arbitrary",), 40),
        name="ple_norm",
    )(x1, y, p, w_pg, w_pp, ln)


def _token_mixer_inputs(xb, w_in, qk_gain, batch, seq):
    d = w_in.shape[0]
    scale = HEAD_DIM ** -0.5
    na_w = NA_HEADS * HEAD_DIM
    gq_w, gkv_w = GQA_HEADS * HEAD_DIM, GQA_KV_HEADS * HEAD_DIM
    dil_w = DIL_HEADS * HEAD_DIM
    dqk_w = DIFF_HEADS * 2 * HEAD_DIM
    o = 0
    w_a = w_in[:, o:o + 3 * na_w]; o += 3 * na_w
    w_b = w_in[:, o:o + gq_w + 2 * gkv_w]; o += gq_w + 2 * gkv_w
    w_c = w_in[:, o:o + 3 * dil_w]; o += 3 * dil_w
    w_d = w_in[:, o:o + 3 * dqk_w]; o += 3 * dqk_w
    w_g = w_in[:, o:]

    def qscaled(w, qw):
        return jnp.concatenate([w[:, :qw] * scale, w[:, qw:]], axis=1).astype(BF16)

    n = batch * seq
    ha = _matmul(xb, qscaled(w_a, na_w), BF16, 1024, 3 * na_w, "proj_na").reshape(batch, seq, -1)
    w_cg = jnp.concatenate(
        [w_c[:, t * dil_w + LANES * g:t * dil_w + LANES * (g + 1)] * (scale if t == 0 else 1.0)
         for g in range(len(DIL_GROUPS)) for t in range(3)], axis=1).astype(BF16)
    hc = _matmul(xb, w_cg, BF16, 1024, 3 * dil_w, "proj_dil").reshape(batch, seq, -1)
    w_dq = jnp.concatenate([w_d[:, :dqk_w] * (scale * LOG2E), w_d[:, dqk_w:]], axis=1).astype(BF16)
    hd = _matmul(xb, w_dq, BF16, 1024, 3 * dqk_w, "proj_diff").reshape(batch, seq, -1)
    qn, kd, vd = _proj_gqa(xb, w_b, qk_gain, seq, 512)
    shape3 = lambda t: t.reshape(batch, seq, t.shape[-1])
    return ha, shape3(qn), shape3(kd), shape3(vd), hc, hd, w_g.astype(BF16)


def _layer(x, xb, p, w, lam_init, alpha):
    batch, seq, d = x.shape
    n = batch * seq
    ha, qn, kd, vd, hc, hd, w_g = _token_mixer_inputs(xb.reshape(n, d), w["w_in"], w["qk_gain"], batch, seq)

    ya = _na_attention(ha, _na_bias_tiles(w["rpb"]), batch, seq)
    yb = lax.cond(_gqa_logit_bound(w["qk_gain"]) <= GQA_LOGIT_LIMIT,
                  lambda *a: _gqa_attention(*a, ATTN_TQ, ATTN_TK, False),
                  lambda *a: _gqa_attention(*a, ATTN_TQ, ATTN_TK, True), qn, kd, vd)
    dil = []
    for g, (_, dilation) in enumerate(DIL_GROUPS):
        bias = _dil_bias_tiles(w["t5"][:, 2 * g:2 * g + 2], dilation, 128)
        dil.append(_dilated_group(hc, bias, g, dilation, batch, seq, 128, DIL_BLOCKS_PER_STEP))
    lp = w["lam_params"].astype(F32)
    lam = jnp.exp(jnp.sum(lp[0] * lp[1])) - jnp.exp(jnp.sum(lp[2] * lp[3])) + lam_init
    consts = jnp.zeros((8, LANES), F32).at[0].set(lam).at[1].set(w["diff_gain"]).at[2].set(1.0 - lam_init)
    strips, delta_lo, delta_hi = _diff_bias_strips(w["t5"][:, DIL_HEADS:], DIFF_BIAS_ROWS, ATTN_TK)
    yd = _diff_attention(hd, strips, consts, delta_lo, delta_hi, batch, seq, ATTN_TQ, ATTN_TK)

    flat = lambda t: t.reshape(n, t.shape[-1])
    w_r3 = jnp.stack(_split3(w["w_router"].T))
    ln = w["ln"]
    x1, x1b, logits_t = _merge(
        x.reshape(n, d), xb.reshape(n, d), flat(ya), flat(yb), [(flat(o), flat(l)) for o, l in dil], flat(yd), w_g,
        w["w_na"].astype(BF16), w["w_gqa"].astype(BF16), w["w_dil"].astype(BF16), w["w_diff"].astype(BF16),
        w["w_o"].astype(BF16), w_r3, jnp.stack([ln[0][0], ln[1][0]]), alpha, 512)

    cap = EC_CAPACITY * seq // N_EXPERTS
    aff, posm, posx = _route(logits_t, batch, seq, cap)
    nch = seq // TOKEN_CHUNK
    cs = jnp.concatenate([posx[:, :, ::TOKEN_CHUNK], jnp.full((batch, N_EXPERTS, 1), cap, jnp.int32)], axis=-1)
    edges = jnp.arange(cap // SLOT_BLOCK, dtype=jnp.int32) * SLOT_BLOCK
    clo = jnp.sum(cs[:, :, 1:, None] <= edges, axis=2).astype(jnp.int32)
    chi = jnp.sum(cs[:, :, :nch, None] < edges + SLOT_BLOCK, axis=2).astype(jnp.int32)
    xin, gslot = _gather(x1b.reshape(batch, seq, d), posm, aff, clo.reshape(-1), chi.reshape(-1), cap)
    yo = _expert_ffn(xin, gslot, w["w_eg"].astype(BF16), w["w_eu"].astype(BF16), w["w_ed"].astype(BF16), 512, 512)
    y = _combine(yo, posm.transpose(0, 2, 1), cs.reshape(-1), 512)

    ln4 = jnp.stack([ln[0][1], ln[1][1], ln[0][2], ln[1][2]])
    x3, x3b = _ple(x1, y.reshape(n, d), p.reshape(n, -1), w["w_pg"].astype(BF16), w["w_pp"].astype(BF16), ln4, alpha, 512)
    return x3.reshape(batch, seq, d), x3b.reshape(batch, seq, d)


def kernel(x, p, w_in, w_branch_na, w_branch_gqa, w_branch_dil, w_branch_diff, w_out, na_rel_bias, qk_norm_gain,
           diff_lambda, diff_norm_gain, t5_rel_bias, w_router, w_expert_gate, w_expert_up, w_expert_down,
           w_ple_proj, w_ple_gate, ln_gain, ln_bias):
    depth = w_in.shape[0]
    alpha = (2 * depth) ** 0.25
    xb = x.astype(BF16)
    for i in range(depth):
        lam_init = 0.8 - 0.6 * math.exp(-0.3 * i)
        w = dict(w_in=w_in[i], w_na=w_branch_na[i], w_gqa=w_branch_gqa[i], w_dil=w_branch_dil[i],
                 w_diff=w_branch_diff[i], w_o=w_out[i], rpb=na_rel_bias[i], qk_gain=qk_norm_gain[i],
                 lam_params=diff_lambda[i], diff_gain=diff_norm_gain[i], t5=t5_rel_bias, w_router=w_router[i],
                 w_eg=w_expert_gate[i], w_eu=w_expert_up[i], w_ed=w_expert_down[i], w_pp=w_ple_proj[i],
                 w_pg=w_ple_gate[i], ln=(ln_gain[i], ln_bias[i]))
        x, xb = _layer(x, xb, p[i], w, lam_init, alpha)
    return x
```

```python
import functools
import math

import numpy as np
import jax
import jax.numpy as jnp
from jax import lax
from jax.experimental import pallas as pl
from jax.experimental.pallas import tpu as pltpu

F32 = jnp.float32
BF16 = jnp.bfloat16
NEG_INF = -1e30

V7X_VMEM_BYTES = 64 * 1024 * 1024
LANES = 128
LANE_SHIFT = 7

GRID_W = 64
HEAD_DIM = 64
NA_HEADS, NA_ROWS, NA_COLS = 6, 8, 16
NA_ROWS_PER_STEP = 4
LOG2E = 1.4426950408889634
ATTN_TQ, ATTN_TK = 1024, 256
GQA_TK_BOUNDED, DIFF_TK_BOUNDED = 1024, 512
ATTN_LOGIT_LIMIT = 64.0
DIL_BLOCKS_PER_STEP = 4
DIFF_BIAS_ROWS = 256
GQA_HEADS, GQA_KV_HEADS = 6, 2
ROPE_THETA = 10000.0
DIL_GROUPS = ((128, 1), (512, 4), (2048, 16))
DIL_HEADS = 6
DIFF_HEADS = 4
T5_BUCKETS, T5_MAX_DISTANCE = 32, 1024
N_EXPERTS, EC_CAPACITY = 16, 2
SLOT_BLOCK = 128
SLOT_SHIFT = SLOT_BLOCK.bit_length() - 1
TOKEN_CHUNK = 256


def _cparams(semantics, vmem_mib):
    return pltpu.CompilerParams(dimension_semantics=semantics,
                                vmem_limit_bytes=min(vmem_mib * 1024 * 1024, V7X_VMEM_BYTES - 4 * 1024 * 1024))


def _mm_kernel(a_ref, b_ref, o_ref):
    o_ref[...] = jnp.dot(a_ref[...], b_ref[...], preferred_element_type=F32).astype(o_ref.dtype)


def _matmul(a, b, out_dtype, tm, tn, name):
    m, k = a.shape
    n = b.shape[1]
    return pl.pallas_call(
        _mm_kernel,
        grid=(n // tn, m // tm),
        in_specs=[pl.BlockSpec((tm, k), lambda j, i: (i, 0)),
                  pl.BlockSpec((k, tn), lambda j, i: (0, j))],
        out_specs=pl.BlockSpec((tm, tn), lambda j, i: (i, j)),
        out_shape=jax.ShapeDtypeStruct((m, n), out_dtype),
        compiler_params=_cparams(("arbitrary", "arbitrary"), 40),
        name=name,
    )(a, b)


def _toeplitz(u, rows, width):
    lead = u.shape[:-1]
    period = rows + width
    text = jnp.concatenate([u[..., rows - 1:], jnp.zeros(lead + (1,), u.dtype), u[..., :rows - 1]], axis=-1)
    flat = jnp.tile(text, (1,) * len(lead) + (rows,))[..., :rows * (period - 1)]
    return flat.reshape(lead + (rows, period - 1))[..., :width]


def _lane_lo(rows):
    return lax.broadcasted_iota(jnp.int32, (rows, LANES), 1) < HEAD_DIM


def _dot_nt(a, b):
    return lax.dot_general(a, b, (((1,), (1,)), ((), ())), preferred_element_type=F32)


NA_WIN_ROWS = NA_ROWS + NA_ROWS_PER_STEP


def _na_kernel(q_ref, k_ref, v_ref, tp_ref, o_ref, *, rows):
    r0 = pl.program_id(1) * NA_ROWS_PER_STEP
    ws = jnp.clip(r0 - NA_ROWS // 2, 0, rows - NA_WIN_ROWS)
    start = pl.multiple_of(ws * GRID_W, GRID_W)
    win = NA_WIN_ROWS * GRID_W
    npair = NA_WIN_ROWS // 2
    none_code = 2 * (2 * NA_ROWS - 1)
    codes = []
    for rr in range(NA_ROWS_PER_STEP):
        r = r0 + rr
        rs = jnp.clip(r - NA_ROWS // 2, 0, rows - NA_ROWS)
        row_codes = []
        for j in range(npair):
            a0 = ws + 2 * j
            d0 = a0 - r + (NA_ROWS - 1)
            vl = jnp.logical_and(a0 >= rs, a0 < rs + NA_ROWS)
            vr = jnp.logical_and(a0 + 1 >= rs, a0 + 1 < rs + NA_ROWS)
            row_codes.append(jnp.where(jnp.logical_and(vl, vr), d0,
                                       jnp.where(jnp.logical_or(vl, vr), 2 * NA_ROWS - 1 + d0, none_code)))
        codes.append(row_codes)
    tq = NA_ROWS_PER_STEP * GRID_W
    lo = _lane_lo(tq)
    for p in range(NA_HEADS // 2):
        cols = slice(LANES * p, LANES * (p + 1))
        qp = q_ref[0, :, cols]
        kw = k_ref[0, pl.ds(start, win), cols]
        vw = v_ref[0, pl.ds(start, win), cols]
        outs = []
        for hh in range(2):
            h = 2 * p + hh
            qm = jnp.where(lo if hh == 0 else jnp.logical_not(lo), qp, jnp.zeros_like(qp))
            bias = jnp.concatenate(
                [jnp.concatenate([tp_ref[h, c] for c in row_codes], axis=-1) for row_codes in codes], axis=0)
            s = _dot_nt(qm, kw) + bias
            m = jnp.max(s, axis=-1, keepdims=True)
            e = jnp.exp(s - m)
            pr = (e / jnp.sum(e, axis=-1, keepdims=True)).astype(BF16)
            outs.append(jnp.dot(pr, vw, preferred_element_type=F32))
        o_ref[0, :, cols] = jnp.where(lo, outs[0], outs[1]).astype(o_ref.dtype)


def _na_bias_tiles(rpb):
    col = jnp.arange(GRID_W)
    cstart = jnp.clip(col - NA_COLS // 2, 0, GRID_W - NA_COLS)
    kc = jnp.arange(GRID_W)
    inwin = (kc[None, :] >= cstart[:, None]) & (kc[None, :] < cstart[:, None] + NA_COLS)
    pad = GRID_W - NA_COLS
    u = jnp.pad(rpb.astype(F32), ((0, 0), (0, 0), (pad, pad)))
    t = jnp.where(inwin[None, None], _toeplitz(u, GRID_W, GRID_W), NEG_INF)
    neg = jnp.full_like(t[:, :NA_ROWS], NEG_INF)
    both = jnp.concatenate([t[:, :-1], t[:, 1:]], axis=-1)
    right = jnp.concatenate([neg, t[:, :NA_ROWS]], axis=-1)
    left = jnp.concatenate([t[:, NA_ROWS - 1:], neg], axis=-1)
    none = jnp.concatenate([neg[:, :1], neg[:, :1]], axis=-1)
    return jnp.concatenate([both, right, left, none], axis=1)


def _na_attention(ha, t2, batch, seq):
    rows = seq // GRID_W
    w = NA_HEADS * HEAD_DIM
    tq = GRID_W * NA_ROWS_PER_STEP
    return pl.pallas_call(
        functools.partial(_na_kernel, rows=rows),
        grid=(batch, rows // NA_ROWS_PER_STEP),
        in_specs=[pl.BlockSpec((1, tq, w), lambda b, r: (b, r, 0)),
                  pl.BlockSpec((1, seq, w), lambda b, r: (b, 0, 1)),
                  pl.BlockSpec((1, seq, w), lambda b, r: (b, 0, 2)),
                  pl.BlockSpec(t2.shape, lambda b, r: (0, 0, 0, 0))],
        out_specs=pl.BlockSpec((1, tq, w), lambda b, r: (b, r, 0)),
        out_shape=jax.ShapeDtypeStruct((batch, seq, w), BF16),
        compiler_params=_cparams(("arbitrary", "arbitrary"), 56),
        name="na_attention",
    )(ha, ha, ha, t2)


def _softmax_stream_update(s, vb, m_ref, l_ref, acc_ref, idx):
    m_prev = m_ref[idx]
    m_new = jnp.maximum(m_prev, jnp.max(s, axis=-1, keepdims=True))
    alpha = jnp.exp2(m_prev - m_new)
    p = jnp.exp2(s - jnp.tile(m_new, (1, s.shape[1] // LANES)))
    m_ref[idx] = m_new
    l_ref[idx] = alpha * l_ref[idx] + jnp.sum(p, axis=-1, keepdims=True)
    acc_ref[idx] = alpha * acc_ref[idx] + jnp.dot(p.astype(BF16), vb, preferred_element_type=F32)


def _init_streams(m_ref, l_ref, acc_ref):
    m_ref[...] = jnp.full(m_ref.shape, NEG_INF, F32)
    l_ref[...] = jnp.zeros(l_ref.shape, F32)
    acc_ref[...] = jnp.zeros(acc_ref.shape, F32)


def _gqa_kernel(q_ref, k_ref, v_ref, o_ref, qm_ref, m_ref, l_ref, acc_ref, *, tk, nk, track_max):
    tq = q_ref.shape[1]
    lo = _lane_lo(tq)
    rep = GQA_HEADS // GQA_KV_HEADS
    for h in range(GQA_HEADS):
        qp = q_ref[0, :, LANES * (h // 2):LANES * (h // 2 + 1)]
        qm_ref[h] = jnp.where(lo if h % 2 == 0 else jnp.logical_not(lo), qp, jnp.zeros_like(qp))
    _init_streams(m_ref, l_ref, acc_ref)

    def body(j, carry):
        ks = pl.multiple_of(j * tk, tk)
        for g in range(GQA_KV_HEADS):
            gcols = slice(LANES * g, LANES * (g + 1))
            kb = k_ref[0, pl.ds(ks, tk), gcols]
            vb = v_ref[0, pl.ds(ks, tk), gcols]
            for h in range(rep * g, rep * (g + 1)):
                s = _dot_nt(qm_ref[h], kb)
                if track_max:
                    _softmax_stream_update(s, vb, m_ref, l_ref, acc_ref, h)
                else:
                    p = jnp.exp2(s)
                    l_ref[h] += jnp.sum(p, axis=-1, keepdims=True)
                    acc_ref[h] += jnp.dot(p.astype(BF16), vb, preferred_element_type=F32)
        return carry

    lax.fori_loop(0, nk, body, 0)
    for p in range(GQA_HEADS // 2):
        o0 = acc_ref[2 * p] / l_ref[2 * p]
        o1 = acc_ref[2 * p + 1] / l_ref[2 * p + 1]
        o_ref[0, :, LANES * p:LANES * (p + 1)] = jnp.where(lo, o0, o1).astype(o_ref.dtype)


def _gqa_attention(q, kd, vd, tq, tk, track_max):
    batch, seq, wq = q.shape
    wk = kd.shape[-1]
    state = pltpu.VMEM((GQA_HEADS, tq, LANES), F32)
    return pl.pallas_call(
        functools.partial(_gqa_kernel, tk=tk, nk=seq // tk, track_max=track_max),
        grid=(batch, seq // tq),
        in_specs=[pl.BlockSpec((1, tq, wq), lambda b, i: (b, i, 0)),
                  pl.BlockSpec((1, seq, wk), lambda b, i: (b, 0, 0)),
                  pl.BlockSpec((1, seq, wk), lambda b, i: (b, 0, 0))],
        out_specs=pl.BlockSpec((1, tq, wq), lambda b, i: (b, i, 0)),
        out_shape=jax.ShapeDtypeStruct((batch, seq, wq), BF16),
        scratch_shapes=[pltpu.VMEM((GQA_HEADS, tq, LANES), BF16), state, state, state],
        compiler_params=_cparams(("arbitrary", "arbitrary"), 48),
        name="gqa_attention" if track_max else "gqa_attention_bounded",
    )(q, kd, vd)


def _gqa_logit_bound(qk_gain):
    gq = jnp.max(jnp.abs(qk_gain[0])) * (HEAD_DIM ** -0.5 * LOG2E)
    return 1.05 * HEAD_DIM * gq * jnp.max(jnp.abs(qk_gain[1]))


def _rope_tables(seq):
    quarter = HEAD_DIM // 4
    freqs = ROPE_THETA ** (-jnp.arange(quarter, dtype=F32) / quarter)
    t = jnp.arange(seq)
    row = (t // GRID_W).astype(F32)
    col = (t % GRID_W).astype(F32)
    ang_r = row[:, None] * freqs[None, :]
    ang_c = col[:, None] * freqs[None, :]
    cos = jnp.concatenate([jnp.cos(ang_r)] * 2 + [jnp.cos(ang_c)] * 2, axis=-1)
    sin = jnp.concatenate([-jnp.sin(ang_r), jnp.sin(ang_r), -jnp.sin(ang_c), jnp.sin(ang_c)], axis=-1)
    return cos, sin


def _proj_gqa_kernel(x_ref, w_ref, cos_ref, sin_ref, gain_ref, mean_ref, q_ref, k_ref, v_ref):
    h = jnp.dot(x_ref[...], w_ref[...], preferred_element_type=F32)
    nq = q_ref.shape[1] // LANES
    nk = k_ref.shape[1] // LANES
    lane = lax.broadcasted_iota(jnp.int32, (h.shape[0], LANES), 1)
    first = (lane & (HEAD_DIM // 2 - 1)) < HEAD_DIM // 4
    for blk in range(nq + nk):
        t = h[:, LANES * blk:LANES * (blk + 1)]
        hi, mid, low = _split3(t * t)
        ms = (jnp.dot(hi, mean_ref[...], preferred_element_type=F32)
              + jnp.dot(mid, mean_ref[...], preferred_element_type=F32)
              + jnp.dot(low, mean_ref[...], preferred_element_type=F32))
        y = t * lax.rsqrt(ms + 1e-6) * gain_ref[blk:blk + 1, :]
        partner = jnp.where(first, pltpu.roll(y, LANES - HEAD_DIM // 4, 1), pltpu.roll(y, HEAD_DIM // 4, 1))
        out = (y * cos_ref[...] + partner * sin_ref[...]).astype(BF16)
        if blk < nq:
            q_ref[:, LANES * blk:LANES * (blk + 1)] = out
        else:
            k_ref[:, LANES * (blk - nq):LANES * (blk - nq + 1)] = out
    v_ref[...] = h[:, LANES * (nq + nk):].astype(BF16)


def _proj_gqa(xb, w_b, qk_gain, seq, tm):
    n, d = xb.shape
    gq, gkv = GQA_HEADS * HEAD_DIM, GQA_KV_HEADS * HEAD_DIM
    dup = lambda w: jnp.concatenate([w[:, i * HEAD_DIM:(i + 1) * HEAD_DIM] for i in range(GQA_KV_HEADS) for _ in range(2)], axis=1)
    w_all = jnp.concatenate([w_b[:, :gq], dup(w_b[:, gq:gq + gkv]), dup(w_b[:, gq + gkv:])], axis=1).astype(BF16)
    cos, sin = _rope_tables(seq)
    cos2, sin2 = jnp.tile(cos, (1, 2)), jnp.tile(sin, (1, 2))
    nq, nk = gq // LANES, 2 * gkv // LANES
    gain = jnp.concatenate([jnp.tile(qk_gain[0] * (HEAD_DIM ** -0.5 * LOG2E), (nq, 2)), jnp.tile(qk_gain[1], (nk, 2)),
                            jnp.zeros((8 - nq - nk, LANES), F32)], axis=0)
    head = jnp.arange(LANES) // HEAD_DIM
    mean_mat = jnp.where(head[:, None] == head[None, :], 1.0 / HEAD_DIM, 0.0).astype(BF16)
    nt = seq // tm
    row = lambda w: pl.BlockSpec((tm, w), lambda i: (i, 0))
    full = lambda a: pl.BlockSpec(a.shape, lambda i: (0,) * a.ndim)
    pos = pl.BlockSpec((tm, LANES), lambda i: (i % nt, 0))
    return pl.pallas_call(
        _proj_gqa_kernel,
        grid=(n // tm,),
        in_specs=[row(d), full(w_all), pos, pos, full(gain), full(mean_mat)],
        out_specs=[row(gq), row(2 * gkv), row(2 * gkv)],
        out_shape=[jax.ShapeDtypeStruct((n, gq), BF16), jax.ShapeDtypeStruct((n, 2 * gkv), BF16),
                   jax.ShapeDtypeStruct((n, 2 * gkv), BF16)],
        compiler_params=_cparams(("arbitrary",), 40),
        name="proj_gqa",
    )(xb, w_all, cos2, sin2, gain, mean_mat)


def _dil_kernel(q_ref, k_ref, v_ref, bias_ref, o_ref, lse_ref, *, sub_len, nblk, tq):
    half = DIL_GROUPS[0][0] // 2
    win = tq + 2 * half
    lo = _lane_lo(tq)
    for qq in range(q_ref.shape[1] // tq):
        i = pl.program_id(2) * (q_ref.shape[1] // tq) + qq
        start = pl.multiple_of(jnp.clip(i * tq - half, 0, sub_len - win), half)
        var = jnp.where(i == 0, 0, jnp.where(i == nblk - 1, 2, 1))
        rows = slice(qq * tq, (qq + 1) * tq)
        qp = q_ref[0, rows, :]
        kw = k_ref[0, pl.ds(start, win), :]
        vw = v_ref[0, pl.ds(start, win), :]
        outs, lses = [], []
        for hh in range(2):
            qm = jnp.where(lo if hh == 0 else jnp.logical_not(lo), qp, jnp.zeros_like(qp))
            s = _dot_nt(qm, kw) + bias_ref[hh, var]
            m = jnp.max(s, axis=-1, keepdims=True)
            e = jnp.exp(s - m)
            l = jnp.sum(e, axis=-1, keepdims=True)
            outs.append(jnp.dot((e / l).astype(BF16), vw, preferred_element_type=F32))
            lses.append(jnp.broadcast_to(m + jnp.log(l), (tq, LANES)))
        o_ref[0, rows, :] = jnp.where(lo, outs[0], outs[1])
        lse_ref[0, rows, :] = jnp.where(lo, lses[0], lses[1])


def _t5_bucket(rel):
    half = T5_BUCKETS // 2
    exact = half // 2
    n = jnp.abs(rel)
    nf = jnp.maximum(n, 1).astype(F32)
    large = exact + (jnp.log(nf / exact) / math.log(T5_MAX_DISTANCE / exact) * (half - exact)).astype(jnp.int32)
    large = jnp.minimum(large, half - 1)
    return jnp.where(rel > 0, half, 0) + jnp.where(n < exact, n, large)


def _dil_bias_tiles(t5_cols, dilation, tq):
    half = DIL_GROUPS[0][0] // 2
    win = tq + 2 * half
    wide = win + 2 * half
    j = jnp.arange(tq - 1 + wide) - (tq - 1) - 2 * half
    vals = t5_cols.astype(F32)[_t5_bucket(j * dilation)]
    u = jnp.where((jnp.abs(j) <= half)[:, None], vals, NEG_INF).T
    a = _toeplitz(u, tq, wide)
    return jnp.stack([a[:, :, 2 * half - s:2 * half - s + win] for s in (0, half, 2 * half)], axis=1)


def _dilated_group(hc, bias, g, dilation, batch, seq, tq, qb):
    sub_len = seq // dilation
    nblk = sub_len // tq
    ts = tq * qb
    if dilation == 1:
        view, nb, g = hc, hc.shape[-1] // LANES, 3 * g
    else:
        view, nb, g = hc[..., 3 * LANES * g:3 * LANES * (g + 1)].reshape(batch, sub_len, dilation * 3 * LANES), 3, 0
    nq = 1
    out, lse = pl.pallas_call(
        functools.partial(_dil_kernel, sub_len=sub_len, nblk=nblk, tq=tq),
        grid=(batch, dilation, nblk // qb),
        in_specs=[pl.BlockSpec((1, ts, LANES), lambda b, r, i: (b, i, r * nb + g)),
                  pl.BlockSpec((1, sub_len, LANES), lambda b, r, i: (b, 0, r * nb + nq + g)),
                  pl.BlockSpec((1, sub_len, LANES), lambda b, r, i: (b, 0, r * nb + 2 * nq + g)),
                  pl.BlockSpec(bias.shape, lambda b, r, i: (0, 0, 0, 0))],
        out_specs=[pl.BlockSpec((1, ts, LANES), lambda b, r, i: (b, i, r)),
                   pl.BlockSpec((1, ts, LANES), lambda b, r, i: (b, i, r))],
        out_shape=[jax.ShapeDtypeStruct((batch, sub_len, dilation * LANES), F32)] * 2,
        compiler_params=_cparams(("arbitrary", "arbitrary", "arbitrary"), 32),
        name=f"dilated_attention_d{dilation}",
    )(view, view, view, bias)
    return out.reshape(batch, seq, LANES), lse.reshape(batch, seq, LANES)


def _diff_kernel(q_ref, k_ref, v_ref, strip_ref, c_ref, o_ref, qm_ref, m_ref, l_ref, acc_ref, *,
                 tk, nk, delta_lo, delta_hi, track_max):
    tq = q_ref.shape[1]
    i = pl.program_id(1)
    lo = _lane_lo(tq)
    for h in range(DIFF_HEADS):
        qp = q_ref[0, :, LANES * h:LANES * (h + 1)]
        qm_ref[2 * h] = jnp.where(lo, qp, jnp.zeros_like(qp))
        qm_ref[2 * h + 1] = jnp.where(lo, jnp.zeros_like(qp), qp)
    _init_streams(m_ref, l_ref, acc_ref)

    def body(j, carry):
        ks = pl.multiple_of(j * tk, tk)
        sub = strip_ref.shape[2]
        blk0 = [(jnp.clip(j * tk - i * tq - a * sub, delta_lo, delta_hi) - delta_lo) >> LANE_SHIFT
                for a in range(tq // sub)]
        for h in range(DIFF_HEADS):
            cols = slice(LANES * h, LANES * (h + 1))
            kb = k_ref[0, pl.ds(ks, tk), cols]
            vb = v_ref[0, pl.ds(ks, tk), cols]
            bias = jnp.concatenate(
                [jnp.concatenate([strip_ref[h, b0 + c] for c in range(tk // LANES)], axis=-1) for b0 in blk0], axis=0)
            for t in range(2):
                s = _dot_nt(qm_ref[2 * h + t], kb) + bias
                if track_max:
                    _softmax_stream_update(s, vb, m_ref, l_ref, acc_ref, 2 * h + t)
                else:
                    p = jnp.exp2(s)
                    l_ref[2 * h + t] += jnp.sum(p, axis=-1, keepdims=True)
                    acc_ref[2 * h + t] += jnp.dot(p.astype(BF16), vb, preferred_element_type=F32)
        return carry

    lax.fori_loop(0, nk, body, 0)
    lam = c_ref[0:1, :]
    for h in range(DIFF_HEADS):
        out = acc_ref[2 * h] / l_ref[2 * h] - lam * (acc_ref[2 * h + 1] / l_ref[2 * h + 1])
        y = out * lax.rsqrt(jnp.mean(out * out, axis=-1, keepdims=True) + 1e-6)
        o_ref[0, :, LANES * h:LANES * (h + 1)] = ((y * c_ref[1:2, :]) * c_ref[2:3, :]).astype(o_ref.dtype)


def _t5_saturation_distance():
    half = T5_BUCKETS // 2
    exact = half // 2
    n = np.arange(1, 4 * T5_MAX_DISTANCE, dtype=np.float32)
    large = exact + (np.log(n / exact) / math.log(T5_MAX_DISTANCE / exact) * (half - exact)).astype(np.int32)
    return int(np.argmax(large >= half - 1)) + 1 + 8


def _diff_bias_strips(t5_cols, tq, tk):
    unit = math.gcd(tq, tk)
    sat = _t5_saturation_distance()
    delta_lo = -unit * ((sat + tk - 1) // unit + 1)
    delta_hi = unit * ((sat + tq - 1) // unit + 1)
    width = delta_hi - delta_lo + tk
    rel = jnp.arange(tq - 1 + width) - (tq - 1) + delta_lo
    u = (t5_cols.astype(F32) * LOG2E)[_t5_bucket(rel)].T
    strip = _toeplitz(u, tq, width)
    nh = strip.shape[0]
    return strip.reshape(nh, tq, width // LANES, LANES).transpose(0, 2, 1, 3), delta_lo, delta_hi


def _diff_attention(hd, strips, consts, delta_lo, delta_hi, tq, tk, track_max):
    batch, seq = hd.shape[:2]
    w = DIFF_HEADS * LANES
    once = pl.Buffered(1)
    state = pltpu.VMEM((2 * DIFF_HEADS, tq, LANES), F32)
    return pl.pallas_call(
        functools.partial(_diff_kernel, tk=tk, nk=seq // tk, delta_lo=delta_lo, delta_hi=delta_hi,
                          track_max=track_max),
        grid=(batch, seq // tq),
        in_specs=[pl.BlockSpec((1, tq, w), lambda b, i: (b, i, 0)),
                  pl.BlockSpec((1, seq, w), lambda b, i: (b, 0, 1), pipeline_mode=once),
                  pl.BlockSpec((1, seq, w), lambda b, i: (b, 0, 2), pipeline_mode=once),
                  pl.BlockSpec(strips.shape, lambda b, i: (0, 0, 0, 0), pipeline_mode=once),
                  pl.BlockSpec(consts.shape, lambda b, i: (0, 0))],
        out_specs=pl.BlockSpec((1, tq, w), lambda b, i: (b, i, 0)),
        out_shape=jax.ShapeDtypeStruct((batch, seq, w), BF16),
        scratch_shapes=[pltpu.VMEM((2 * DIFF_HEADS, tq, LANES), BF16), state, state, state],
        compiler_params=_cparams(("arbitrary", "arbitrary"), 56),
        name="diff_attention" if track_max else "diff_attention_bounded",
    )(hd, hd, hd, strips, consts)


def _diff_logit_bound(hd, t5_cols):
    n = DIFF_HEADS * 2
    sq = lambda t: jnp.max(jnp.sum(jnp.square(t.astype(F32)).reshape(-1, n, HEAD_DIM), axis=-1), axis=0)
    qn2 = sq(hd[..., :n * HEAD_DIM])
    kn2 = sq(hd[..., n * HEAD_DIM:2 * n * HEAD_DIM])
    return 1.05 * (jnp.sqrt(jnp.max(qn2 * kn2)) + LOG2E * jnp.max(jnp.abs(t5_cols)))


def _diff_mixer(hd, t5_cols, consts):
    def run(tk, track_max):
        def branch(hd_, t5_, consts_):
            strips, delta_lo, delta_hi = _diff_bias_strips(t5_, DIFF_BIAS_ROWS, tk)
            return _diff_attention(hd_, strips, consts_, delta_lo, delta_hi, ATTN_TQ, tk, track_max)
        return branch

    return lax.cond(_diff_logit_bound(hd, t5_cols) <= ATTN_LOGIT_LIMIT,
                    run(DIFF_TK_BOUNDED, False), run(ATTN_TK, True), hd, t5_cols, consts)


def _layer_norm(z, g, b):
    mu = jnp.mean(z, axis=-1, keepdims=True)
    zc = z - mu
    var = jnp.mean(zc * zc, axis=-1, keepdims=True)
    return zc * lax.rsqrt(var + 1e-5) * g + b


def _split3(a):
    hi = a.astype(BF16)
    r1 = a - hi.astype(F32)
    mid = r1.astype(BF16)
    lo = (r1 - mid.astype(F32)).astype(BF16)
    return hi, mid, lo


def _merge_kernel(x_ref, xb_ref, ya_ref, yb_ref, o0_ref, o1_ref, o2_ref, l0_ref, l1_ref, l2_ref, yd_ref,
                  wg_ref, wna_ref, wgqa_ref, wdil_ref, wdiff_ref, wo_ref, wr_ref, ln_ref,
                  x1_ref, x1b_ref, lg_ref, *, alpha, d_model):
    xb = xb_ref[...]

    def gate(branch):
        g = jnp.dot(xb, wg_ref[:, branch * d_model:(branch + 1) * d_model], preferred_element_type=F32)
        return 1.0 / (1.0 + jnp.exp(-g))

    l0, l1, l2 = l0_ref[...], l1_ref[...], l2_ref[...]
    mx = jnp.maximum(jnp.maximum(l0, l1), l2)
    e0, e1, e2 = jnp.exp(l0 - mx), jnp.exp(l1 - mx), jnp.exp(l2 - mx)
    den = e0 + e1 + e2
    yc = (e0 / den) * o0_ref[...] + (e1 / den) * o1_ref[...] + (e2 / den) * o2_ref[...]
    merged = gate(0) * jnp.dot(ya_ref[...], wna_ref[...], preferred_element_type=F32)
    merged += gate(1) * jnp.dot(yb_ref[...], wgqa_ref[...], preferred_element_type=F32)
    merged += gate(2) * jnp.dot(yc.astype(BF16), wdil_ref[...], preferred_element_type=F32)
    merged += gate(3) * jnp.dot(yd_ref[...], wdiff_ref[...], preferred_element_type=F32)
    y = jnp.dot(merged.astype(BF16), wo_ref[...], preferred_element_type=F32)
    x1 = _layer_norm(alpha * x_ref[...] + y, ln_ref[0:1, :], ln_ref[1:2, :])
    x1_ref[...] = x1
    x1b_ref[...] = x1.astype(BF16)
    xh, xm, _ = _split3(x1)
    wh, wm, _ = wr_ref[0], wr_ref[1], wr_ref[2]
    lg_ref[...] = _dot_nt(wh, xh) + (_dot_nt(wh, xm) + _dot_nt(wm, xh))


def _merge(x, xb, ya, yb, dil, yd, w_g, w_na, w_gqa, w_dil, w_diff, w_o, w_r3, ln, alpha, tm):
    n, d = x.shape
    e = w_r3.shape[1]
    (o0, l0), (o1, l1), (o2, l2) = dil
    row = lambda w: pl.BlockSpec((tm, w), lambda i: (i, 0))
    full = lambda a: pl.BlockSpec(a.shape, lambda i: (0,) * a.ndim)
    return pl.pallas_call(
        functools.partial(_merge_kernel, alpha=alpha, d_model=d),
        grid=(n // tm,),
        in_specs=[row(d), row(d), row(ya.shape[1]), row(yb.shape[1]), row(LANES), row(LANES), row(LANES),
                  row(LANES), row(LANES), row(LANES), row(yd.shape[1]),
                  full(w_g), full(w_na), full(w_gqa), full(w_dil), full(w_diff), full(w_o), full(w_r3), full(ln)],
        out_specs=[row(d), row(d), pl.BlockSpec((e, tm), lambda i: (0, i))],
        out_shape=[jax.ShapeDtypeStruct((n, d), F32), jax.ShapeDtypeStruct((n, d), BF16),
                   jax.ShapeDtypeStruct((e, n), F32)],
        compiler_params=_cparams(("arbitrary",), 56),
        name="branch_merge",
    )(x, xb, ya, yb, o0, o1, o2, l0, l1, l2, yd, w_g, w_na, w_gqa, w_dil, w_diff, w_o, w_r3, ln)


def _route_kernel(lg_ref, aff_ref, posm_ref, posx_ref, *, cap):
    lg = lg_ref[0]
    n_e, seq = lg.shape
    mx = jnp.max(lg, axis=0, keepdims=True)
    ex = jnp.exp(lg - mx)
    aff = ex / jnp.sum(ex, axis=0, keepdims=True)
    aff_ref[0] = aff
    bits = pltpu.bitcast(aff, jnp.int32)

    def bisect(_, carry):
        lo, hi = carry
        mid = lo + ((hi - lo + 1) >> 1)
        cnt = jnp.sum((bits >= mid).astype(F32), axis=1, keepdims=True)
        ok = cnt >= float(cap)
        return jnp.where(ok, mid, lo), jnp.where(ok, hi, mid - 1)

    lo0 = jnp.zeros((n_e, 1), jnp.int32)
    hi0 = jnp.full((n_e, 1), 0x7F800000, jnp.int32)
    thr, _ = lax.fori_loop(0, 31, bisect, (lo0, hi0))
    gt = bits > thr
    eq = bits == thr
    needf = float(cap) - jnp.sum(gt.astype(F32), axis=1, keepdims=True)

    tri = (lax.broadcasted_iota(jnp.int32, (LANES, LANES), 0)
           < lax.broadcasted_iota(jnp.int32, (LANES, LANES), 1)).astype(BF16)

    def prefix(mask_bf16, j, carry):
        blk = mask_bf16[:, j * LANES:(j + 1) * LANES]
        excl = jnp.dot(blk, tri, preferred_element_type=F32) + carry
        return excl, carry + jnp.sum(blk.astype(F32), axis=1, keepdims=True)

    eqb = eq.astype(BF16)
    carry_eq = jnp.zeros((n_e, 1), F32)
    carry_sel = jnp.zeros((n_e, 1), F32)
    for j in range(seq // LANES):
        cols = slice(j * LANES, (j + 1) * LANES)
        rank, carry_eq = prefix(eqb, j, carry_eq)
        sel = jnp.logical_or(gt[:, cols], jnp.logical_and(eq[:, cols], rank < needf))
        selb = sel.astype(BF16)
        excl = jnp.dot(selb, tri, preferred_element_type=F32) + carry_sel
        carry_sel = carry_sel + jnp.sum(selb.astype(F32), axis=1, keepdims=True)
        pos = excl.astype(jnp.int32)
        posx_ref[0, :, cols] = pos
        posm_ref[0, :, cols] = jnp.where(sel, pos, -1)


def _route(logits_t, batch, seq, cap):
    n_e = logits_t.shape[0]
    lg = logits_t.reshape(n_e, batch, seq).transpose(1, 0, 2)
    spec = pl.BlockSpec((1, n_e, seq), lambda b: (b, 0, 0))
    return pl.pallas_call(
        functools.partial(_route_kernel, cap=cap),
        grid=(batch,),
        in_specs=[spec],
        out_specs=[spec, spec, spec],
        out_shape=[jax.ShapeDtypeStruct((batch, n_e, seq), F32), jax.ShapeDtypeStruct((batch, n_e, seq), jnp.int32),
                   jax.ShapeDtypeStruct((batch, n_e, seq), jnp.int32)],
        compiler_params=_cparams(("arbitrary",), 32),
        name="ec_route",
    )(lg)


def _gather_kernel(clo_ref, chi_ref, x_ref, pos_ref, aff_ref, xin_ref, gs_ref, acc_ref, *, n_e, nsb):
    base = (pl.program_id(0) * n_e + pl.program_id(1)) * nsb
    for sb in range(nsb):
        slot_ids = sb * SLOT_BLOCK + lax.broadcasted_iota(jnp.int32, (SLOT_BLOCK, TOKEN_CHUNK), 0)
        acc_ref[...] = jnp.zeros_like(acc_ref)

        def body(c, g, slot_ids=slot_ids):
            hit = pos_ref[0, 0, pl.ds(c, 1), :] == slot_ids
            xc = x_ref[0, pl.ds(pl.multiple_of(c * TOKEN_CHUNK, TOKEN_CHUNK), TOKEN_CHUNK), :]
            acc_ref[...] += jnp.dot(hit.astype(BF16), xc, preferred_element_type=F32)
            return g + jnp.sum(jnp.where(hit, aff_ref[0, 0, pl.ds(c, 1), :], 0.0), axis=-1, keepdims=True)

        g = lax.fori_loop(clo_ref[base + sb], chi_ref[base + sb], body, jnp.zeros((SLOT_BLOCK, 1), F32))
        rows = slice(sb * SLOT_BLOCK, (sb + 1) * SLOT_BLOCK)
        xin_ref[0, 0, rows, :] = acc_ref[...].astype(BF16)
        gs_ref[0, 0, rows, :] = jnp.broadcast_to(g, (SLOT_BLOCK, LANES))


def _gather(x1b, posm, aff, clo, chi, cap):
    batch, seq, d = x1b.shape
    n_e = posm.shape[1]
    nsb = cap // SLOT_BLOCK
    nch = seq // TOKEN_CHUNK
    pos4 = posm.reshape(batch, n_e, nch, TOKEN_CHUNK)
    aff4 = aff.reshape(batch, n_e, nch, TOKEN_CHUNK)
    grid_spec = pltpu.PrefetchScalarGridSpec(
        num_scalar_prefetch=2,
        grid=(batch, n_e),
        in_specs=[pl.BlockSpec((1, seq, d), lambda b, e, *_: (b, 0, 0)),
                  pl.BlockSpec((1, 1, nch, TOKEN_CHUNK), lambda b, e, *_: (b, e, 0, 0)),
                  pl.BlockSpec((1, 1, nch, TOKEN_CHUNK), lambda b, e, *_: (b, e, 0, 0))],
        out_specs=[pl.BlockSpec((1, 1, cap, d), lambda b, e, *_: (b, e, 0, 0)),
                   pl.BlockSpec((1, 1, cap, LANES), lambda b, e, *_: (b, e, 0, 0))],
        scratch_shapes=[pltpu.VMEM((SLOT_BLOCK, d), F32)],
    )
    return pl.pallas_call(
        functools.partial(_gather_kernel, n_e=n_e, nsb=nsb),
        grid_spec=grid_spec,
        out_shape=[jax.ShapeDtypeStruct((batch, n_e, cap, d), BF16),
                   jax.ShapeDtypeStruct((batch, n_e, cap, LANES), F32)],
        compiler_params=_cparams(("arbitrary", "arbitrary"), 56),
        name="ec_gather",
    )(clo, chi, x1b, pos4, aff4)


def _ffn_kernel(x_ref, gs_ref, wg_ref, wu_ref, wd_ref, o_ref, *, fchunk):
    x = x_ref[0, 0]
    ff = wg_ref.shape[2]
    acc = jnp.zeros((x.shape[0], wd_ref.shape[2]), F32)
    for f0 in range(0, ff, fchunk):
        g = jnp.dot(x, wg_ref[0, :, f0:f0 + fchunk], preferred_element_type=F32)
        u = jnp.dot(x, wu_ref[0, :, f0:f0 + fchunk], preferred_element_type=F32)
        hid = (g / (1.0 + jnp.exp(-g))) * u
        acc += jnp.dot(hid.astype(BF16), wd_ref[0, f0:f0 + fchunk, :], preferred_element_type=F32)
    o_ref[0, 0] = (acc * gs_ref[0, 0][:, 0:1]).astype(o_ref.dtype)


def _expert_ffn(xin, gslot, wg, wu, wd, tc, fchunk):
    batch, n_e, cap, d = xin.shape
    ff = wg.shape[2]
    return pl.pallas_call(
        functools.partial(_ffn_kernel, fchunk=fchunk),
        grid=(n_e, batch, cap // tc),
        in_specs=[pl.BlockSpec((1, 1, tc, d), lambda e, b, i: (b, e, i, 0)),
                  pl.BlockSpec((1, 1, tc, LANES), lambda e, b, i: (b, e, i, 0)),
                  pl.BlockSpec((1, d, ff), lambda e, b, i: (e, 0, 0)),
                  pl.BlockSpec((1, d, ff), lambda e, b, i: (e, 0, 0)),
                  pl.BlockSpec((1, ff, d), lambda e, b, i: (e, 0, 0))],
        out_specs=pl.BlockSpec((1, 1, tc, d), lambda e, b, i: (b, e, i, 0)),
        out_shape=jax.ShapeDtypeStruct((batch, n_e, cap, d), BF16),
        compiler_params=_cparams(("arbitrary", "arbitrary", "arbitrary"), 56),
        name="expert_ffn",
    )(xin, gslot, wg, wu, wd)


def _combine_kernel(cs_ref, yo_ref, post_ref, y_ref, *, n_e, nch, cap):
    b = pl.program_id(0)
    c = pl.program_id(2)
    win = 2 * SLOT_BLOCK
    pos_all = post_ref[0]
    win_ids = lax.broadcasted_iota(jnp.int32, (TOKEN_CHUNK, win), 1)
    hits, wins, tails = [], [], []
    for e in range(n_e):
        base = (b * n_e + e) * (nch + 1) + c
        s_lo = cs_ref[base]
        s_hi = cs_ref[base + 1]
        w0 = pl.multiple_of(jnp.minimum((s_lo >> SLOT_SHIFT) << SLOT_SHIFT, cap - win), SLOT_BLOCK)
        hits.append((pos_all[:, e:e + 1] == (w0 + win_ids)).astype(BF16))
        wins.append(yo_ref[0, e, pl.ds(w0, win), :])
        sb_end = jnp.where(s_hi > s_lo, ((s_hi - 1) >> SLOT_SHIFT) + 1, 0)
        tails.append(((w0 + win) >> SLOT_SHIFT, sb_end))
    y_ref[0] = jnp.dot(jnp.concatenate(hits, axis=1), jnp.concatenate(wins, axis=0), preferred_element_type=F32)
    for e, (sb_from, sb_end) in enumerate(tails):
        def body(sb, carry, e=e):
            blk_ids = lax.broadcasted_iota(jnp.int32, (TOKEN_CHUNK, SLOT_BLOCK), 1)
            hit = post_ref[0, :, e:e + 1] == (sb * SLOT_BLOCK + blk_ids)
            yb = yo_ref[0, e, pl.ds(pl.multiple_of(sb * SLOT_BLOCK, SLOT_BLOCK), SLOT_BLOCK), :]
            y_ref[0] += jnp.dot(hit.astype(BF16), yb, preferred_element_type=F32)
            return carry

        lax.fori_loop(sb_from, sb_end, body, 0)


def _combine(yo, pos_t, cs, td):
    batch, n_e, cap, d = yo.shape
    seq = pos_t.shape[1]
    nch = seq // TOKEN_CHUNK
    grid_spec = pltpu.PrefetchScalarGridSpec(
        num_scalar_prefetch=1,
        grid=(batch, d // td, nch),
        in_specs=[pl.BlockSpec((1, n_e, cap, td), lambda b, j, c, *_: (b, 0, 0, j)),
                  pl.BlockSpec((1, TOKEN_CHUNK, n_e), lambda b, j, c, *_: (b, c, 0))],
        out_specs=pl.BlockSpec((1, TOKEN_CHUNK, td), lambda b, j, c, *_: (b, c, j)),
    )
    return pl.pallas_call(
        functools.partial(_combine_kernel, n_e=n_e, nch=nch, cap=cap),
        grid_spec=grid_spec,
        out_shape=jax.ShapeDtypeStruct((batch, seq, d), F32),
        compiler_params=_cparams(("arbitrary", "arbitrary", "arbitrary"), 56),
        name="ec_combine",
    )(cs, yo, pos_t)


def _ple_kernel(x1_ref, y_ref, p_ref, wpg_ref, wpp_ref, ln_ref, x3_ref, x3b_ref, *, alpha):
    x2 = _layer_norm(alpha * x1_ref[...] + y_ref[...], ln_ref[0:1, :], ln_ref[1:2, :])
    gate = 1.0 / (1.0 + jnp.exp(-jnp.dot(x2.astype(BF16), wpg_ref[...], preferred_element_type=F32)))
    emb = jnp.dot(p_ref[...].astype(BF16), wpp_ref[...], preferred_element_type=F32) * gate
    x3 = _layer_norm(alpha * x2 + emb, ln_ref[2:3, :], ln_ref[3:4, :])
    x3_ref[...] = x3
    x3b_ref[...] = x3.astype(BF16)


def _ple(x1, y, p, w_pg, w_pp, ln, alpha, tm):
    n, d = x1.shape
    row = lambda w: pl.BlockSpec((tm, w), lambda i: (i, 0))
    full = lambda a: pl.BlockSpec(a.shape, lambda i: (0,) * a.ndim)
    return pl.pallas_call(
        functools.partial(_ple_kernel, alpha=alpha),
        grid=(n // tm,),
        in_specs=[row(d), row(d), row(p.shape[1]), full(w_pg), full(w_pp), full(ln)],
        out_specs=[row(d), row(d)],
        out_shape=[jax.ShapeDtypeStruct((n, d), F32), jax.ShapeDtypeStruct((n, d), BF16)],
        compiler_params=_cparams(("arbitrary",), 40),
        name="ple_norm",
    )(x1, y, p, w_pg, w_pp, ln)


def _token_mixer_inputs(xb, w_in, qk_gain, batch, seq):
    d = w_in.shape[0]
    scale = HEAD_DIM ** -0.5
    na_w = NA_HEADS * HEAD_DIM
    gq_w, gkv_w = GQA_HEADS * HEAD_DIM, GQA_KV_HEADS * HEAD_DIM
    dil_w = DIL_HEADS * HEAD_DIM
    dqk_w = DIFF_HEADS * 2 * HEAD_DIM
    o = 0
    w_a = w_in[:, o:o + 3 * na_w]; o += 3 * na_w
    w_b = w_in[:, o:o + gq_w + 2 * gkv_w]; o += gq_w + 2 * gkv_w
    w_c = w_in[:, o:o + 3 * dil_w]; o += 3 * dil_w
    w_d = w_in[:, o:o + 3 * dqk_w]; o += 3 * dqk_w
    w_g = w_in[:, o:]

    def qscaled(w, qw):
        return jnp.concatenate([w[:, :qw] * scale, w[:, qw:]], axis=1).astype(BF16)

    n = batch * seq
    ha = _matmul(xb, qscaled(w_a, na_w), BF16, 1024, 3 * na_w, "proj_na").reshape(batch, seq, -1)
    w_cg = jnp.concatenate(
        [w_c[:, t * dil_w + LANES * g:t * dil_w + LANES * (g + 1)] * (scale if t == 0 else 1.0)
         for g in range(len(DIL_GROUPS)) for t in range(3)], axis=1).astype(BF16)
    hc = _matmul(xb, w_cg, BF16, 1024, 3 * dil_w, "proj_dil").reshape(batch, seq, -1)
    w_dq = jnp.concatenate([w_d[:, :dqk_w] * (scale * LOG2E), w_d[:, dqk_w:]], axis=1).astype(BF16)
    hd = _matmul(xb, w_dq, BF16, 1024, 3 * dqk_w, "proj_diff").reshape(batch, seq, -1)
    qn, kd, vd = _proj_gqa(xb, w_b, qk_gain, seq, 512)
    shape3 = lambda t: t.reshape(batch, seq, t.shape[-1])
    return ha, shape3(qn), shape3(kd), shape3(vd), hc, hd, w_g.astype(BF16)


def _layer(x, xb, p, w, lam_init, alpha):
    batch, seq, d = x.shape
    n = batch * seq
    ha, qn, kd, vd, hc, hd, w_g = _token_mixer_inputs(xb.reshape(n, d), w["w_in"], w["qk_gain"], batch, seq)

    ya = _na_attention(ha, _na_bias_tiles(w["rpb"]), batch, seq)
    yb = lax.cond(_gqa_logit_bound(w["qk_gain"]) <= ATTN_LOGIT_LIMIT,
                  lambda *a: _gqa_attention(*a, ATTN_TQ, GQA_TK_BOUNDED, False),
                  lambda *a: _gqa_attention(*a, ATTN_TQ, ATTN_TK, True), qn, kd, vd)
    dil = []
    for g, (_, dilation) in enumerate(DIL_GROUPS):
        bias = _dil_bias_tiles(w["t5"][:, 2 * g:2 * g + 2], dilation, 128)
        dil.append(_dilated_group(hc, bias, g, dilation, batch, seq, 128, DIL_BLOCKS_PER_STEP))
    lp = w["lam_params"].astype(F32)
    lam = jnp.exp(jnp.sum(lp[0] * lp[1])) - jnp.exp(jnp.sum(lp[2] * lp[3])) + lam_init
    consts = jnp.zeros((8, LANES), F32).at[0].set(lam).at[1].set(w["diff_gain"]).at[2].set(1.0 - lam_init)
    yd = _diff_mixer(hd, w["t5"][:, DIL_HEADS:], consts)

    flat = lambda t: t.reshape(n, t.shape[-1])
    w_r3 = jnp.stack(_split3(w["w_router"].T))
    ln = w["ln"]
    x1, x1b, logits_t = _merge(
        x.reshape(n, d), xb.reshape(n, d), flat(ya), flat(yb), [(flat(o), flat(l)) for o, l in dil], flat(yd), w_g,
        w["w_na"].astype(BF16), w["w_gqa"].astype(BF16), w["w_dil"].astype(BF16), w["w_diff"].astype(BF16),
        w["w_o"].astype(BF16), w_r3, jnp.stack([ln[0][0], ln[1][0]]), alpha, 512)

    cap = EC_CAPACITY * seq // N_EXPERTS
    aff, posm, posx = _route(logits_t, batch, seq, cap)
    nch = seq // TOKEN_CHUNK
    cs = jnp.concatenate([posx[:, :, ::TOKEN_CHUNK], jnp.full((batch, N_EXPERTS, 1), cap, jnp.int32)], axis=-1)
    edges = jnp.arange(cap // SLOT_BLOCK, dtype=jnp.int32) * SLOT_BLOCK
    clo = jnp.sum(cs[:, :, 1:, None] <= edges, axis=2).astype(jnp.int32)
    chi = jnp.sum(cs[:, :, :nch, None] < edges + SLOT_BLOCK, axis=2).astype(jnp.int32)
    xin, gslot = _gather(x1b.reshape(batch, seq, d), posm, aff, clo.reshape(-1), chi.reshape(-1), cap)
    yo = _expert_ffn(xin, gslot, w["w_eg"].astype(BF16), w["w_eu"].astype(BF16), w["w_ed"].astype(BF16), 512, 512)
    y = _combine(yo, posm.transpose(0, 2, 1), cs.reshape(-1), 512)

    ln4 = jnp.stack([ln[0][1], ln[1][1], ln[0][2], ln[1][2]])
    x3, x3b = _ple(x1, y.reshape(n, d), p.reshape(n, -1), w["w_pg"].astype(BF16), w["w_pp"].astype(BF16), ln4, alpha, 512)
    return x3.reshape(batch, seq, d), x3b.reshape(batch, seq, d)


def kernel(x, p, w_in, w_branch_na, w_branch_gqa, w_branch_dil, w_branch_diff, w_out, na_rel_bias, qk_norm_gain,
           diff_lambda, diff_norm_gain, t5_rel_bias, w_router, w_expert_gate, w_expert_up, w_expert_down,
           w_ple_proj, w_ple_gate, ln_gain, ln_bias):
    depth = w_in.shape[0]
    alpha = (2 * depth) ** 0.25
    xb = x.astype(BF16)
    for i in range(depth):
        lam_init = 0.8 - 0.6 * math.exp(-0.3 * i)
        w = dict(w_in=w_in[i], w_na=w_branch_na[i], w_gqa=w_branch_gqa[i], w_dil=w_branch_dil[i],
                 w_diff=w_branch_diff[i], w_o=w_out[i], rpb=na_rel_bias[i], qk_gain=qk_norm_gain[i],
                 lam_params=diff_lambda[i], diff_gain=diff_norm_gain[i], t5=t5_rel_bias, w_router=w_router[i],
                 w_eg=w_expert_gate[i], w_eu=w_expert_up[i], w_ed=w_expert_down[i], w_pp=w_ple_proj[i],
                 w_pg=w_ple_gate[i], ln=(ln_gain[i], ln_bias[i]))
        x, xb = _layer(x, xb, p[i], w, lam_init, alpha)
    return x
```

```python
import functools
import math

import numpy as np
import jax
import jax.numpy as jnp
from jax import lax
from jax.experimental import pallas as pl
from jax.experimental.pallas import tpu as pltpu

F32 = jnp.float32
BF16 = jnp.bfloat16
NEG_INF = -1e30

V7X_VMEM_BYTES = 64 * 1024 * 1024
LANES = 128
LANE_SHIFT = 7

GRID_W = 64
HEAD_DIM = 64
NA_HEADS, NA_ROWS, NA_COLS = 6, 8, 16
NA_ROWS_PER_STEP = 4
LOG2E = 1.4426950408889634
ATTN_TQ, ATTN_TK = 1024, 256
GQA_TK_BOUNDED, DIFF_TK_BOUNDED = 1024, 512
ATTN_LOGIT_LIMIT = 64.0
DIL_TQ = 128
DIL_BLOCKS_PER_STEP = 4
DIFF_BIAS_ROWS = 256
GQA_HEADS, GQA_KV_HEADS = 6, 2
ROPE_THETA = 10000.0
DIL_GROUPS = ((128, 1), (512, 4), (2048, 16))
DIL_HEADS = 6
DIFF_HEADS = 4
T5_BUCKETS, T5_MAX_DISTANCE = 32, 1024
N_EXPERTS, EC_CAPACITY = 16, 2
SLOT_BLOCK = 128
SLOT_SHIFT = SLOT_BLOCK.bit_length() - 1
TOKEN_CHUNK = 256


def _cparams(semantics, vmem_mib):
    return pltpu.CompilerParams(dimension_semantics=semantics,
                                vmem_limit_bytes=min(vmem_mib * 1024 * 1024, V7X_VMEM_BYTES - 4 * 1024 * 1024))


def _mm_kernel(a_ref, b_ref, o_ref):
    o_ref[...] = jnp.dot(a_ref[...], b_ref[...], preferred_element_type=F32).astype(o_ref.dtype)


def _matmul(a, b, out_dtype, tm, tn, name):
    m, k = a.shape
    n = b.shape[1]
    return pl.pallas_call(
        _mm_kernel,
        grid=(n // tn, m // tm),
        in_specs=[pl.BlockSpec((tm, k), lambda j, i: (i, 0)),
                  pl.BlockSpec((k, tn), lambda j, i: (0, j))],
        out_specs=pl.BlockSpec((tm, tn), lambda j, i: (i, j)),
        out_shape=jax.ShapeDtypeStruct((m, n), out_dtype),
        compiler_params=_cparams(("arbitrary", "arbitrary"), 40),
        name=name,
    )(a, b)


def _toeplitz(u, rows, width):
    lead = u.shape[:-1]
    period = rows + width
    text = jnp.concatenate([u[..., rows - 1:], jnp.zeros(lead + (1,), u.dtype), u[..., :rows - 1]], axis=-1)
    flat = jnp.tile(text, (1,) * len(lead) + (rows,))[..., :rows * (period - 1)]
    return flat.reshape(lead + (rows, period - 1))[..., :width]


def _lane_lo(rows):
    return lax.broadcasted_iota(jnp.int32, (rows, LANES), 1) < HEAD_DIM


def _dot_nt(a, b):
    return lax.dot_general(a, b, (((1,), (1,)), ((), ())), preferred_element_type=F32)


NA_WIN_ROWS = NA_ROWS + NA_ROWS_PER_STEP


def _na_kernel(q_ref, k_ref, v_ref, tp_ref, o_ref, *, rows):
    r0 = pl.program_id(1) * NA_ROWS_PER_STEP
    ws = jnp.clip(r0 - NA_ROWS // 2, 0, rows - NA_WIN_ROWS)
    start = pl.multiple_of(ws * GRID_W, GRID_W)
    win = NA_WIN_ROWS * GRID_W
    npair = NA_WIN_ROWS // 2
    none_code = 2 * (2 * NA_ROWS - 1)
    codes = []
    for rr in range(NA_ROWS_PER_STEP):
        r = r0 + rr
        rs = jnp.clip(r - NA_ROWS // 2, 0, rows - NA_ROWS)
        row_codes = []
        for j in range(npair):
            a0 = ws + 2 * j
            d0 = a0 - r + (NA_ROWS - 1)
            vl = jnp.logical_and(a0 >= rs, a0 < rs + NA_ROWS)
            vr = jnp.logical_and(a0 + 1 >= rs, a0 + 1 < rs + NA_ROWS)
            row_codes.append(jnp.where(jnp.logical_and(vl, vr), d0,
                                       jnp.where(jnp.logical_or(vl, vr), 2 * NA_ROWS - 1 + d0, none_code)))
        codes.append(row_codes)
    tq = NA_ROWS_PER_STEP * GRID_W
    lo = _lane_lo(tq)
    for p in range(NA_HEADS // 2):
        cols = slice(LANES * p, LANES * (p + 1))
        qp = q_ref[0, :, cols]
        kw = k_ref[0, pl.ds(start, win), cols]
        vw = v_ref[0, pl.ds(start, win), cols]
        outs = []
        for hh in range(2):
            h = 2 * p + hh
            qm = jnp.where(lo if hh == 0 else jnp.logical_not(lo), qp, jnp.zeros_like(qp))
            bias = jnp.concatenate(
                [jnp.concatenate([tp_ref[h, c] for c in row_codes], axis=-1) for row_codes in codes], axis=0)
            s = _dot_nt(qm, kw) + bias
            e = jnp.exp2(s - jnp.max(s, axis=-1, keepdims=True))
            pv = jnp.dot(e.astype(BF16), vw, preferred_element_type=F32)
            outs.append(pv / jnp.sum(e, axis=-1, keepdims=True))
        o_ref[0, :, cols] = jnp.where(lo, outs[0], outs[1]).astype(o_ref.dtype)


def _na_bias_tiles(rpb):
    col = jnp.arange(GRID_W)
    cstart = jnp.clip(col - NA_COLS // 2, 0, GRID_W - NA_COLS)
    kc = jnp.arange(GRID_W)
    inwin = (kc[None, :] >= cstart[:, None]) & (kc[None, :] < cstart[:, None] + NA_COLS)
    pad = GRID_W - NA_COLS
    u = jnp.pad(rpb.astype(F32) * LOG2E, ((0, 0), (0, 0), (pad, pad)))
    t = jnp.where(inwin[None, None], _toeplitz(u, GRID_W, GRID_W), NEG_INF)
    neg = jnp.full_like(t[:, :NA_ROWS], NEG_INF)
    both = jnp.concatenate([t[:, :-1], t[:, 1:]], axis=-1)
    right = jnp.concatenate([neg, t[:, :NA_ROWS]], axis=-1)
    left = jnp.concatenate([t[:, NA_ROWS - 1:], neg], axis=-1)
    none = jnp.concatenate([neg[:, :1], neg[:, :1]], axis=-1)
    return jnp.concatenate([both, right, left, none], axis=1)


def _na_attention(ha, t2, batch, seq):
    rows = seq // GRID_W
    w = NA_HEADS * HEAD_DIM
    tq = GRID_W * NA_ROWS_PER_STEP
    return pl.pallas_call(
        functools.partial(_na_kernel, rows=rows),
        grid=(batch, rows // NA_ROWS_PER_STEP),
        in_specs=[pl.BlockSpec((1, tq, w), lambda b, r: (b, r, 0)),
                  pl.BlockSpec((1, seq, w), lambda b, r: (b, 0, 1)),
                  pl.BlockSpec((1, seq, w), lambda b, r: (b, 0, 2)),
                  pl.BlockSpec(t2.shape, lambda b, r: (0, 0, 0, 0))],
        out_specs=pl.BlockSpec((1, tq, w), lambda b, r: (b, r, 0)),
        out_shape=jax.ShapeDtypeStruct((batch, seq, w), BF16),
        compiler_params=_cparams(("arbitrary", "arbitrary"), 56),
        name="na_attention",
    )(ha, ha, ha, t2)


def _softmax_stream_update(s, vb, m_ref, l_ref, acc_ref, idx):
    m_prev = m_ref[idx]
    m_new = jnp.maximum(m_prev, jnp.max(s, axis=-1, keepdims=True))
    alpha = jnp.exp2(m_prev - m_new)
    p = jnp.exp2(s - jnp.tile(m_new, (1, s.shape[1] // LANES)))
    m_ref[idx] = m_new
    l_ref[idx] = alpha * l_ref[idx] + jnp.sum(p, axis=-1, keepdims=True)
    acc_ref[idx] = alpha * acc_ref[idx] + jnp.dot(p.astype(BF16), vb, preferred_element_type=F32)


def _init_streams(m_ref, l_ref, acc_ref):
    m_ref[...] = jnp.full(m_ref.shape, NEG_INF, F32)
    l_ref[...] = jnp.zeros(l_ref.shape, F32)
    acc_ref[...] = jnp.zeros(acc_ref.shape, F32)


def _gqa_kernel(q_ref, k_ref, v_ref, o_ref, qm_ref, m_ref, l_ref, acc_ref, *, tk, nk, track_max):
    tq = q_ref.shape[1]
    lo = _lane_lo(tq)
    rep = GQA_HEADS // GQA_KV_HEADS
    for h in range(GQA_HEADS):
        qp = q_ref[0, :, LANES * (h // 2):LANES * (h // 2 + 1)]
        qm_ref[h] = jnp.where(lo if h % 2 == 0 else jnp.logical_not(lo), qp, jnp.zeros_like(qp))
    _init_streams(m_ref, l_ref, acc_ref)

    def body(j, carry):
        ks = pl.multiple_of(j * tk, tk)
        for g in range(GQA_KV_HEADS):
            gcols = slice(LANES * g, LANES * (g + 1))
            kb = k_ref[0, pl.ds(ks, tk), gcols]
            vb = v_ref[0, pl.ds(ks, tk), gcols]
            for h in range(rep * g, rep * (g + 1)):
                s = _dot_nt(qm_ref[h], kb)
                if track_max:
                    _softmax_stream_update(s, vb, m_ref, l_ref, acc_ref, h)
                else:
                    p = jnp.exp2(s)
                    l_ref[h] += jnp.sum(p, axis=-1, keepdims=True)
                    acc_ref[h] += jnp.dot(p.astype(BF16), vb, preferred_element_type=F32)
        return carry

    lax.fori_loop(0, nk, body, 0)
    for p in range(GQA_HEADS // 2):
        o0 = acc_ref[2 * p] / l_ref[2 * p]
        o1 = acc_ref[2 * p + 1] / l_ref[2 * p + 1]
        o_ref[0, :, LANES * p:LANES * (p + 1)] = jnp.where(lo, o0, o1).astype(o_ref.dtype)


def _gqa_attention(q, kd, vd, tq, tk, track_max):
    batch, seq, wq = q.shape
    wk = kd.shape[-1]
    state = pltpu.VMEM((GQA_HEADS, tq, LANES), F32)
    return pl.pallas_call(
        functools.partial(_gqa_kernel, tk=tk, nk=seq // tk, track_max=track_max),
        grid=(batch, seq // tq),
        in_specs=[pl.BlockSpec((1, tq, wq), lambda b, i: (b, i, 0)),
                  pl.BlockSpec((1, seq, wk), lambda b, i: (b, 0, 0)),
                  pl.BlockSpec((1, seq, wk), lambda b, i: (b, 0, 0))],
        out_specs=pl.BlockSpec((1, tq, wq), lambda b, i: (b, i, 0)),
        out_shape=jax.ShapeDtypeStruct((batch, seq, wq), BF16),
        scratch_shapes=[pltpu.VMEM((GQA_HEADS, tq, LANES), BF16), state, state, state],
        compiler_params=_cparams(("arbitrary", "arbitrary"), 48),
        name="gqa_attention" if track_max else "gqa_attention_bounded",
    )(q, kd, vd)


def _gqa_logit_bound(qk_gain):
    gq = jnp.max(jnp.abs(qk_gain[0])) * (HEAD_DIM ** -0.5 * LOG2E)
    return 1.05 * HEAD_DIM * gq * jnp.max(jnp.abs(qk_gain[1]))


def _rope_tables(seq):
    quarter = HEAD_DIM // 4
    freqs = ROPE_THETA ** (-jnp.arange(quarter, dtype=F32) / quarter)
    t = jnp.arange(seq)
    row = (t // GRID_W).astype(F32)
    col = (t % GRID_W).astype(F32)
    ang_r = row[:, None] * freqs[None, :]
    ang_c = col[:, None] * freqs[None, :]
    cos = jnp.concatenate([jnp.cos(ang_r)] * 2 + [jnp.cos(ang_c)] * 2, axis=-1)
    sin = jnp.concatenate([-jnp.sin(ang_r), jnp.sin(ang_r), -jnp.sin(ang_c), jnp.sin(ang_c)], axis=-1)
    return cos, sin


def _proj_gqa_kernel(x_ref, w_ref, cos_ref, sin_ref, gain_ref, mean_ref, q_ref, k_ref, v_ref):
    h = jnp.dot(x_ref[...], w_ref[...], preferred_element_type=F32)
    nq = q_ref.shape[1] // LANES
    nk = k_ref.shape[1] // LANES
    lane = lax.broadcasted_iota(jnp.int32, (h.shape[0], LANES), 1)
    first = (lane & (HEAD_DIM // 2 - 1)) < HEAD_DIM // 4
    for blk in range(nq + nk):
        t = h[:, LANES * blk:LANES * (blk + 1)]
        hi, mid, low = _split3(t * t)
        ms = (jnp.dot(hi, mean_ref[...], preferred_element_type=F32)
              + jnp.dot(mid, mean_ref[...], preferred_element_type=F32)
              + jnp.dot(low, mean_ref[...], preferred_element_type=F32))
        y = t * lax.rsqrt(ms + 1e-6) * gain_ref[blk:blk + 1, :]
        partner = jnp.where(first, pltpu.roll(y, LANES - HEAD_DIM // 4, 1), pltpu.roll(y, HEAD_DIM // 4, 1))
        out = (y * cos_ref[...] + partner * sin_ref[...]).astype(BF16)
        if blk < nq:
            q_ref[:, LANES * blk:LANES * (blk + 1)] = out
        else:
            k_ref[:, LANES * (blk - nq):LANES * (blk - nq + 1)] = out
    v_ref[...] = h[:, LANES * (nq + nk):].astype(BF16)


def _proj_gqa(xb, w_b, qk_gain, seq, tm):
    n, d = xb.shape
    gq, gkv = GQA_HEADS * HEAD_DIM, GQA_KV_HEADS * HEAD_DIM
    dup = lambda w: jnp.concatenate([w[:, i * HEAD_DIM:(i + 1) * HEAD_DIM] for i in range(GQA_KV_HEADS) for _ in range(2)], axis=1)
    w_all = jnp.concatenate([w_b[:, :gq], dup(w_b[:, gq:gq + gkv]), dup(w_b[:, gq + gkv:])], axis=1).astype(BF16)
    cos, sin = _rope_tables(seq)
    cos2, sin2 = jnp.tile(cos, (1, 2)), jnp.tile(sin, (1, 2))
    nq, nk = gq // LANES, 2 * gkv // LANES
    gain = jnp.concatenate([jnp.tile(qk_gain[0] * (HEAD_DIM ** -0.5 * LOG2E), (nq, 2)), jnp.tile(qk_gain[1], (nk, 2)),
                            jnp.zeros((8 - nq - nk, LANES), F32)], axis=0)
    head = jnp.arange(LANES) // HEAD_DIM
    mean_mat = jnp.where(head[:, None] == head[None, :], 1.0 / HEAD_DIM, 0.0).astype(BF16)
    nt = seq // tm
    row = lambda w: pl.BlockSpec((tm, w), lambda i: (i, 0))
    full = lambda a: pl.BlockSpec(a.shape, lambda i: (0,) * a.ndim)
    pos = pl.BlockSpec((tm, LANES), lambda i: (i % nt, 0))
    return pl.pallas_call(
        _proj_gqa_kernel,
        grid=(n // tm,),
        in_specs=[row(d), full(w_all), pos, pos, full(gain), full(mean_mat)],
        out_specs=[row(gq), row(2 * gkv), row(2 * gkv)],
        out_shape=[jax.ShapeDtypeStruct((n, gq), BF16), jax.ShapeDtypeStruct((n, 2 * gkv), BF16),
                   jax.ShapeDtypeStruct((n, 2 * gkv), BF16)],
        compiler_params=_cparams(("arbitrary",), 40),
        name="proj_gqa",
    )(xb, w_all, cos2, sin2, gain, mean_mat)


def _dil_kernel(q_ref, k_ref, v_ref, bias_ref, o_ref, lse_ref, *, sub_len, nblk, tq):
    half = DIL_GROUPS[0][0] // 2
    win = tq + 2 * half
    lo = _lane_lo(tq)
    for qq in range(q_ref.shape[1] // tq):
        i = pl.program_id(2) * (q_ref.shape[1] // tq) + qq
        start = pl.multiple_of(jnp.clip(i * tq - half, 0, sub_len - win), half)
        var = jnp.where(i == 0, 0, jnp.where(i == nblk - 1, 2, 1))
        rows = slice(qq * tq, (qq + 1) * tq)
        qp = q_ref[0, rows, :]
        kw = k_ref[0, pl.ds(start, win), :]
        vw = v_ref[0, pl.ds(start, win), :]
        outs, lses = [], []
        for hh in range(2):
            qm = jnp.where(lo if hh == 0 else jnp.logical_not(lo), qp, jnp.zeros_like(qp))
            s = _dot_nt(qm, kw) + bias_ref[hh, var]
            m = jnp.max(s, axis=-1, keepdims=True)
            e = jnp.exp(s - m)
            l = jnp.sum(e, axis=-1, keepdims=True)
            outs.append(jnp.dot(e.astype(BF16), vw, preferred_element_type=F32) / l)
            lses.append(jnp.broadcast_to(m + jnp.log(l), (tq, LANES)))
        o_ref[0, rows, :] = jnp.where(lo, outs[0], outs[1])
        lse_ref[0, rows, :] = jnp.where(lo, lses[0], lses[1])


def _t5_bucket(rel):
    half = T5_BUCKETS // 2
    exact = half // 2
    n = jnp.abs(rel)
    nf = jnp.maximum(n, 1).astype(F32)
    large = exact + (jnp.log(nf / exact) / math.log(T5_MAX_DISTANCE / exact) * (half - exact)).astype(jnp.int32)
    large = jnp.minimum(large, half - 1)
    return jnp.where(rel > 0, half, 0) + jnp.where(n < exact, n, large)


def _dil_bias_tiles(t5_cols, dilation, tq):
    half = DIL_GROUPS[0][0] // 2
    win = tq + 2 * half
    wide = win + 2 * half
    j = jnp.arange(tq - 1 + wide) - (tq - 1) - 2 * half
    vals = t5_cols.astype(F32)[_t5_bucket(j * dilation)]
    u = jnp.where((jnp.abs(j) <= half)[:, None], vals, NEG_INF).T
    a = _toeplitz(u, tq, wide)
    return jnp.stack([a[:, :, 2 * half - s:2 * half - s + win] for s in (0, half, 2 * half)], axis=1)


def _dilated_group(hc, bias, g, dilation, batch, seq, tq, qb):
    sub_len = seq // dilation
    nblk = sub_len // tq
    ts = tq * qb
    if dilation == 1:
        view, nb, g = hc, hc.shape[-1] // LANES, 3 * g
    else:
        view, nb, g = hc[..., 3 * LANES * g:3 * LANES * (g + 1)].reshape(batch, sub_len, dilation * 3 * LANES), 3, 0
    nq = 1
    out, lse = pl.pallas_call(
        functools.partial(_dil_kernel, sub_len=sub_len, nblk=nblk, tq=tq),
        grid=(batch, dilation, nblk // qb),
        in_specs=[pl.BlockSpec((1, ts, LANES), lambda b, r, i: (b, i, r * nb + g)),
                  pl.BlockSpec((1, sub_len, LANES), lambda b, r, i: (b, 0, r * nb + nq + g)),
                  pl.BlockSpec((1, sub_len, LANES), lambda b, r, i: (b, 0, r * nb + 2 * nq + g)),
                  pl.BlockSpec(bias.shape, lambda b, r, i: (0, 0, 0, 0))],
        out_specs=[pl.BlockSpec((1, ts, LANES), lambda b, r, i: (b, i, r)),
                   pl.BlockSpec((1, ts, LANES), lambda b, r, i: (b, i, r))],
        out_shape=[jax.ShapeDtypeStruct((batch, sub_len, dilation * LANES), F32)] * 2,
        compiler_params=_cparams(("arbitrary", "arbitrary", "arbitrary"), 32),
        name=f"dilated_attention_d{dilation}",
    )(view, view, view, bias)
    return out.reshape(batch, seq, LANES), lse.reshape(batch, seq, LANES)


def _diff_kernel(q_ref, k_ref, v_ref, strip_ref, c_ref, o_ref, qm_ref, m_ref, l_ref, acc_ref, *,
                 tk, nk, delta_lo, delta_hi, track_max):
    tq = q_ref.shape[1]
    i = pl.program_id(1)
    lo = _lane_lo(tq)
    for h in range(DIFF_HEADS):
        qp = q_ref[0, :, LANES * h:LANES * (h + 1)]
        qm_ref[2 * h] = jnp.where(lo, qp, jnp.zeros_like(qp))
        qm_ref[2 * h + 1] = jnp.where(lo, jnp.zeros_like(qp), qp)
    _init_streams(m_ref, l_ref, acc_ref)

    def body(j, carry):
        ks = pl.multiple_of(j * tk, tk)
        sub = strip_ref.shape[2]
        blk0 = [(jnp.clip(j * tk - i * tq - a * sub, delta_lo, delta_hi) - delta_lo) >> LANE_SHIFT
                for a in range(tq // sub)]
        for h in range(DIFF_HEADS):
            cols = slice(LANES * h, LANES * (h + 1))
            kb = k_ref[0, pl.ds(ks, tk), cols]
            vb = v_ref[0, pl.ds(ks, tk), cols]
            bias = jnp.concatenate(
                [jnp.concatenate([strip_ref[h, b0 + c] for c in range(tk // LANES)], axis=-1) for b0 in blk0], axis=0)
            for t in range(2):
                s = _dot_nt(qm_ref[2 * h + t], kb) + bias
                if track_max:
                    _softmax_stream_update(s, vb, m_ref, l_ref, acc_ref, 2 * h + t)
                else:
                    p = jnp.exp2(s)
                    l_ref[2 * h + t] += jnp.sum(p, axis=-1, keepdims=True)
                    acc_ref[2 * h + t] += jnp.dot(p.astype(BF16), vb, preferred_element_type=F32)
        return carry

    lax.fori_loop(0, nk, body, 0)
    lam = c_ref[0:1, :]
    for h in range(DIFF_HEADS):
        out = acc_ref[2 * h] / l_ref[2 * h] - lam * (acc_ref[2 * h + 1] / l_ref[2 * h + 1])
        y = out * lax.rsqrt(jnp.mean(out * out, axis=-1, keepdims=True) + 1e-6)
        o_ref[0, :, LANES * h:LANES * (h + 1)] = ((y * c_ref[1:2, :]) * c_ref[2:3, :]).astype(o_ref.dtype)


def _t5_saturation_distance():
    half = T5_BUCKETS // 2
    exact = half // 2
    n = np.arange(1, 4 * T5_MAX_DISTANCE, dtype=np.float32)
    large = exact + (np.log(n / exact) / math.log(T5_MAX_DISTANCE / exact) * (half - exact)).astype(np.int32)
    return int(np.argmax(large >= half - 1)) + 1 + 8


def _diff_bias_strips(t5_cols, tq, tk):
    unit = math.gcd(tq, tk)
    sat = _t5_saturation_distance()
    delta_lo = -unit * ((sat + tk - 1) // unit + 1)
    delta_hi = unit * ((sat + tq - 1) // unit + 1)
    width = delta_hi - delta_lo + tk
    rel = jnp.arange(tq - 1 + width) - (tq - 1) + delta_lo
    u = (t5_cols.astype(F32) * LOG2E)[_t5_bucket(rel)].T
    strip = _toeplitz(u, tq, width)
    nh = strip.shape[0]
    return strip.reshape(nh, tq, width // LANES, LANES).transpose(0, 2, 1, 3), delta_lo, delta_hi


def _diff_attention(hd, strips, consts, delta_lo, delta_hi, tq, tk, track_max):
    batch, seq = hd.shape[:2]
    w = DIFF_HEADS * LANES
    once = pl.Buffered(1)
    state = pltpu.VMEM((2 * DIFF_HEADS, tq, LANES), F32)
    return pl.pallas_call(
        functools.partial(_diff_kernel, tk=tk, nk=seq // tk, delta_lo=delta_lo, delta_hi=delta_hi,
                          track_max=track_max),
        grid=(batch, seq // tq),
        in_specs=[pl.BlockSpec((1, tq, w), lambda b, i: (b, i, 0)),
                  pl.BlockSpec((1, seq, w), lambda b, i: (b, 0, 1), pipeline_mode=once),
                  pl.BlockSpec((1, seq, w), lambda b, i: (b, 0, 2), pipeline_mode=once),
                  pl.BlockSpec(strips.shape, lambda b, i: (0, 0, 0, 0), pipeline_mode=once),
                  pl.BlockSpec(consts.shape, lambda b, i: (0, 0))],
        out_specs=pl.BlockSpec((1, tq, w), lambda b, i: (b, i, 0)),
        out_shape=jax.ShapeDtypeStruct((batch, seq, w), BF16),
        scratch_shapes=[pltpu.VMEM((2 * DIFF_HEADS, tq, LANES), BF16), state, state, state],
        compiler_params=_cparams(("arbitrary", "arbitrary"), 56),
        name="diff_attention" if track_max else "diff_attention_bounded",
    )(hd, hd, hd, strips, consts)


def _diff_logit_bound(hd, t5_cols):
    n = DIFF_HEADS * 2
    sq = lambda t: jnp.max(jnp.sum(jnp.square(t.astype(F32)).reshape(-1, n, HEAD_DIM), axis=-1), axis=0)
    qn2 = sq(hd[..., :n * HEAD_DIM])
    kn2 = sq(hd[..., n * HEAD_DIM:2 * n * HEAD_DIM])
    return 1.05 * (jnp.sqrt(jnp.max(qn2 * kn2)) + LOG2E * jnp.max(jnp.abs(t5_cols)))


def _diff_mixer(hd, t5_cols, bounded_strips, consts):
    strips_b, lo_b, hi_b = bounded_strips

    def bounded(hd_, t5_, strips_, consts_):
        return _diff_attention(hd_, strips_, consts_, lo_b, hi_b, ATTN_TQ, DIFF_TK_BOUNDED, False)

    def tracked(hd_, t5_, strips_, consts_):
        strips, lo, hi = _diff_bias_strips(t5_, DIFF_BIAS_ROWS, ATTN_TK)
        return _diff_attention(hd_, strips, consts_, lo, hi, ATTN_TQ, ATTN_TK, True)

    return lax.cond(_diff_logit_bound(hd, t5_cols) <= ATTN_LOGIT_LIMIT, bounded, tracked,
                    hd, t5_cols, strips_b, consts)


def _layer_norm(z, g, b):
    mu = jnp.mean(z, axis=-1, keepdims=True)
    zc = z - mu
    var = jnp.mean(zc * zc, axis=-1, keepdims=True)
    return zc * lax.rsqrt(var + 1e-5) * g + b


def _split3(a):
    hi = a.astype(BF16)
    r1 = a - hi.astype(F32)
    mid = r1.astype(BF16)
    lo = (r1 - mid.astype(F32)).astype(BF16)
    return hi, mid, lo


def _merge_kernel(x_ref, xb_ref, ya_ref, yb_ref, o0_ref, o1_ref, o2_ref, l0_ref, l1_ref, l2_ref, yd_ref,
                  wg_ref, wna_ref, wgqa_ref, wdil_ref, wdiff_ref, wo_ref, wr_ref, ln_ref,
                  x1_ref, x1b_ref, lg_ref, *, alpha, d_model):
    xb = xb_ref[...]

    def gate(branch):
        g = jnp.dot(xb, wg_ref[:, branch * d_model:(branch + 1) * d_model], preferred_element_type=F32)
        return 1.0 / (1.0 + jnp.exp(-g))

    l0, l1, l2 = l0_ref[...], l1_ref[...], l2_ref[...]
    mx = jnp.maximum(jnp.maximum(l0, l1), l2)
    e0, e1, e2 = jnp.exp(l0 - mx), jnp.exp(l1 - mx), jnp.exp(l2 - mx)
    den = e0 + e1 + e2
    yc = (e0 / den) * o0_ref[...] + (e1 / den) * o1_ref[...] + (e2 / den) * o2_ref[...]
    merged = gate(0) * jnp.dot(ya_ref[...], wna_ref[...], preferred_element_type=F32)
    merged += gate(1) * jnp.dot(yb_ref[...], wgqa_ref[...], preferred_element_type=F32)
    merged += gate(2) * jnp.dot(yc.astype(BF16), wdil_ref[...], preferred_element_type=F32)
    merged += gate(3) * jnp.dot(yd_ref[...], wdiff_ref[...], preferred_element_type=F32)
    y = jnp.dot(merged.astype(BF16), wo_ref[...], preferred_element_type=F32)
    x1 = _layer_norm(alpha * x_ref[...] + y, ln_ref[0:1, :], ln_ref[1:2, :])
    x1_ref[...] = x1
    x1b_ref[...] = x1.astype(BF16)
    xh, xm, _ = _split3(x1)
    wh, wm, _ = wr_ref[0], wr_ref[1], wr_ref[2]
    lg_ref[...] = _dot_nt(wh, xh) + (_dot_nt(wh, xm) + _dot_nt(wm, xh))


def _merge(x, xb, ya, yb, dil, yd, w_g, w_na, w_gqa, w_dil, w_diff, w_o, w_r3, ln, alpha, tm):
    n, d = x.shape
    e = w_r3.shape[1]
    (o0, l0), (o1, l1), (o2, l2) = dil
    row = lambda w: pl.BlockSpec((tm, w), lambda i: (i, 0))
    full = lambda a: pl.BlockSpec(a.shape, lambda i: (0,) * a.ndim)
    return pl.pallas_call(
        functools.partial(_merge_kernel, alpha=alpha, d_model=d),
        grid=(n // tm,),
        in_specs=[row(d), row(d), row(ya.shape[1]), row(yb.shape[1]), row(LANES), row(LANES), row(LANES),
                  row(LANES), row(LANES), row(LANES), row(yd.shape[1]),
                  full(w_g), full(w_na), full(w_gqa), full(w_dil), full(w_diff), full(w_o), full(w_r3), full(ln)],
        out_specs=[row(d), row(d), pl.BlockSpec((e, tm), lambda i: (0, i))],
        out_shape=[jax.ShapeDtypeStruct((n, d), F32), jax.ShapeDtypeStruct((n, d), BF16),
                   jax.ShapeDtypeStruct((e, n), F32)],
        compiler_params=_cparams(("arbitrary",), 56),
        name="branch_merge",
    )(x, xb, ya, yb, o0, o1, o2, l0, l1, l2, yd, w_g, w_na, w_gqa, w_dil, w_diff, w_o, w_r3, ln)


def _route_kernel(lg_ref, aff_ref, posm_ref, posx_ref, *, cap):
    lg = lg_ref[0]
    n_e, seq = lg.shape
    mx = jnp.max(lg, axis=0, keepdims=True)
    ex = jnp.exp(lg - mx)
    aff = ex / jnp.sum(ex, axis=0, keepdims=True)
    aff_ref[0] = aff
    bits = pltpu.bitcast(aff, jnp.int32)

    def bisect(_, carry):
        lo, hi = carry
        mid = lo + ((hi - lo + 1) >> 1)
        cnt = jnp.sum((bits >= mid).astype(F32), axis=1, keepdims=True)
        ok = cnt >= float(cap)
        return jnp.where(ok, mid, lo), jnp.where(ok, hi, mid - 1)

    lo0 = jnp.zeros((n_e, 1), jnp.int32)
    hi0 = jnp.full((n_e, 1), 0x7F800000, jnp.int32)
    thr, _ = lax.fori_loop(0, 31, bisect, (lo0, hi0))
    gt = bits > thr
    eq = bits == thr
    needf = float(cap) - jnp.sum(gt.astype(F32), axis=1, keepdims=True)

    tri = (lax.broadcasted_iota(jnp.int32, (LANES, LANES), 0)
           < lax.broadcasted_iota(jnp.int32, (LANES, LANES), 1)).astype(BF16)

    def prefix(mask_bf16, j, carry):
        blk = mask_bf16[:, j * LANES:(j + 1) * LANES]
        excl = jnp.dot(blk, tri, preferred_element_type=F32) + carry
        return excl, carry + jnp.sum(blk.astype(F32), axis=1, keepdims=True)

    eqb = eq.astype(BF16)
    carry_eq = jnp.zeros((n_e, 1), F32)
    carry_sel = jnp.zeros((n_e, 1), F32)
    for j in range(seq // LANES):
        cols = slice(j * LANES, (j + 1) * LANES)
        rank, carry_eq = prefix(eqb, j, carry_eq)
        sel = jnp.logical_or(gt[:, cols], jnp.logical_and(eq[:, cols], rank < needf))
        selb = sel.astype(BF16)
        excl = jnp.dot(selb, tri, preferred_element_type=F32) + carry_sel
        carry_sel = carry_sel + jnp.sum(selb.astype(F32), axis=1, keepdims=True)
        pos = excl.astype(jnp.int32)
        posx_ref[0, :, cols] = pos
        posm_ref[0, :, cols] = jnp.where(sel, pos, -1)


def _route(logits_t, batch, seq, cap):
    n_e = logits_t.shape[0]
    lg = logits_t.reshape(n_e, batch, seq).transpose(1, 0, 2)
    spec = pl.BlockSpec((1, n_e, seq), lambda b: (b, 0, 0))
    return pl.pallas_call(
        functools.partial(_route_kernel, cap=cap),
        grid=(batch,),
        in_specs=[spec],
        out_specs=[spec, spec, spec],
        out_shape=[jax.ShapeDtypeStruct((batch, n_e, seq), F32), jax.ShapeDtypeStruct((batch, n_e, seq), jnp.int32),
                   jax.ShapeDtypeStruct((batch, n_e, seq), jnp.int32)],
        compiler_params=_cparams(("arbitrary",), 32),
        name="ec_route",
    )(lg)


def _gather_kernel(clo_ref, chi_ref, x_ref, pos_ref, aff_ref, xin_ref, gs_ref, acc_ref, *, n_e, nsb):
    base = (pl.program_id(0) * n_e + pl.program_id(1)) * nsb
    for sb in range(nsb):
        slot_ids = sb * SLOT_BLOCK + lax.broadcasted_iota(jnp.int32, (SLOT_BLOCK, TOKEN_CHUNK), 0)
        acc_ref[...] = jnp.zeros_like(acc_ref)

        def body(pair, g, slot_ids=slot_ids):
            hits = [pos_ref[0, 0, pl.ds(2 * pair + c, 1), :] == slot_ids for c in range(2)]
            xc = x_ref[0, pl.ds(pl.multiple_of(pair * (2 * TOKEN_CHUNK), 2 * TOKEN_CHUNK), 2 * TOKEN_CHUNK), :]
            acc_ref[...] += jnp.dot(jnp.concatenate(hits, axis=1).astype(BF16), xc, preferred_element_type=F32)
            for c in range(2):
                g = g + jnp.sum(jnp.where(hits[c], aff_ref[0, 0, pl.ds(2 * pair + c, 1), :], 0.0), axis=-1, keepdims=True)
            return g

        g = lax.fori_loop(clo_ref[base + sb] >> 1, (chi_ref[base + sb] + 1) >> 1, body,
                          jnp.zeros((SLOT_BLOCK, 1), F32))
        rows = slice(sb * SLOT_BLOCK, (sb + 1) * SLOT_BLOCK)
        xin_ref[0, 0, rows, :] = acc_ref[...].astype(BF16)
        gs_ref[0, 0, rows, :] = jnp.broadcast_to(g, (SLOT_BLOCK, LANES))


def _gather(x1b, posm, aff, clo, chi, cap):
    batch, seq, d = x1b.shape
    n_e = posm.shape[1]
    nsb = cap // SLOT_BLOCK
    nch = seq // TOKEN_CHUNK
    pos4 = posm.reshape(batch, n_e, nch, TOKEN_CHUNK)
    aff4 = aff.reshape(batch, n_e, nch, TOKEN_CHUNK)
    grid_spec = pltpu.PrefetchScalarGridSpec(
        num_scalar_prefetch=2,
        grid=(batch, n_e),
        in_specs=[pl.BlockSpec((1, seq, d), lambda b, e, *_: (b, 0, 0)),
                  pl.BlockSpec((1, 1, nch, TOKEN_CHUNK), lambda b, e, *_: (b, e, 0, 0)),
                  pl.BlockSpec((1, 1, nch, TOKEN_CHUNK), lambda b, e, *_: (b, e, 0, 0))],
        out_specs=[pl.BlockSpec((1, 1, cap, d), lambda b, e, *_: (b, e, 0, 0)),
                   pl.BlockSpec((1, 1, cap, LANES), lambda b, e, *_: (b, e, 0, 0))],
        scratch_shapes=[pltpu.VMEM((SLOT_BLOCK, d), F32)],
    )
    return pl.pallas_call(
        functools.partial(_gather_kernel, n_e=n_e, nsb=nsb),
        grid_spec=grid_spec,
        out_shape=[jax.ShapeDtypeStruct((batch, n_e, cap, d), BF16),
                   jax.ShapeDtypeStruct((batch, n_e, cap, LANES), F32)],
        compiler_params=_cparams(("arbitrary", "arbitrary"), 56),
        name="ec_gather",
    )(clo, chi, x1b, pos4, aff4)


def _ffn_kernel(x_ref, gs_ref, wg_ref, wu_ref, wd_ref, o_ref, *, fchunk):
    x = x_ref[0, 0]
    ff = wg_ref.shape[2]
    acc = jnp.zeros((x.shape[0], wd_ref.shape[2]), F32)
    for f0 in range(0, ff, fchunk):
        g = jnp.dot(x, wg_ref[0, :, f0:f0 + fchunk], preferred_element_type=F32)
        u = jnp.dot(x, wu_ref[0, :, f0:f0 + fchunk], preferred_element_type=F32)
        hid = (g / (1.0 + jnp.exp(-g))) * u
        acc += jnp.dot(hid.astype(BF16), wd_ref[0, f0:f0 + fchunk, :], preferred_element_type=F32)
    o_ref[0, 0] = (acc * gs_ref[0, 0][:, 0:1]).astype(o_ref.dtype)


def _expert_ffn(xin, gslot, wg, wu, wd, tc, fchunk):
    batch, n_e, cap, d = xin.shape
    ff = wg.shape[2]
    return pl.pallas_call(
        functools.partial(_ffn_kernel, fchunk=fchunk),
        grid=(n_e, batch, cap // tc),
        in_specs=[pl.BlockSpec((1, 1, tc, d), lambda e, b, i: (b, e, i, 0)),
                  pl.BlockSpec((1, 1, tc, LANES), lambda e, b, i: (b, e, i, 0)),
                  pl.BlockSpec((1, d, ff), lambda e, b, i: (e, 0, 0)),
                  pl.BlockSpec((1, d, ff), lambda e, b, i: (e, 0, 0)),
                  pl.BlockSpec((1, ff, d), lambda e, b, i: (e, 0, 0))],
        out_specs=pl.BlockSpec((1, 1, tc, d), lambda e, b, i: (b, e, i, 0)),
        out_shape=jax.ShapeDtypeStruct((batch, n_e, cap, d), BF16),
        compiler_params=_cparams(("arbitrary", "arbitrary", "arbitrary"), 56),
        name="expert_ffn",
    )(xin, gslot, wg, wu, wd)


def _combine_kernel(cs_ref, yo_ref, post_ref, y_ref, *, n_e, nch, cap):
    b = pl.program_id(0)
    c = pl.program_id(2)
    win = 2 * SLOT_BLOCK
    pos_all = post_ref[0]
    win_ids = lax.broadcasted_iota(jnp.int32, (TOKEN_CHUNK, win), 1)
    hits, wins, tails = [], [], []
    for e in range(n_e):
        base = (b * n_e + e) * (nch + 1) + c
        s_lo = cs_ref[base]
        s_hi = cs_ref[base + 1]
        w0 = pl.multiple_of(jnp.minimum((s_lo >> SLOT_SHIFT) << SLOT_SHIFT, cap - win), SLOT_BLOCK)
        hits.append((pos_all[:, e:e + 1] == (w0 + win_ids)).astype(BF16))
        wins.append(yo_ref[0, e, pl.ds(w0, win), :])
        sb_end = jnp.where(s_hi > s_lo, ((s_hi - 1) >> SLOT_SHIFT) + 1, 0)
        tails.append(((w0 + win) >> SLOT_SHIFT, sb_end))
    y_ref[0] = jnp.dot(jnp.concatenate(hits, axis=1), jnp.concatenate(wins, axis=0), preferred_element_type=F32)
    for e, (sb_from, sb_end) in enumerate(tails):
        def body(sb, carry, e=e):
            blk_ids = lax.broadcasted_iota(jnp.int32, (TOKEN_CHUNK, SLOT_BLOCK), 1)
            hit = post_ref[0, :, e:e + 1] == (sb * SLOT_BLOCK + blk_ids)
            yb = yo_ref[0, e, pl.ds(pl.multiple_of(sb * SLOT_BLOCK, SLOT_BLOCK), SLOT_BLOCK), :]
            y_ref[0] += jnp.dot(hit.astype(BF16), yb, preferred_element_type=F32)
            return carry

        lax.fori_loop(sb_from, sb_end, body, 0)


def _combine(yo, pos_t, cs, td):
    batch, n_e, cap, d = yo.shape
    seq = pos_t.shape[1]
    nch = seq // TOKEN_CHUNK
    grid_spec = pltpu.PrefetchScalarGridSpec(
        num_scalar_prefetch=1,
        grid=(batch, d // td, nch),
        in_specs=[pl.BlockSpec((1, n_e, cap, td), lambda b, j, c, *_: (b, 0, 0, j)),
                  pl.BlockSpec((1, TOKEN_CHUNK, n_e), lambda b, j, c, *_: (b, c, 0))],
        out_specs=pl.BlockSpec((1, TOKEN_CHUNK, td), lambda b, j, c, *_: (b, c, j)),
    )
    return pl.pallas_call(
        functools.partial(_combine_kernel, n_e=n_e, nch=nch, cap=cap),
        grid_spec=grid_spec,
        out_shape=jax.ShapeDtypeStruct((batch, seq, d), F32),
        compiler_params=_cparams(("arbitrary", "arbitrary", "arbitrary"), 56),
        name="ec_combine",
    )(cs, yo, pos_t)


def _ple_kernel(x1_ref, y_ref, p_ref, wpg_ref, wpp_ref, ln_ref, x3_ref, x3b_ref, *, alpha):
    x2 = _layer_norm(alpha * x1_ref[...] + y_ref[...], ln_ref[0:1, :], ln_ref[1:2, :])
    gate = 1.0 / (1.0 + jnp.exp(-jnp.dot(x2.astype(BF16), wpg_ref[...], preferred_element_type=F32)))
    emb = jnp.dot(p_ref[...].astype(BF16), wpp_ref[...], preferred_element_type=F32) * gate
    x3 = _layer_norm(alpha * x2 + emb, ln_ref[2:3, :], ln_ref[3:4, :])
    x3_ref[...] = x3
    x3b_ref[...] = x3.astype(BF16)


def _ple(x1, y, p, w_pg, w_pp, ln, alpha, tm):
    n, d = x1.shape
    row = lambda w: pl.BlockSpec((tm, w), lambda i: (i, 0))
    full = lambda a: pl.BlockSpec(a.shape, lambda i: (0,) * a.ndim)
    return pl.pallas_call(
        functools.partial(_ple_kernel, alpha=alpha),
        grid=(n // tm,),
        in_specs=[row(d), row(d), row(p.shape[1]), full(w_pg), full(w_pp), full(ln)],
        out_specs=[row(d), row(d)],
        out_shape=[jax.ShapeDtypeStruct((n, d), F32), jax.ShapeDtypeStruct((n, d), BF16)],
        compiler_params=_cparams(("arbitrary",), 40),
        name="ple_norm",
    )(x1, y, p, w_pg, w_pp, ln)


def _token_mixer_inputs(xb, w_in, qk_gain, batch, seq):
    d = w_in.shape[0]
    scale = HEAD_DIM ** -0.5
    na_w = NA_HEADS * HEAD_DIM
    gq_w, gkv_w = GQA_HEADS * HEAD_DIM, GQA_KV_HEADS * HEAD_DIM
    dil_w = DIL_HEADS * HEAD_DIM
    dqk_w = DIFF_HEADS * 2 * HEAD_DIM
    o = 0
    w_a = w_in[:, o:o + 3 * na_w]; o += 3 * na_w
    w_b = w_in[:, o:o + gq_w + 2 * gkv_w]; o += gq_w + 2 * gkv_w
    w_c = w_in[:, o:o + 3 * dil_w]; o += 3 * dil_w
    w_d = w_in[:, o:o + 3 * dqk_w]; o += 3 * dqk_w
    w_g = w_in[:, o:]

    def qscaled(w, qw):
        return jnp.concatenate([w[:, :qw] * (scale * LOG2E), w[:, qw:]], axis=1).astype(BF16)

    n = batch * seq
    ha = _matmul(xb, qscaled(w_a, na_w), BF16, 1024, 3 * na_w, "proj_na").reshape(batch, seq, -1)
    w_cg = jnp.concatenate(
        [w_c[:, t * dil_w + LANES * g:t * dil_w + LANES * (g + 1)] * (scale if t == 0 else 1.0)
         for g in range(len(DIL_GROUPS)) for t in range(3)], axis=1).astype(BF16)
    hc = _matmul(xb, w_cg, BF16, 1024, 3 * dil_w, "proj_dil").reshape(batch, seq, -1)
    w_dq = jnp.concatenate([w_d[:, :dqk_w] * (scale * LOG2E), w_d[:, dqk_w:]], axis=1).astype(BF16)
    hd = _matmul(xb, w_dq, BF16, 1024, 3 * dqk_w, "proj_diff").reshape(batch, seq, -1)
    qn, kd, vd = _proj_gqa(xb, w_b, qk_gain, seq, 512)
    shape3 = lambda t: t.reshape(batch, seq, t.shape[-1])
    return ha, shape3(qn), shape3(kd), shape3(vd), hc, hd, w_g.astype(BF16)


def _layer(x, xb, p, w, lam_init, alpha):
    batch, seq, d = x.shape
    n = batch * seq
    ha, qn, kd, vd, hc, hd, w_g = _token_mixer_inputs(xb.reshape(n, d), w["w_in"], w["qk_gain"], batch, seq)

    ya = _na_attention(ha, _na_bias_tiles(w["rpb"]), batch, seq)
    yb = lax.cond(_gqa_logit_bound(w["qk_gain"]) <= ATTN_LOGIT_LIMIT,
                  lambda *a: _gqa_attention(*a, ATTN_TQ, GQA_TK_BOUNDED, False),
                  lambda *a: _gqa_attention(*a, ATTN_TQ, ATTN_TK, True), qn, kd, vd)
    dil = []
    for g, (_, dilation) in enumerate(DIL_GROUPS):
        dil.append(_dilated_group(hc, w["dil_bias"][g], g, dilation, batch, seq, DIL_TQ, DIL_BLOCKS_PER_STEP))
    lp = w["lam_params"].astype(F32)
    lam = jnp.exp(jnp.sum(lp[0] * lp[1])) - jnp.exp(jnp.sum(lp[2] * lp[3])) + lam_init
    consts = jnp.zeros((8, LANES), F32).at[0].set(lam).at[1].set(w["diff_gain"]).at[2].set(1.0 - lam_init)
    yd = _diff_mixer(hd, w["t5"][:, DIL_HEADS:], w["diff_strips"], consts)

    flat = lambda t: t.reshape(n, t.shape[-1])
    w_r3 = jnp.stack(_split3(w["w_router"].T))
    ln = w["ln"]
    x1, x1b, logits_t = _merge(
        x.reshape(n, d), xb.reshape(n, d), flat(ya), flat(yb), [(flat(o), flat(l)) for o, l in dil], flat(yd), w_g,
        w["w_na"].astype(BF16), w["w_gqa"].astype(BF16), w["w_dil"].astype(BF16), w["w_diff"].astype(BF16),
        w["w_o"].astype(BF16), w_r3, jnp.stack([ln[0][0], ln[1][0]]), alpha, 512)

    cap = EC_CAPACITY * seq // N_EXPERTS
    aff, posm, posx = _route(logits_t, batch, seq, cap)
    nch = seq // TOKEN_CHUNK
    cs = jnp.concatenate([posx[:, :, ::TOKEN_CHUNK], jnp.full((batch, N_EXPERTS, 1), cap, jnp.int32)], axis=-1)
    edges = jnp.arange(cap // SLOT_BLOCK, dtype=jnp.int32) * SLOT_BLOCK
    clo = jnp.sum(cs[:, :, 1:, None] <= edges, axis=2).astype(jnp.int32)
    chi = jnp.sum(cs[:, :, :nch, None] < edges + SLOT_BLOCK, axis=2).astype(jnp.int32)
    xin, gslot = _gather(x1b.reshape(batch, seq, d), posm, aff, clo.reshape(-1), chi.reshape(-1), cap)
    yo = _expert_ffn(xin, gslot, w["w_eg"].astype(BF16), w["w_eu"].astype(BF16), w["w_ed"].astype(BF16), 512, 512)
    y = _combine(yo, posm.transpose(0, 2, 1), cs.reshape(-1), 512)

    ln4 = jnp.stack([ln[0][1], ln[1][1], ln[0][2], ln[1][2]])
    x3, x3b = _ple(x1, y.reshape(n, d), p.reshape(n, -1), w["w_pg"].astype(BF16), w["w_pp"].astype(BF16), ln4, alpha, 512)
    return x3.reshape(batch, seq, d), x3b.reshape(batch, seq, d)


def kernel(x, p, w_in, w_branch_na, w_branch_gqa, w_branch_dil, w_branch_diff, w_out, na_rel_bias, qk_norm_gain,
           diff_lambda, diff_norm_gain, t5_rel_bias, w_router, w_expert_gate, w_expert_up, w_expert_down,
           w_ple_proj, w_ple_gate, ln_gain, ln_bias):
    depth = w_in.shape[0]
    alpha = (2 * depth) ** 0.25
    xb = x.astype(BF16)
    diff_strips = _diff_bias_strips(t5_rel_bias[:, DIL_HEADS:], DIFF_BIAS_ROWS, DIFF_TK_BOUNDED)
    dil_bias = [_dil_bias_tiles(t5_rel_bias[:, 2 * g:2 * g + 2], dilation, DIL_TQ)
                for g, (_, dilation) in enumerate(DIL_GROUPS)]
    for i in range(depth):
        lam_init = 0.8 - 0.6 * math.exp(-0.3 * i)
        w = dict(diff_strips=diff_strips, dil_bias=dil_bias, w_in=w_in[i], w_na=w_branch_na[i], w_gqa=w_branch_gqa[i], w_dil=w_branch_dil[i],
                 w_diff=w_branch_diff[i], w_o=w_out[i], rpb=na_rel_bias[i], qk_gain=qk_norm_gain[i],
                 lam_params=diff_lambda[i], diff_gain=diff_norm_gain[i], t5=t5_rel_bias, w_router=w_router[i],
                 w_eg=w_expert_gate[i], w_eu=w_expert_up[i], w_ed=w_expert_down[i], w_pp=w_ple_proj[i],
                 w_pg=w_ple_gate[i], ln=(ln_gain[i], ln_bias[i]))
        x, xb = _layer(x, xb, p[i], w, lam_init, alpha)
    return x
```

```python
import functools
import math

import numpy as np
import jax
import jax.numpy as jnp
from jax import lax
from jax.experimental import pallas as pl
from jax.experimental.pallas import tpu as pltpu

F32 = jnp.float32
BF16 = jnp.bfloat16
NEG_INF = -1e30

V7X_VMEM_BYTES = 64 * 1024 * 1024
LANES = 128
LANE_SHIFT = 7

GRID_W = 64
HEAD_DIM = 64
NA_HEADS, NA_ROWS, NA_COLS = 6, 8, 16
NA_ROWS_PER_STEP = 4
LOG2E = 1.4426950408889634
ATTN_TQ, ATTN_TK = 1024, 256
GQA_TK_BOUNDED, DIFF_TK_BOUNDED = 1024, 512
ATTN_LOGIT_LIMIT = 64.0
DIL_TQ = 128
DIL_BLOCKS_PER_STEP = 4
DIFF_BIAS_ROWS = 256
GQA_HEADS, GQA_KV_HEADS = 6, 2
ROPE_THETA = 10000.0
DIL_GROUPS = ((128, 1), (512, 4), (2048, 16))
DIL_HEADS = 6
DIFF_HEADS = 4
T5_BUCKETS, T5_MAX_DISTANCE = 32, 1024
N_EXPERTS, EC_CAPACITY = 16, 2
SLOT_BLOCK = 128
SLOT_SHIFT = SLOT_BLOCK.bit_length() - 1
TOKEN_CHUNK = 256


def _cparams(semantics, vmem_mib):
    return pltpu.CompilerParams(dimension_semantics=semantics,
                                vmem_limit_bytes=min(vmem_mib * 1024 * 1024, V7X_VMEM_BYTES - 4 * 1024 * 1024))


def _mm_kernel(a_ref, b_ref, o_ref):
    o_ref[...] = jnp.dot(a_ref[...], b_ref[...], preferred_element_type=F32).astype(o_ref.dtype)


def _matmul(a, b, out_dtype, tm, tn, name):
    m, k = a.shape
    n = b.shape[1]
    return pl.pallas_call(
        _mm_kernel,
        grid=(n // tn, m // tm),
        in_specs=[pl.BlockSpec((tm, k), lambda j, i: (i, 0)),
                  pl.BlockSpec((k, tn), lambda j, i: (0, j))],
        out_specs=pl.BlockSpec((tm, tn), lambda j, i: (i, j)),
        out_shape=jax.ShapeDtypeStruct((m, n), out_dtype),
        compiler_params=_cparams(("arbitrary", "arbitrary"), 40),
        name=name,
    )(a, b)


def _toeplitz(u, rows, width):
    lead = u.shape[:-1]
    period = rows + width
    text = jnp.concatenate([u[..., rows - 1:], jnp.zeros(lead + (1,), u.dtype), u[..., :rows - 1]], axis=-1)
    flat = jnp.tile(text, (1,) * len(lead) + (rows,))[..., :rows * (period - 1)]
    return flat.reshape(lead + (rows, period - 1))[..., :width]


def _lane_lo(rows):
    return lax.broadcasted_iota(jnp.int32, (rows, LANES), 1) < HEAD_DIM


def _dot_nt(a, b):
    return lax.dot_general(a, b, (((1,), (1,)), ((), ())), preferred_element_type=F32)


NA_WIN_ROWS = NA_ROWS + NA_ROWS_PER_STEP


def _na_kernel(q_ref, k_ref, v_ref, tp_ref, o_ref, *, rows):
    r0 = pl.program_id(1) * NA_ROWS_PER_STEP
    ws = jnp.clip(r0 - NA_ROWS // 2, 0, rows - NA_WIN_ROWS)
    start = pl.multiple_of(ws * GRID_W, GRID_W)
    win = NA_WIN_ROWS * GRID_W
    npair = NA_WIN_ROWS // 2
    none_code = 2 * (2 * NA_ROWS - 1)
    codes = []
    for rr in range(NA_ROWS_PER_STEP):
        r = r0 + rr
        rs = jnp.clip(r - NA_ROWS // 2, 0, rows - NA_ROWS)
        row_codes = []
        for j in range(npair):
            a0 = ws + 2 * j
            d0 = a0 - r + (NA_ROWS - 1)
            vl = jnp.logical_and(a0 >= rs, a0 < rs + NA_ROWS)
            vr = jnp.logical_and(a0 + 1 >= rs, a0 + 1 < rs + NA_ROWS)
            row_codes.append(jnp.where(jnp.logical_and(vl, vr), d0,
                                       jnp.where(jnp.logical_or(vl, vr), 2 * NA_ROWS - 1 + d0, none_code)))
        codes.append(row_codes)
    tq = NA_ROWS_PER_STEP * GRID_W
    lo = _lane_lo(tq)
    for p in range(NA_HEADS // 2):
        cols = slice(LANES * p, LANES * (p + 1))
        qp = q_ref[0, :, cols]
        kw = k_ref[0, pl.ds(start, win), cols]
        vw = v_ref[0, pl.ds(start, win), cols]
        outs = []
        for hh in range(2):
            h = 2 * p + hh
            qm = jnp.where(lo if hh == 0 else jnp.logical_not(lo), qp, jnp.zeros_like(qp))
            bias = jnp.concatenate(
                [jnp.concatenate([tp_ref[h, c] for c in row_codes], axis=-1) for row_codes in codes], axis=0)
            s = _dot_nt(qm, kw) + bias
            e = jnp.exp2(s - jnp.max(s, axis=-1, keepdims=True))
            pv = jnp.dot(e.astype(BF16), vw, preferred_element_type=F32)
            outs.append(pv / jnp.sum(e, axis=-1, keepdims=True))
        o_ref[0, :, cols] = jnp.where(lo, outs[0], outs[1]).astype(o_ref.dtype)


def _na_bias_tiles(rpb):
    col = jnp.arange(GRID_W)
    cstart = jnp.clip(col - NA_COLS // 2, 0, GRID_W - NA_COLS)
    kc = jnp.arange(GRID_W)
    inwin = (kc[None, :] >= cstart[:, None]) & (kc[None, :] < cstart[:, None] + NA_COLS)
    pad = GRID_W - NA_COLS
    u = jnp.pad(rpb.astype(F32) * LOG2E, ((0, 0), (0, 0), (pad, pad)))
    t = jnp.where(inwin[None, None], _toeplitz(u, GRID_W, GRID_W), NEG_INF)
    neg = jnp.full_like(t[:, :NA_ROWS], NEG_INF)
    both = jnp.concatenate([t[:, :-1], t[:, 1:]], axis=-1)
    right = jnp.concatenate([neg, t[:, :NA_ROWS]], axis=-1)
    left = jnp.concatenate([t[:, NA_ROWS - 1:], neg], axis=-1)
    none = jnp.concatenate([neg[:, :1], neg[:, :1]], axis=-1)
    return jnp.concatenate([both, right, left, none], axis=1)


def _na_attention(ha, t2, batch, seq):
    rows = seq // GRID_W
    w = NA_HEADS * HEAD_DIM
    tq = GRID_W * NA_ROWS_PER_STEP
    return pl.pallas_call(
        functools.partial(_na_kernel, rows=rows),
        grid=(batch, rows // NA_ROWS_PER_STEP),
        in_specs=[pl.BlockSpec((1, tq, w), lambda b, r: (b, r, 0)),
                  pl.BlockSpec((1, seq, w), lambda b, r: (b, 0, 1)),
                  pl.BlockSpec((1, seq, w), lambda b, r: (b, 0, 2)),
                  pl.BlockSpec(t2.shape, lambda b, r: (0, 0, 0, 0))],
        out_specs=pl.BlockSpec((1, tq, w), lambda b, r: (b, r, 0)),
        out_shape=jax.ShapeDtypeStruct((batch, seq, w), BF16),
        compiler_params=_cparams(("arbitrary", "arbitrary"), 56),
        name="na_attention",
    )(ha, ha, ha, t2)


def _softmax_stream_update(s, vb, m_ref, l_ref, acc_ref, idx):
    m_prev = m_ref[idx]
    m_new = jnp.maximum(m_prev, jnp.max(s, axis=-1, keepdims=True))
    alpha = jnp.exp2(m_prev - m_new)
    p = jnp.exp2(s - jnp.tile(m_new, (1, s.shape[1] // LANES)))
    m_ref[idx] = m_new
    l_ref[idx] = alpha * l_ref[idx] + jnp.sum(p, axis=-1, keepdims=True)
    acc_ref[idx] = alpha * acc_ref[idx] + jnp.dot(p.astype(BF16), vb, preferred_element_type=F32)


def _init_streams(m_ref, l_ref, acc_ref):
    m_ref[...] = jnp.full(m_ref.shape, NEG_INF, F32)
    l_ref[...] = jnp.zeros(l_ref.shape, F32)
    acc_ref[...] = jnp.zeros(acc_ref.shape, F32)


def _gqa_kernel(q_ref, k_ref, v_ref, o_ref, qm_ref, m_ref, l_ref, acc_ref, *, tk, nk, track_max):
    tq = q_ref.shape[1]
    lo = _lane_lo(tq)
    rep = GQA_HEADS // GQA_KV_HEADS
    for h in range(GQA_HEADS):
        qp = q_ref[0, :, LANES * (h // 2):LANES * (h // 2 + 1)]
        qm_ref[h] = jnp.where(lo if h % 2 == 0 else jnp.logical_not(lo), qp, jnp.zeros_like(qp))
    _init_streams(m_ref, l_ref, acc_ref)

    def body(j, carry):
        ks = pl.multiple_of(j * tk, tk)
        for g in range(GQA_KV_HEADS):
            gcols = slice(LANES * g, LANES * (g + 1))
            kb = k_ref[0, pl.ds(ks, tk), gcols]
            vb = v_ref[0, pl.ds(ks, tk), gcols]
            for h in range(rep * g, rep * (g + 1)):
                s = _dot_nt(qm_ref[h], kb)
                if track_max:
                    _softmax_stream_update(s, vb, m_ref, l_ref, acc_ref, h)
                else:
                    p = jnp.exp2(s)
                    l_ref[h] += jnp.sum(p, axis=-1, keepdims=True)
                    acc_ref[h] += jnp.dot(p.astype(BF16), vb, preferred_element_type=F32)
        return carry

    lax.fori_loop(0, nk, body, 0)
    for p in range(GQA_HEADS // 2):
        o0 = acc_ref[2 * p] / l_ref[2 * p]
        o1 = acc_ref[2 * p + 1] / l_ref[2 * p + 1]
        o_ref[0, :, LANES * p:LANES * (p + 1)] = jnp.where(lo, o0, o1).astype(o_ref.dtype)


def _gqa_attention(q, kd, vd, tq, tk, track_max):
    batch, seq, wq = q.shape
    wk = kd.shape[-1]
    state = pltpu.VMEM((GQA_HEADS, tq, LANES), F32)
    return pl.pallas_call(
        functools.partial(_gqa_kernel, tk=tk, nk=seq // tk, track_max=track_max),
        grid=(batch, seq // tq),
        in_specs=[pl.BlockSpec((1, tq, wq), lambda b, i: (b, i, 0)),
                  pl.BlockSpec((1, seq, wk), lambda b, i: (b, 0, 0)),
                  pl.BlockSpec((1, seq, wk), lambda b, i: (b, 0, 0))],
        out_specs=pl.BlockSpec((1, tq, wq), lambda b, i: (b, i, 0)),
        out_shape=jax.ShapeDtypeStruct((batch, seq, wq), BF16),
        scratch_shapes=[pltpu.VMEM((GQA_HEADS, tq, LANES), BF16), state, state, state],
        compiler_params=_cparams(("arbitrary", "arbitrary"), 48),
        name="gqa_attention" if track_max else "gqa_attention_bounded",
    )(q, kd, vd)


def _gqa_logit_bound(qk_gain):
    gq = jnp.max(jnp.abs(qk_gain[0])) * (HEAD_DIM ** -0.5 * LOG2E)
    return 1.05 * HEAD_DIM * gq * jnp.max(jnp.abs(qk_gain[1]))


def _rope_tables(seq):
    quarter = HEAD_DIM // 4
    freqs = ROPE_THETA ** (-jnp.arange(quarter, dtype=F32) / quarter)
    t = jnp.arange(seq)
    row = (t // GRID_W).astype(F32)
    col = (t % GRID_W).astype(F32)
    ang_r = row[:, None] * freqs[None, :]
    ang_c = col[:, None] * freqs[None, :]
    cos = jnp.concatenate([jnp.cos(ang_r)] * 2 + [jnp.cos(ang_c)] * 2, axis=-1)
    sin = jnp.concatenate([-jnp.sin(ang_r), jnp.sin(ang_r), -jnp.sin(ang_c), jnp.sin(ang_c)], axis=-1)
    return cos, sin


def _proj_gqa_kernel(x_ref, w_ref, cos_ref, sin_ref, gain_ref, mean_ref, q_ref, k_ref, v_ref):
    h = jnp.dot(x_ref[...], w_ref[...], preferred_element_type=F32)
    nq = q_ref.shape[1] // LANES
    nk = k_ref.shape[1] // LANES
    lane = lax.broadcasted_iota(jnp.int32, (h.shape[0], LANES), 1)
    first = (lane & (HEAD_DIM // 2 - 1)) < HEAD_DIM // 4
    for blk in range(nq + nk):
        t = h[:, LANES * blk:LANES * (blk + 1)]
        hi, mid, low = _split3(t * t)
        ms = (jnp.dot(hi, mean_ref[...], preferred_element_type=F32)
              + jnp.dot(mid, mean_ref[...], preferred_element_type=F32)
              + jnp.dot(low, mean_ref[...], preferred_element_type=F32))
        y = t * lax.rsqrt(ms + 1e-6) * gain_ref[blk:blk + 1, :]
        partner = jnp.where(first, pltpu.roll(y, LANES - HEAD_DIM // 4, 1), pltpu.roll(y, HEAD_DIM // 4, 1))
        out = (y * cos_ref[...] + partner * sin_ref[...]).astype(BF16)
        if blk < nq:
            q_ref[:, LANES * blk:LANES * (blk + 1)] = out
        else:
            k_ref[:, LANES * (blk - nq):LANES * (blk - nq + 1)] = out
    v_ref[...] = h[:, LANES * (nq + nk):].astype(BF16)


def _proj_gqa(xb, w_b, qk_gain, seq, tm):
    n, d = xb.shape
    gq, gkv = GQA_HEADS * HEAD_DIM, GQA_KV_HEADS * HEAD_DIM
    dup = lambda w: jnp.concatenate([w[:, i * HEAD_DIM:(i + 1) * HEAD_DIM] for i in range(GQA_KV_HEADS) for _ in range(2)], axis=1)
    w_all = jnp.concatenate([w_b[:, :gq], dup(w_b[:, gq:gq + gkv]), dup(w_b[:, gq + gkv:])], axis=1).astype(BF16)
    cos, sin = _rope_tables(seq)
    cos2, sin2 = jnp.tile(cos, (1, 2)), jnp.tile(sin, (1, 2))
    nq, nk = gq // LANES, 2 * gkv // LANES
    gain = jnp.concatenate([jnp.tile(qk_gain[0] * (HEAD_DIM ** -0.5 * LOG2E), (nq, 2)), jnp.tile(qk_gain[1], (nk, 2)),
                            jnp.zeros((8 - nq - nk, LANES), F32)], axis=0)
    head = jnp.arange(LANES) // HEAD_DIM
    mean_mat = jnp.where(head[:, None] == head[None, :], 1.0 / HEAD_DIM, 0.0).astype(BF16)
    nt = seq // tm
    row = lambda w: pl.BlockSpec((tm, w), lambda i: (i, 0))
    full = lambda a: pl.BlockSpec(a.shape, lambda i: (0,) * a.ndim)
    pos = pl.BlockSpec((tm, LANES), lambda i: (i % nt, 0))
    return pl.pallas_call(
        _proj_gqa_kernel,
        grid=(n // tm,),
        in_specs=[row(d), full(w_all), pos, pos, full(gain), full(mean_mat)],
        out_specs=[row(gq), row(2 * gkv), row(2 * gkv)],
        out_shape=[jax.ShapeDtypeStruct((n, gq), BF16), jax.ShapeDtypeStruct((n, 2 * gkv), BF16),
                   jax.ShapeDtypeStruct((n, 2 * gkv), BF16)],
        compiler_params=_cparams(("arbitrary",), 40),
        name="proj_gqa",
    )(xb, w_all, cos2, sin2, gain, mean_mat)


def _dil_kernel(q_ref, k_ref, v_ref, bias_ref, o_ref, lse_ref, *, sub_len, nblk, tq):
    half = DIL_GROUPS[0][0] // 2
    win = tq + 2 * half
    lo = _lane_lo(tq)
    for qq in range(q_ref.shape[1] // tq):
        i = pl.program_id(2) * (q_ref.shape[1] // tq) + qq
        start = pl.multiple_of(jnp.clip(i * tq - half, 0, sub_len - win), half)
        var = jnp.where(i == 0, 0, jnp.where(i == nblk - 1, 2, 1))
        rows = slice(qq * tq, (qq + 1) * tq)
        qp = q_ref[0, rows, :]
        kw = k_ref[0, pl.ds(start, win), :]
        vw = v_ref[0, pl.ds(start, win), :]
        outs, lses = [], []
        for hh in range(2):
            qm = jnp.where(lo if hh == 0 else jnp.logical_not(lo), qp, jnp.zeros_like(qp))
            s = _dot_nt(qm, kw) + bias_ref[hh, var]
            m = jnp.max(s, axis=-1, keepdims=True)
            e = jnp.exp(s - m)
            l = jnp.sum(e, axis=-1, keepdims=True)
            outs.append(jnp.dot(e.astype(BF16), vw, preferred_element_type=F32) / l)
            lses.append(jnp.broadcast_to(m + jnp.log(l), (tq, LANES)))
        o_ref[0, rows, :] = jnp.where(lo, outs[0], outs[1])
        lse_ref[0, rows, :] = jnp.where(lo, lses[0], lses[1])


def _t5_bucket(rel):
    half = T5_BUCKETS // 2
    exact = half // 2
    n = jnp.abs(rel)
    nf = jnp.maximum(n, 1).astype(F32)
    large = exact + (jnp.log(nf / exact) / math.log(T5_MAX_DISTANCE / exact) * (half - exact)).astype(jnp.int32)
    large = jnp.minimum(large, half - 1)
    return jnp.where(rel > 0, half, 0) + jnp.where(n < exact, n, large)


def _dil_bias_tiles(t5_cols, dilation, tq):
    half = DIL_GROUPS[0][0] // 2
    win = tq + 2 * half
    wide = win + 2 * half
    j = jnp.arange(tq - 1 + wide) - (tq - 1) - 2 * half
    vals = t5_cols.astype(F32)[_t5_bucket(j * dilation)]
    u = jnp.where((jnp.abs(j) <= half)[:, None], vals, NEG_INF).T
    a = _toeplitz(u, tq, wide)
    return jnp.stack([a[:, :, 2 * half - s:2 * half - s + win] for s in (0, half, 2 * half)], axis=1)


def _dilated_group(hc, bias, g, dilation, batch, seq, tq, qb):
    sub_len = seq // dilation
    nblk = sub_len // tq
    ts = tq * qb
    if dilation == 1:
        view, nb, g = hc, hc.shape[-1] // LANES, 3 * g
    else:
        view, nb, g = hc[..., 3 * LANES * g:3 * LANES * (g + 1)].reshape(batch, sub_len, dilation * 3 * LANES), 3, 0
    nq = 1
    out, lse = pl.pallas_call(
        functools.partial(_dil_kernel, sub_len=sub_len, nblk=nblk, tq=tq),
        grid=(batch, dilation, nblk // qb),
        in_specs=[pl.BlockSpec((1, ts, LANES), lambda b, r, i: (b, i, r * nb + g)),
                  pl.BlockSpec((1, sub_len, LANES), lambda b, r, i: (b, 0, r * nb + nq + g)),
                  pl.BlockSpec((1, sub_len, LANES), lambda b, r, i: (b, 0, r * nb + 2 * nq + g)),
                  pl.BlockSpec(bias.shape, lambda b, r, i: (0, 0, 0, 0))],
        out_specs=[pl.BlockSpec((1, ts, LANES), lambda b, r, i: (b, i, r)),
                   pl.BlockSpec((1, ts, LANES), lambda b, r, i: (b, i, r))],
        out_shape=[jax.ShapeDtypeStruct((batch, sub_len, dilation * LANES), F32)] * 2,
        compiler_params=_cparams(("arbitrary", "arbitrary", "arbitrary"), 32),
        name=f"dilated_attention_d{dilation}",
    )(view, view, view, bias)
    return out.reshape(batch, seq, LANES), lse.reshape(batch, seq, LANES)


def _diff_kernel(q_ref, k_ref, v_ref, strip_ref, c_ref, o_ref, qm_ref, m_ref, l_ref, acc_ref, *,
                 tk, nk, delta_lo, delta_hi, track_max):
    tq = q_ref.shape[1]
    i = pl.program_id(1)
    lo = _lane_lo(tq)
    for h in range(DIFF_HEADS):
        qp = q_ref[0, :, LANES * h:LANES * (h + 1)]
        qm_ref[2 * h] = jnp.where(lo, qp, jnp.zeros_like(qp))
        qm_ref[2 * h + 1] = jnp.where(lo, jnp.zeros_like(qp), qp)
    _init_streams(m_ref, l_ref, acc_ref)

    def body(j, carry):
        ks = pl.multiple_of(j * tk, tk)
        sub = strip_ref.shape[2]
        blk0 = [(jnp.clip(j * tk - i * tq - a * sub, delta_lo, delta_hi) - delta_lo) >> LANE_SHIFT
                for a in range(tq // sub)]
        for h in range(DIFF_HEADS):
            cols = slice(LANES * h, LANES * (h + 1))
            kb = k_ref[0, pl.ds(ks, tk), cols]
            vb = v_ref[0, pl.ds(ks, tk), cols]
            bias = jnp.concatenate(
                [jnp.concatenate([strip_ref[h, b0 + c] for c in range(tk // LANES)], axis=-1) for b0 in blk0], axis=0)
            for t in range(2):
                s = _dot_nt(qm_ref[2 * h + t], kb) + bias
                if track_max:
                    _softmax_stream_update(s, vb, m_ref, l_ref, acc_ref, 2 * h + t)
                else:
                    p = jnp.exp2(s)
                    l_ref[2 * h + t] += jnp.sum(p, axis=-1, keepdims=True)
                    acc_ref[2 * h + t] += jnp.dot(p.astype(BF16), vb, preferred_element_type=F32)
        return carry

    lax.fori_loop(0, nk, body, 0)
    lam = c_ref[0:1, :]
    for h in range(DIFF_HEADS):
        out = acc_ref[2 * h] / l_ref[2 * h] - lam * (acc_ref[2 * h + 1] / l_ref[2 * h + 1])
        y = out * lax.rsqrt(jnp.mean(out * out, axis=-1, keepdims=True) + 1e-6)
        o_ref[0, :, LANES * h:LANES * (h + 1)] = ((y * c_ref[1:2, :]) * c_ref[2:3, :]).astype(o_ref.dtype)


def _t5_saturation_distance():
    half = T5_BUCKETS // 2
    exact = half // 2
    n = np.arange(1, 4 * T5_MAX_DISTANCE, dtype=np.float32)
    large = exact + (np.log(n / exact) / math.log(T5_MAX_DISTANCE / exact) * (half - exact)).astype(np.int32)
    return int(np.argmax(large >= half - 1)) + 1 + 8


def _diff_bias_strips(t5_cols, tq, tk):
    unit = math.gcd(tq, tk)
    sat = _t5_saturation_distance()
    delta_lo = -unit * ((sat + tk - 1) // unit + 1)
    delta_hi = unit * ((sat + tq - 1) // unit + 1)
    width = delta_hi - delta_lo + tk
    rel = jnp.arange(tq - 1 + width) - (tq - 1) + delta_lo
    u = (t5_cols.astype(F32) * LOG2E)[_t5_bucket(rel)].T
    strip = _toeplitz(u, tq, width)
    nh = strip.shape[0]
    return strip.reshape(nh, tq, width // LANES, LANES).transpose(0, 2, 1, 3), delta_lo, delta_hi


def _diff_attention(hd, strips, consts, delta_lo, delta_hi, tq, tk, track_max):
    batch, seq = hd.shape[:2]
    w = DIFF_HEADS * LANES
    once = pl.Buffered(1)
    state = pltpu.VMEM((2 * DIFF_HEADS, tq, LANES), F32)
    return pl.pallas_call(
        functools.partial(_diff_kernel, tk=tk, nk=seq // tk, delta_lo=delta_lo, delta_hi=delta_hi,
                          track_max=track_max),
        grid=(batch, seq // tq),
        in_specs=[pl.BlockSpec((1, tq, w), lambda b, i: (b, i, 0)),
                  pl.BlockSpec((1, seq, w), lambda b, i: (b, 0, 1), pipeline_mode=once),
                  pl.BlockSpec((1, seq, w), lambda b, i: (b, 0, 2), pipeline_mode=once),
                  pl.BlockSpec(strips.shape, lambda b, i: (0, 0, 0, 0), pipeline_mode=once),
                  pl.BlockSpec(consts.shape, lambda b, i: (0, 0))],
        out_specs=pl.BlockSpec((1, tq, w), lambda b, i: (b, i, 0)),
        out_shape=jax.ShapeDtypeStruct((batch, seq, w), BF16),
        scratch_shapes=[pltpu.VMEM((2 * DIFF_HEADS, tq, LANES), BF16), state, state, state],
        compiler_params=_cparams(("arbitrary", "arbitrary"), 56),
        name="diff_attention" if track_max else "diff_attention_bounded",
    )(hd, hd, hd, strips, consts)


def _proj_diff_kernel(x_ref, w_ref, ind_ref, o_ref, st_ref):
    hb = jnp.dot(x_ref[...], w_ref[...], preferred_element_type=F32).astype(BF16)
    o_ref[...] = hb
    qk = hb[:, :ind_ref.shape[0]].astype(F32)
    norms = jnp.dot((qk * qk).astype(BF16), ind_ref[...], preferred_element_type=F32)
    st_ref[0] = jnp.broadcast_to(jnp.max(norms, axis=0, keepdims=True), st_ref.shape[1:])


def _proj_diff(xb, w_dq, tm):
    n, d = xb.shape
    width = w_dq.shape[1]
    nqk = 2 * DIFF_HEADS * 2 * HEAD_DIM
    ind = (jnp.arange(nqk)[:, None] // HEAD_DIM == jnp.arange(LANES)[None, :]).astype(BF16)
    return pl.pallas_call(
        _proj_diff_kernel,
        grid=(n // tm,),
        in_specs=[pl.BlockSpec((tm, d), lambda i: (i, 0)), pl.BlockSpec(w_dq.shape, lambda i: (0, 0)),
                  pl.BlockSpec(ind.shape, lambda i: (0, 0))],
        out_specs=[pl.BlockSpec((tm, width), lambda i: (i, 0)), pl.BlockSpec((1, 8, LANES), lambda i: (i, 0, 0))],
        out_shape=[jax.ShapeDtypeStruct((n, width), BF16), jax.ShapeDtypeStruct((n // tm, 8, LANES), F32)],
        compiler_params=_cparams(("arbitrary",), 40),
        name="proj_diff",
    )(xb, w_dq, ind)


def _diff_logit_bound(norm_stats, t5_cols):
    n = DIFF_HEADS * 2
    mx = jnp.max(norm_stats, axis=(0, 1))
    return 1.05 * (jnp.sqrt(jnp.max(mx[:n] * mx[n:2 * n])) + LOG2E * jnp.max(jnp.abs(t5_cols)))


def _diff_mixer(hd, norm_stats, t5_cols, bounded_strips, consts):
    strips_b, lo_b, hi_b = bounded_strips

    def bounded(hd_, t5_, strips_, consts_):
        return _diff_attention(hd_, strips_, consts_, lo_b, hi_b, ATTN_TQ, DIFF_TK_BOUNDED, False)

    def tracked(hd_, t5_, strips_, consts_):
        strips, lo, hi = _diff_bias_strips(t5_, DIFF_BIAS_ROWS, ATTN_TK)
        return _diff_attention(hd_, strips, consts_, lo, hi, ATTN_TQ, ATTN_TK, True)

    return lax.cond(_diff_logit_bound(norm_stats, t5_cols) <= ATTN_LOGIT_LIMIT, bounded, tracked,
                    hd, t5_cols, strips_b, consts)


def _layer_norm(z, g, b):
    mu = jnp.mean(z, axis=-1, keepdims=True)
    zc = z - mu
    var = jnp.mean(zc * zc, axis=-1, keepdims=True)
    return zc * lax.rsqrt(var + 1e-5) * g + b


def _split3(a):
    hi = a.astype(BF16)
    r1 = a - hi.astype(F32)
    mid = r1.astype(BF16)
    lo = (r1 - mid.astype(F32)).astype(BF16)
    return hi, mid, lo


def _merge_kernel(x_ref, xb_ref, ya_ref, yb_ref, o0_ref, o1_ref, o2_ref, l0_ref, l1_ref, l2_ref, yd_ref,
                  wg_ref, wna_ref, wgqa_ref, wdil_ref, wdiff_ref, wo_ref, wr_ref, ln_ref,
                  x1_ref, x1b_ref, lg_ref, *, alpha, d_model):
    xb = xb_ref[...]

    def gate(branch):
        g = jnp.dot(xb, wg_ref[:, branch * d_model:(branch + 1) * d_model], preferred_element_type=F32)
        return 1.0 / (1.0 + jnp.exp(-g))

    l0, l1, l2 = l0_ref[...], l1_ref[...], l2_ref[...]
    mx = jnp.maximum(jnp.maximum(l0, l1), l2)
    e0, e1, e2 = jnp.exp(l0 - mx), jnp.exp(l1 - mx), jnp.exp(l2 - mx)
    den = e0 + e1 + e2
    yc = (e0 / den) * o0_ref[...] + (e1 / den) * o1_ref[...] + (e2 / den) * o2_ref[...]
    merged = gate(0) * jnp.dot(ya_ref[...], wna_ref[...], preferred_element_type=F32)
    merged += gate(1) * jnp.dot(yb_ref[...], wgqa_ref[...], preferred_element_type=F32)
    merged += gate(2) * jnp.dot(yc.astype(BF16), wdil_ref[...], preferred_element_type=F32)
    merged += gate(3) * jnp.dot(yd_ref[...], wdiff_ref[...], preferred_element_type=F32)
    y = jnp.dot(merged.astype(BF16), wo_ref[...], preferred_element_type=F32)
    x1 = _layer_norm(alpha * x_ref[...] + y, ln_ref[0:1, :], ln_ref[1:2, :])
    x1_ref[...] = x1
    x1b_ref[...] = x1.astype(BF16)
    xh, xm, _ = _split3(x1)
    wh, wm, _ = wr_ref[0], wr_ref[1], wr_ref[2]
    lg_ref[...] = _dot_nt(wh, xh) + (_dot_nt(wh, xm) + _dot_nt(wm, xh))


def _merge(x, xb, ya, yb, dil, yd, w_g, w_na, w_gqa, w_dil, w_diff, w_o, w_r3, ln, alpha, tm):
    n, d = x.shape
    e = w_r3.shape[1]
    (o0, l0), (o1, l1), (o2, l2) = dil
    row = lambda w: pl.BlockSpec((tm, w), lambda i: (i, 0))
    full = lambda a: pl.BlockSpec(a.shape, lambda i: (0,) * a.ndim)
    return pl.pallas_call(
        functools.partial(_merge_kernel, alpha=alpha, d_model=d),
        grid=(n // tm,),
        in_specs=[row(d), row(d), row(ya.shape[1]), row(yb.shape[1]), row(LANES), row(LANES), row(LANES),
                  row(LANES), row(LANES), row(LANES), row(yd.shape[1]),
                  full(w_g), full(w_na), full(w_gqa), full(w_dil), full(w_diff), full(w_o), full(w_r3), full(ln)],
        out_specs=[row(d), row(d), pl.BlockSpec((e, tm), lambda i: (0, i))],
        out_shape=[jax.ShapeDtypeStruct((n, d), F32), jax.ShapeDtypeStruct((n, d), BF16),
                   jax.ShapeDtypeStruct((e, n), F32)],
        compiler_params=_cparams(("arbitrary",), 56),
        name="branch_merge",
    )(x, xb, ya, yb, o0, o1, o2, l0, l1, l2, yd, w_g, w_na, w_gqa, w_dil, w_diff, w_o, w_r3, ln)


def _route_kernel(lg_ref, aff_ref, posm_ref, posx_ref, *, cap):
    lg = lg_ref[0]
    n_e, seq = lg.shape
    mx = jnp.max(lg, axis=0, keepdims=True)
    ex = jnp.exp(lg - mx)
    aff = ex / jnp.sum(ex, axis=0, keepdims=True)
    aff_ref[0] = aff
    bits = pltpu.bitcast(aff, jnp.int32)

    def bisect(_, carry):
        lo, hi = carry
        mid = lo + ((hi - lo + 1) >> 1)
        cnt = jnp.sum((bits >= mid).astype(F32), axis=1, keepdims=True)
        ok = cnt >= float(cap)
        return jnp.where(ok, mid, lo), jnp.where(ok, hi, mid - 1)

    lo0 = jnp.zeros((n_e, 1), jnp.int32)
    hi0 = jnp.full((n_e, 1), 0x7F800000, jnp.int32)
    thr, _ = lax.fori_loop(0, 31, bisect, (lo0, hi0))
    gt = bits > thr
    eq = bits == thr
    needf = float(cap) - jnp.sum(gt.astype(F32), axis=1, keepdims=True)

    tri = (lax.broadcasted_iota(jnp.int32, (LANES, LANES), 0)
           < lax.broadcasted_iota(jnp.int32, (LANES, LANES), 1)).astype(BF16)

    def prefix(mask_bf16, j, carry):
        blk = mask_bf16[:, j * LANES:(j + 1) * LANES]
        excl = jnp.dot(blk, tri, preferred_element_type=F32) + carry
        return excl, carry + jnp.sum(blk.astype(F32), axis=1, keepdims=True)

    eqb = eq.astype(BF16)
    carry_eq = jnp.zeros((n_e, 1), F32)
    carry_sel = jnp.zeros((n_e, 1), F32)
    for j in range(seq // LANES):
        cols = slice(j * LANES, (j + 1) * LANES)
        rank, carry_eq = prefix(eqb, j, carry_eq)
        sel = jnp.logical_or(gt[:, cols], jnp.logical_and(eq[:, cols], rank < needf))
        selb = sel.astype(BF16)
        excl = jnp.dot(selb, tri, preferred_element_type=F32) + carry_sel
        carry_sel = carry_sel + jnp.sum(selb.astype(F32), axis=1, keepdims=True)
        pos = excl.astype(jnp.int32)
        posx_ref[0, :, cols] = pos
        posm_ref[0, :, cols] = jnp.where(sel, pos, -1)


def _route(logits_t, batch, seq, cap):
    n_e = logits_t.shape[0]
    lg = logits_t.reshape(n_e, batch, seq).transpose(1, 0, 2)
    spec = pl.BlockSpec((1, n_e, seq), lambda b: (b, 0, 0))
    return pl.pallas_call(
        functools.partial(_route_kernel, cap=cap),
        grid=(batch,),
        in_specs=[spec],
        out_specs=[spec, spec, spec],
        out_shape=[jax.ShapeDtypeStruct((batch, n_e, seq), F32), jax.ShapeDtypeStruct((batch, n_e, seq), jnp.int32),
                   jax.ShapeDtypeStruct((batch, n_e, seq), jnp.int32)],
        compiler_params=_cparams(("arbitrary",), 32),
        name="ec_route",
    )(lg)


def _gather_kernel(clo_ref, chi_ref, x_ref, pos_ref, aff_ref, xin_ref, gs_ref, acc_ref, *, n_e, nsb):
    base = (pl.program_id(0) * n_e + pl.program_id(1)) * nsb
    for sb in range(nsb):
        slot_ids = sb * SLOT_BLOCK + lax.broadcasted_iota(jnp.int32, (SLOT_BLOCK, TOKEN_CHUNK), 0)
        acc_ref[...] = jnp.zeros_like(acc_ref)

        def body(pair, g, slot_ids=slot_ids):
            hits = [pos_ref[0, 0, pl.ds(2 * pair + c, 1), :] == slot_ids for c in range(2)]
            xc = x_ref[0, pl.ds(pl.multiple_of(pair * (2 * TOKEN_CHUNK), 2 * TOKEN_CHUNK), 2 * TOKEN_CHUNK), :]
            acc_ref[...] += jnp.dot(jnp.concatenate(hits, axis=1).astype(BF16), xc, preferred_element_type=F32)
            for c in range(2):
                g = g + jnp.sum(jnp.where(hits[c], aff_ref[0, 0, pl.ds(2 * pair + c, 1), :], 0.0), axis=-1, keepdims=True)
            return g

        g = lax.fori_loop(clo_ref[base + sb] >> 1, (chi_ref[base + sb] + 1) >> 1, body,
                          jnp.zeros((SLOT_BLOCK, 1), F32))
        rows = slice(sb * SLOT_BLOCK, (sb + 1) * SLOT_BLOCK)
        xin_ref[0, 0, rows, :] = acc_ref[...].astype(BF16)
        gs_ref[0, 0, rows, :] = jnp.broadcast_to(g, (SLOT_BLOCK, LANES))


def _gather(x1b, posm, aff, clo, chi, cap):
    batch, seq, d = x1b.shape
    n_e = posm.shape[1]
    nsb = cap // SLOT_BLOCK
    nch = seq // TOKEN_CHUNK
    pos4 = posm.reshape(batch, n_e, nch, TOKEN_CHUNK)
    aff4 = aff.reshape(batch, n_e, nch, TOKEN_CHUNK)
    grid_spec = pltpu.PrefetchScalarGridSpec(
        num_scalar_prefetch=2,
        grid=(batch, n_e),
        in_specs=[pl.BlockSpec((1, seq, d), lambda b, e, *_: (b, 0, 0)),
                  pl.BlockSpec((1, 1, nch, TOKEN_CHUNK), lambda b, e, *_: (b, e, 0, 0)),
                  pl.BlockSpec((1, 1, nch, TOKEN_CHUNK), lambda b, e, *_: (b, e, 0, 0))],
        out_specs=[pl.BlockSpec((1, 1, cap, d), lambda b, e, *_: (b, e, 0, 0)),
                   pl.BlockSpec((1, 1, cap, LANES), lambda b, e, *_: (b, e, 0, 0))],
        scratch_shapes=[pltpu.VMEM((SLOT_BLOCK, d), F32)],
    )
    return pl.pallas_call(
        functools.partial(_gather_kernel, n_e=n_e, nsb=nsb),
        grid_spec=grid_spec,
        out_shape=[jax.ShapeDtypeStruct((batch, n_e, cap, d), BF16),
                   jax.ShapeDtypeStruct((batch, n_e, cap, LANES), F32)],
        compiler_params=_cparams(("arbitrary", "arbitrary"), 56),
        name="ec_gather",
    )(clo, chi, x1b, pos4, aff4)


def _ffn_kernel(x_ref, gs_ref, wg_ref, wu_ref, wd_ref, o_ref, *, fchunk):
    x = x_ref[0, 0]
    ff = wg_ref.shape[3]
    acc = jnp.zeros((x.shape[0], wd_ref.shape[3]), F32)
    for f0 in range(0, ff, fchunk):
        g = jnp.dot(x, wg_ref[0, 0, :, f0:f0 + fchunk], preferred_element_type=F32)
        u = jnp.dot(x, wu_ref[0, 0, :, f0:f0 + fchunk], preferred_element_type=F32)
        hid = (g / (1.0 + jnp.exp(-g))) * u
        acc += jnp.dot(hid.astype(BF16), wd_ref[0, 0, f0:f0 + fchunk, :], preferred_element_type=F32)
    o_ref[0, 0] = (acc * gs_ref[0, 0][:, 0:1]).astype(o_ref.dtype)


def _expert_ffn(xin, gslot, wg, wu, wd, layer, tc, fchunk):
    batch, n_e, cap, d = xin.shape
    ff = wg.shape[3]
    return pl.pallas_call(
        functools.partial(_ffn_kernel, fchunk=fchunk),
        grid=(n_e, batch, cap // tc),
        in_specs=[pl.BlockSpec((1, 1, tc, d), lambda e, b, i: (b, e, i, 0)),
                  pl.BlockSpec((1, 1, tc, LANES), lambda e, b, i: (b, e, i, 0)),
                  pl.BlockSpec((1, 1, d, ff), lambda e, b, i: (layer, e, 0, 0)),
                  pl.BlockSpec((1, 1, d, ff), lambda e, b, i: (layer, e, 0, 0)),
                  pl.BlockSpec((1, 1, ff, d), lambda e, b, i: (layer, e, 0, 0))],
        out_specs=pl.BlockSpec((1, 1, tc, d), lambda e, b, i: (b, e, i, 0)),
        out_shape=jax.ShapeDtypeStruct((batch, n_e, cap, d), BF16),
        compiler_params=_cparams(("arbitrary", "arbitrary", "arbitrary"), 56),
        name="expert_ffn",
    )(xin, gslot, wg, wu, wd)


def _combine_kernel(cs_ref, yo_ref, post_ref, y_ref, *, n_e, nch, cap):
    b = pl.program_id(0)
    c = pl.program_id(2)
    win = 2 * SLOT_BLOCK
    pos_all = post_ref[0]
    win_ids = lax.broadcasted_iota(jnp.int32, (TOKEN_CHUNK, win), 1)
    hits, wins, tails = [], [], []
    for e in range(n_e):
        base = (b * n_e + e) * (nch + 1) + c
        s_lo = cs_ref[base]
        s_hi = cs_ref[base + 1]
        w0 = pl.multiple_of(jnp.minimum((s_lo >> SLOT_SHIFT) << SLOT_SHIFT, cap - win), SLOT_BLOCK)
        hits.append((pos_all[:, e:e + 1] == (w0 + win_ids)).astype(BF16))
        wins.append(yo_ref[0, e, pl.ds(w0, win), :])
        sb_end = jnp.where(s_hi > s_lo, ((s_hi - 1) >> SLOT_SHIFT) + 1, 0)
        tails.append(((w0 + win) >> SLOT_SHIFT, sb_end))
    y_ref[0] = jnp.dot(jnp.concatenate(hits, axis=1), jnp.concatenate(wins, axis=0), preferred_element_type=F32)
    for e, (sb_from, sb_end) in enumerate(tails):
        def body(sb, carry, e=e):
            blk_ids = lax.broadcasted_iota(jnp.int32, (TOKEN_CHUNK, SLOT_BLOCK), 1)
            hit = post_ref[0, :, e:e + 1] == (sb * SLOT_BLOCK + blk_ids)
            yb = yo_ref[0, e, pl.ds(pl.multiple_of(sb * SLOT_BLOCK, SLOT_BLOCK), SLOT_BLOCK), :]
            y_ref[0] += jnp.dot(hit.astype(BF16), yb, preferred_element_type=F32)
            return carry

        lax.fori_loop(sb_from, sb_end, body, 0)


def _combine(yo, pos_t, cs, td):
    batch, n_e, cap, d = yo.shape
    seq = pos_t.shape[1]
    nch = seq // TOKEN_CHUNK
    grid_spec = pltpu.PrefetchScalarGridSpec(
        num_scalar_prefetch=1,
        grid=(batch, d // td, nch),
        in_specs=[pl.BlockSpec((1, n_e, cap, td), lambda b, j, c, *_: (b, 0, 0, j)),
                  pl.BlockSpec((1, TOKEN_CHUNK, n_e), lambda b, j, c, *_: (b, c, 0))],
        out_specs=pl.BlockSpec((1, TOKEN_CHUNK, td), lambda b, j, c, *_: (b, c, j)),
    )
    return pl.pallas_call(
        functools.partial(_combine_kernel, n_e=n_e, nch=nch, cap=cap),
        grid_spec=grid_spec,
        out_shape=jax.ShapeDtypeStruct((batch, seq, d), F32),
        compiler_params=_cparams(("arbitrary", "arbitrary", "arbitrary"), 56),
        name="ec_combine",
    )(cs, yo, pos_t)


def _ple_kernel(x1_ref, y_ref, p_ref, wpg_ref, wpp_ref, ln_ref, x3_ref, x3b_ref, *, alpha):
    x2 = _layer_norm(alpha * x1_ref[...] + y_ref[...], ln_ref[0:1, :], ln_ref[1:2, :])
    gate = 1.0 / (1.0 + jnp.exp(-jnp.dot(x2.astype(BF16), wpg_ref[...], preferred_element_type=F32)))
    emb = jnp.dot(p_ref[...].astype(BF16), wpp_ref[...], preferred_element_type=F32) * gate
    x3 = _layer_norm(alpha * x2 + emb, ln_ref[2:3, :], ln_ref[3:4, :])
    x3_ref[...] = x3
    x3b_ref[...] = x3.astype(BF16)


def _ple(x1, y, p, w_pg, w_pp, ln, alpha, tm):
    n, d = x1.shape
    row = lambda w: pl.BlockSpec((tm, w), lambda i: (i, 0))
    full = lambda a: pl.BlockSpec(a.shape, lambda i: (0,) * a.ndim)
    return pl.pallas_call(
        functools.partial(_ple_kernel, alpha=alpha),
        grid=(n // tm,),
        in_specs=[row(d), row(d), row(p.shape[1]), full(w_pg), full(w_pp), full(ln)],
        out_specs=[row(d), row(d)],
        out_shape=[jax.ShapeDtypeStruct((n, d), F32), jax.ShapeDtypeStruct((n, d), BF16)],
        compiler_params=_cparams(("arbitrary",), 40),
        name="ple_norm",
    )(x1, y, p, w_pg, w_pp, ln)


def _token_mixer_inputs(xb, w_in, qk_gain, batch, seq):
    d = w_in.shape[0]
    scale = HEAD_DIM ** -0.5
    na_w = NA_HEADS * HEAD_DIM
    gq_w, gkv_w = GQA_HEADS * HEAD_DIM, GQA_KV_HEADS * HEAD_DIM
    dil_w = DIL_HEADS * HEAD_DIM
    dqk_w = DIFF_HEADS * 2 * HEAD_DIM
    o = 0
    w_a = w_in[:, o:o + 3 * na_w]; o += 3 * na_w
    w_b = w_in[:, o:o + gq_w + 2 * gkv_w]; o += gq_w + 2 * gkv_w
    w_c = w_in[:, o:o + 3 * dil_w]; o += 3 * dil_w
    w_d = w_in[:, o:o + 3 * dqk_w]; o += 3 * dqk_w
    w_g = w_in[:, o:]

    def qscaled(w, qw):
        return jnp.concatenate([w[:, :qw] * (scale * LOG2E), w[:, qw:]], axis=1).astype(BF16)

    n = batch * seq
    ha = _matmul(xb, qscaled(w_a, na_w), BF16, 1024, 3 * na_w, "proj_na").reshape(batch, seq, -1)
    w_cg = jnp.concatenate(
        [w_c[:, t * dil_w + LANES * g:t * dil_w + LANES * (g + 1)] * (scale if t == 0 else 1.0)
         for g in range(len(DIL_GROUPS)) for t in range(3)], axis=1).astype(BF16)
    hc = _matmul(xb, w_cg, BF16, 1024, 3 * dil_w, "proj_dil").reshape(batch, seq, -1)
    w_dq = jnp.concatenate([w_d[:, :dqk_w] * (scale * LOG2E), w_d[:, dqk_w:]], axis=1).astype(BF16)
    hd, hd_stats = _proj_diff(xb, w_dq, 1024)
    hd = hd.reshape(batch, seq, -1)
    qn, kd, vd = _proj_gqa(xb, w_b, qk_gain, seq, 512)
    shape3 = lambda t: t.reshape(batch, seq, t.shape[-1])
    return ha, shape3(qn), shape3(kd), shape3(vd), hc, hd, hd_stats, w_g.astype(BF16)


def _layer(x, xb, p, w, lam_init, alpha):
    batch, seq, d = x.shape
    n = batch * seq
    ha, qn, kd, vd, hc, hd, hd_stats, w_g = _token_mixer_inputs(xb.reshape(n, d), w["w_in"], w["qk_gain"], batch, seq)

    ya = _na_attention(ha, _na_bias_tiles(w["rpb"]), batch, seq)
    yb = lax.cond(_gqa_logit_bound(w["qk_gain"]) <= ATTN_LOGIT_LIMIT,
                  lambda *a: _gqa_attention(*a, ATTN_TQ, GQA_TK_BOUNDED, False),
                  lambda *a: _gqa_attention(*a, ATTN_TQ, ATTN_TK, True), qn, kd, vd)
    dil = []
    for g, (_, dilation) in enumerate(DIL_GROUPS):
        dil.append(_dilated_group(hc, w["dil_bias"][g], g, dilation, batch, seq, DIL_TQ, DIL_BLOCKS_PER_STEP))
    lp = w["lam_params"].astype(F32)
    lam = jnp.exp(jnp.sum(lp[0] * lp[1])) - jnp.exp(jnp.sum(lp[2] * lp[3])) + lam_init
    consts = jnp.zeros((8, LANES), F32).at[0].set(lam).at[1].set(w["diff_gain"]).at[2].set(1.0 - lam_init)
    yd = _diff_mixer(hd, hd_stats, w["t5"][:, DIL_HEADS:], w["diff_strips"], consts)

    flat = lambda t: t.reshape(n, t.shape[-1])
    w_r3 = jnp.stack(_split3(w["w_router"].T))
    ln = w["ln"]
    x1, x1b, logits_t = _merge(
        x.reshape(n, d), xb.reshape(n, d), flat(ya), flat(yb), [(flat(o), flat(l)) for o, l in dil], flat(yd), w_g,
        w["w_na"].astype(BF16), w["w_gqa"].astype(BF16), w["w_dil"].astype(BF16), w["w_diff"].astype(BF16),
        w["w_o"].astype(BF16), w_r3, jnp.stack([ln[0][0], ln[1][0]]), alpha, 512)

    cap = EC_CAPACITY * seq // N_EXPERTS
    aff, posm, posx = _route(logits_t, batch, seq, cap)
    nch = seq // TOKEN_CHUNK
    cs = jnp.concatenate([posx[:, :, ::TOKEN_CHUNK], jnp.full((batch, N_EXPERTS, 1), cap, jnp.int32)], axis=-1)
    edges = jnp.arange(cap // SLOT_BLOCK, dtype=jnp.int32) * SLOT_BLOCK
    clo = jnp.sum(cs[:, :, 1:, None] <= edges, axis=2).astype(jnp.int32)
    chi = jnp.sum(cs[:, :, :nch, None] < edges + SLOT_BLOCK, axis=2).astype(jnp.int32)
    xin, gslot = _gather(x1b.reshape(batch, seq, d), posm, aff, clo.reshape(-1), chi.reshape(-1), cap)
    yo = _expert_ffn(xin, gslot, w["w_eg"], w["w_eu"], w["w_ed"], w["layer"], 512, 512)
    y = _combine(yo, posm.transpose(0, 2, 1), cs.reshape(-1), 512)

    ln4 = jnp.stack([ln[0][1], ln[1][1], ln[0][2], ln[1][2]])
    x3, x3b = _ple(x1, y.reshape(n, d), p.reshape(n, -1), w["w_pg"].astype(BF16), w["w_pp"].astype(BF16), ln4, alpha, 512)
    return x3.reshape(batch, seq, d), x3b.reshape(batch, seq, d)


def kernel(x, p, w_in, w_branch_na, w_branch_gqa, w_branch_dil, w_branch_diff, w_out, na_rel_bias, qk_norm_gain,
           diff_lambda, diff_norm_gain, t5_rel_bias, w_router, w_expert_gate, w_expert_up, w_expert_down,
           w_ple_proj, w_ple_gate, ln_gain, ln_bias):
    depth = w_in.shape[0]
    alpha = (2 * depth) ** 0.25
    xb = x.astype(BF16)
    w_eg, w_eu, w_ed = (t.astype(BF16) for t in (w_expert_gate, w_expert_up, w_expert_down))
    diff_strips = _diff_bias_strips(t5_rel_bias[:, DIL_HEADS:], DIFF_BIAS_ROWS, DIFF_TK_BOUNDED)
    dil_bias = [_dil_bias_tiles(t5_rel_bias[:, 2 * g:2 * g + 2], dilation, DIL_TQ)
                for g, (_, dilation) in enumerate(DIL_GROUPS)]
    for i in range(depth):
        lam_init = 0.8 - 0.6 * math.exp(-0.3 * i)
        w = dict(diff_strips=diff_strips, dil_bias=dil_bias, w_in=w_in[i], w_na=w_branch_na[i], w_gqa=w_branch_gqa[i], w_dil=w_branch_dil[i],
                 w_diff=w_branch_diff[i], w_o=w_out[i], rpb=na_rel_bias[i], qk_gain=qk_norm_gain[i],
                 lam_params=diff_lambda[i], diff_gain=diff_norm_gain[i], t5=t5_rel_bias, w_router=w_router[i],
                 w_eg=w_eg, w_eu=w_eu, w_ed=w_ed, layer=i, w_pp=w_ple_proj[i],
                 w_pg=w_ple_gate[i], ln=(ln_gain[i], ln_bias[i]))
        x, xb = _layer(x, xb, p[i], w, lam_init, alpha)
    return x
```

```python
import functools
import math

import numpy as np
import jax
import jax.numpy as jnp
from jax import lax
from jax.experimental import pallas as pl
from jax.experimental.pallas import tpu as pltpu

F32 = jnp.float32
BF16 = jnp.bfloat16
NEG_INF = -1e30

V7X_VMEM_BYTES = 64 * 1024 * 1024
LANES = 128
LANE_SHIFT = 7

GRID_W = 64
HEAD_DIM = 64
NA_HEADS, NA_ROWS, NA_COLS = 6, 8, 16
NA_ROWS_PER_STEP = 4
LOG2E = 1.4426950408889634
ATTN_TQ, ATTN_TK = 1024, 256
GQA_TK_BOUNDED, DIFF_TK_BOUNDED = 1024, 1024
ATTN_LOGIT_LIMIT = 64.0
DIL_TQ = 128
DIL_BLOCKS_PER_STEP = 4
DIFF_BIAS_ROWS = 256
GQA_HEADS, GQA_KV_HEADS = 6, 2
ROPE_THETA = 10000.0
DIL_GROUPS = ((128, 1), (512, 4), (2048, 16))
DIL_HEADS = 6
DIFF_HEADS = 4
T5_BUCKETS, T5_MAX_DISTANCE = 32, 1024
N_EXPERTS, EC_CAPACITY = 16, 2
SLOT_BLOCK = 128
SLOT_SHIFT = SLOT_BLOCK.bit_length() - 1
TOKEN_CHUNK = 256


def _cparams(semantics, vmem_mib):
    return pltpu.CompilerParams(dimension_semantics=semantics,
                                vmem_limit_bytes=min(vmem_mib * 1024 * 1024, V7X_VMEM_BYTES - 4 * 1024 * 1024))


def _mm_kernel(a_ref, b_ref, o_ref):
    o_ref[...] = jnp.dot(a_ref[...], b_ref[...], preferred_element_type=F32).astype(o_ref.dtype)


def _matmul(a, b, out_dtype, tm, tn, name):
    m, k = a.shape
    n = b.shape[1]
    return pl.pallas_call(
        _mm_kernel,
        grid=(n // tn, m // tm),
        in_specs=[pl.BlockSpec((tm, k), lambda j, i: (i, 0)),
                  pl.BlockSpec((k, tn), lambda j, i: (0, j))],
        out_specs=pl.BlockSpec((tm, tn), lambda j, i: (i, j)),
        out_shape=jax.ShapeDtypeStruct((m, n), out_dtype),
        compiler_params=_cparams(("arbitrary", "arbitrary"), 40),
        name=name,
    )(a, b)


def _toeplitz(u, rows, width):
    lead = u.shape[:-1]
    period = rows + width
    text = jnp.concatenate([u[..., rows - 1:], jnp.zeros(lead + (1,), u.dtype), u[..., :rows - 1]], axis=-1)
    flat = jnp.tile(text, (1,) * len(lead) + (rows,))[..., :rows * (period - 1)]
    return flat.reshape(lead + (rows, period - 1))[..., :width]


def _lane_lo(rows):
    return lax.broadcasted_iota(jnp.int32, (rows, LANES), 1) < HEAD_DIM


def _dot_nt(a, b):
    return lax.dot_general(a, b, (((1,), (1,)), ((), ())), preferred_element_type=F32)


NA_WIN_ROWS = NA_ROWS + NA_ROWS_PER_STEP


def _na_kernel(q_ref, k_ref, v_ref, tp_ref, o_ref, *, rows):
    r0 = pl.program_id(1) * NA_ROWS_PER_STEP
    ws = jnp.clip(r0 - NA_ROWS // 2, 0, rows - NA_WIN_ROWS)
    start = pl.multiple_of(ws * GRID_W, GRID_W)
    win = NA_WIN_ROWS * GRID_W
    npair = NA_WIN_ROWS // 2
    none_code = 2 * (2 * NA_ROWS - 1)
    codes = []
    for rr in range(NA_ROWS_PER_STEP):
        r = r0 + rr
        rs = jnp.clip(r - NA_ROWS // 2, 0, rows - NA_ROWS)
        row_codes = []
        for j in range(npair):
            a0 = ws + 2 * j
            d0 = a0 - r + (NA_ROWS - 1)
            vl = jnp.logical_and(a0 >= rs, a0 < rs + NA_ROWS)
            vr = jnp.logical_and(a0 + 1 >= rs, a0 + 1 < rs + NA_ROWS)
            row_codes.append(jnp.where(jnp.logical_and(vl, vr), d0,
                                       jnp.where(jnp.logical_or(vl, vr), 2 * NA_ROWS - 1 + d0, none_code)))
        codes.append(row_codes)
    tq = NA_ROWS_PER_STEP * GRID_W
    lo = _lane_lo(tq)
    for p in range(NA_HEADS // 2):
        cols = slice(LANES * p, LANES * (p + 1))
        qp = q_ref[0, :, cols]
        kw = k_ref[0, pl.ds(start, win), cols]
        vw = v_ref[0, pl.ds(start, win), cols]
        outs = []
        for hh in range(2):
            h = 2 * p + hh
            qm = jnp.where(lo if hh == 0 else jnp.logical_not(lo), qp, jnp.zeros_like(qp))
            bias = jnp.concatenate(
                [jnp.concatenate([tp_ref[h, c] for c in row_codes], axis=-1) for row_codes in codes], axis=0)
            s = _dot_nt(qm, kw) + bias
            e = jnp.exp2(s - jnp.max(s, axis=-1, keepdims=True))
            pv = jnp.dot(e.astype(BF16), vw, preferred_element_type=F32)
            outs.append(pv / jnp.sum(e, axis=-1, keepdims=True))
        o_ref[0, :, cols] = jnp.where(lo, outs[0], outs[1]).astype(o_ref.dtype)


def _na_bias_tiles(rpb):
    col = jnp.arange(GRID_W)
    cstart = jnp.clip(col - NA_COLS // 2, 0, GRID_W - NA_COLS)
    kc = jnp.arange(GRID_W)
    inwin = (kc[None, :] >= cstart[:, None]) & (kc[None, :] < cstart[:, None] + NA_COLS)
    pad = GRID_W - NA_COLS
    u = jnp.pad(rpb.astype(F32) * LOG2E, ((0, 0), (0, 0), (pad, pad)))
    t = jnp.where(inwin[None, None], _toeplitz(u, GRID_W, GRID_W), NEG_INF)
    neg = jnp.full_like(t[:, :NA_ROWS], NEG_INF)
    both = jnp.concatenate([t[:, :-1], t[:, 1:]], axis=-1)
    right = jnp.concatenate([neg, t[:, :NA_ROWS]], axis=-1)
    left = jnp.concatenate([t[:, NA_ROWS - 1:], neg], axis=-1)
    none = jnp.concatenate([neg[:, :1], neg[:, :1]], axis=-1)
    return jnp.concatenate([both, right, left, none], axis=1)


def _na_attention(ha, t2, batch, seq):
    rows = seq // GRID_W
    w = NA_HEADS * HEAD_DIM
    tq = GRID_W * NA_ROWS_PER_STEP
    return pl.pallas_call(
        functools.partial(_na_kernel, rows=rows),
        grid=(batch, rows // NA_ROWS_PER_STEP),
        in_specs=[pl.BlockSpec((1, tq, w), lambda b, r: (b, r, 0)),
                  pl.BlockSpec((1, seq, w), lambda b, r: (b, 0, 1)),
                  pl.BlockSpec((1, seq, w), lambda b, r: (b, 0, 2)),
                  pl.BlockSpec(t2.shape, lambda b, r: (0, 0, 0, 0))],
        out_specs=pl.BlockSpec((1, tq, w), lambda b, r: (b, r, 0)),
        out_shape=jax.ShapeDtypeStruct((batch, seq, w), BF16),
        compiler_params=_cparams(("arbitrary", "arbitrary"), 56),
        name="na_attention",
    )(ha, ha, ha, t2)


def _softmax_stream_update(s, vb, m_ref, l_ref, acc_ref, idx):
    m_prev = m_ref[idx]
    m_new = jnp.maximum(m_prev, jnp.max(s, axis=-1, keepdims=True))
    alpha = jnp.exp2(m_prev - m_new)
    p = jnp.exp2(s - jnp.tile(m_new, (1, s.shape[1] // LANES)))
    m_ref[idx] = m_new
    l_ref[idx] = alpha * l_ref[idx] + jnp.sum(p, axis=-1, keepdims=True)
    acc_ref[idx] = alpha * acc_ref[idx] + jnp.dot(p.astype(BF16), vb, preferred_element_type=F32)


def _init_streams(m_ref, l_ref, acc_ref):
    m_ref[...] = jnp.full(m_ref.shape, NEG_INF, F32)
    l_ref[...] = jnp.zeros(l_ref.shape, F32)
    acc_ref[...] = jnp.zeros(acc_ref.shape, F32)


def _gqa_kernel(q_ref, k_ref, v_ref, o_ref, qm_ref, m_ref, l_ref, acc_ref, *, tk, nk, track_max):
    tq = q_ref.shape[1]
    lo = _lane_lo(tq)
    rep = GQA_HEADS // GQA_KV_HEADS
    for h in range(GQA_HEADS):
        qp = q_ref[0, :, LANES * (h // 2):LANES * (h // 2 + 1)]
        qm_ref[h] = jnp.where(lo if h % 2 == 0 else jnp.logical_not(lo), qp, jnp.zeros_like(qp))
    _init_streams(m_ref, l_ref, acc_ref)

    def body(j, carry):
        ks = pl.multiple_of(j * tk, tk)
        for g in range(GQA_KV_HEADS):
            gcols = slice(LANES * g, LANES * (g + 1))
            kb = k_ref[0, pl.ds(ks, tk), gcols]
            vb = v_ref[0, pl.ds(ks, tk), gcols]
            for h in range(rep * g, rep * (g + 1)):
                s = _dot_nt(qm_ref[h], kb)
                if track_max:
                    _softmax_stream_update(s, vb, m_ref, l_ref, acc_ref, h)
                else:
                    p = jnp.exp2(s)
                    l_ref[h] += jnp.sum(p, axis=-1, keepdims=True)
                    acc_ref[h] += jnp.dot(p.astype(BF16), vb, preferred_element_type=F32)
        return carry

    lax.fori_loop(0, nk, body, 0)
    for p in range(GQA_HEADS // 2):
        o0 = acc_ref[2 * p] / l_ref[2 * p]
        o1 = acc_ref[2 * p + 1] / l_ref[2 * p + 1]
        o_ref[0, :, LANES * p:LANES * (p + 1)] = jnp.where(lo, o0, o1).astype(o_ref.dtype)


def _gqa_attention(q, kd, vd, tq, tk, track_max):
    batch, seq, wq = q.shape
    wk = kd.shape[-1]
    state = pltpu.VMEM((GQA_HEADS, tq, LANES), F32)
    return pl.pallas_call(
        functools.partial(_gqa_kernel, tk=tk, nk=seq // tk, track_max=track_max),
        grid=(batch, seq // tq),
        in_specs=[pl.BlockSpec((1, tq, wq), lambda b, i: (b, i, 0)),
                  pl.BlockSpec((1, seq, wk), lambda b, i: (b, 0, 0)),
                  pl.BlockSpec((1, seq, wk), lambda b, i: (b, 0, 0))],
        out_specs=pl.BlockSpec((1, tq, wq), lambda b, i: (b, i, 0)),
        out_shape=jax.ShapeDtypeStruct((batch, seq, wq), BF16),
        scratch_shapes=[pltpu.VMEM((GQA_HEADS, tq, LANES), BF16), state, state, state],
        compiler_params=_cparams(("arbitrary", "arbitrary"), 48),
        name="gqa_attention" if track_max else "gqa_attention_bounded",
    )(q, kd, vd)


def _gqa_logit_bound(qk_gain):
    gq = jnp.max(jnp.abs(qk_gain[0])) * (HEAD_DIM ** -0.5 * LOG2E)
    return 1.05 * HEAD_DIM * gq * jnp.max(jnp.abs(qk_gain[1]))


def _rope_tables(seq):
    quarter = HEAD_DIM // 4
    freqs = ROPE_THETA ** (-jnp.arange(quarter, dtype=F32) / quarter)
    t = jnp.arange(seq)
    row = (t // GRID_W).astype(F32)
    col = (t % GRID_W).astype(F32)
    ang_r = row[:, None] * freqs[None, :]
    ang_c = col[:, None] * freqs[None, :]
    cos = jnp.concatenate([jnp.cos(ang_r)] * 2 + [jnp.cos(ang_c)] * 2, axis=-1)
    sin = jnp.concatenate([-jnp.sin(ang_r), jnp.sin(ang_r), -jnp.sin(ang_c), jnp.sin(ang_c)], axis=-1)
    return cos, sin


def _proj_gqa_kernel(x_ref, w_ref, cos_ref, sin_ref, gain_ref, mean_ref, q_ref, k_ref, v_ref):
    h = jnp.dot(x_ref[...], w_ref[...], preferred_element_type=F32)
    nq = q_ref.shape[1] // LANES
    nk = k_ref.shape[1] // LANES
    lane = lax.broadcasted_iota(jnp.int32, (h.shape[0], LANES), 1)
    first = (lane & (HEAD_DIM // 2 - 1)) < HEAD_DIM // 4
    for blk in range(nq + nk):
        t = h[:, LANES * blk:LANES * (blk + 1)]
        hi, mid, low = _split3(t * t)
        ms = (jnp.dot(hi, mean_ref[...], preferred_element_type=F32)
              + jnp.dot(mid, mean_ref[...], preferred_element_type=F32)
              + jnp.dot(low, mean_ref[...], preferred_element_type=F32))
        y = t * lax.rsqrt(ms + 1e-6) * gain_ref[blk:blk + 1, :]
        partner = jnp.where(first, pltpu.roll(y, LANES - HEAD_DIM // 4, 1), pltpu.roll(y, HEAD_DIM // 4, 1))
        out = (y * cos_ref[...] + partner * sin_ref[...]).astype(BF16)
        if blk < nq:
            q_ref[:, LANES * blk:LANES * (blk + 1)] = out
        else:
            k_ref[:, LANES * (blk - nq):LANES * (blk - nq + 1)] = out
    v_ref[...] = h[:, LANES * (nq + nk):].astype(BF16)


def _proj_gqa(xb, w_b, qk_gain, seq, tm):
    n, d = xb.shape
    gq, gkv = GQA_HEADS * HEAD_DIM, GQA_KV_HEADS * HEAD_DIM
    dup = lambda w: jnp.concatenate([w[:, i * HEAD_DIM:(i + 1) * HEAD_DIM] for i in range(GQA_KV_HEADS) for _ in range(2)], axis=1)
    w_all = jnp.concatenate([w_b[:, :gq], dup(w_b[:, gq:gq + gkv]), dup(w_b[:, gq + gkv:])], axis=1).astype(BF16)
    cos, sin = _rope_tables(seq)
    cos2, sin2 = jnp.tile(cos, (1, 2)), jnp.tile(sin, (1, 2))
    nq, nk = gq // LANES, 2 * gkv // LANES
    gain = jnp.concatenate([jnp.tile(qk_gain[0] * (HEAD_DIM ** -0.5 * LOG2E), (nq, 2)), jnp.tile(qk_gain[1], (nk, 2)),
                            jnp.zeros((8 - nq - nk, LANES), F32)], axis=0)
    head = jnp.arange(LANES) // HEAD_DIM
    mean_mat = jnp.where(head[:, None] == head[None, :], 1.0 / HEAD_DIM, 0.0).astype(BF16)
    nt = seq // tm
    row = lambda w: pl.BlockSpec((tm, w), lambda i: (i, 0))
    full = lambda a: pl.BlockSpec(a.shape, lambda i: (0,) * a.ndim)
    pos = pl.BlockSpec((tm, LANES), lambda i: (i % nt, 0))
    return pl.pallas_call(
        _proj_gqa_kernel,
        grid=(n // tm,),
        in_specs=[row(d), full(w_all), pos, pos, full(gain), full(mean_mat)],
        out_specs=[row(gq), row(2 * gkv), row(2 * gkv)],
        out_shape=[jax.ShapeDtypeStruct((n, gq), BF16), jax.ShapeDtypeStruct((n, 2 * gkv), BF16),
                   jax.ShapeDtypeStruct((n, 2 * gkv), BF16)],
        compiler_params=_cparams(("arbitrary",), 40),
        name="proj_gqa",
    )(xb, w_all, cos2, sin2, gain, mean_mat)


def _dil_kernel(q_ref, k_ref, v_ref, bias_ref, o_ref, lse_ref, *, sub_len, nblk, tq):
    half = DIL_GROUPS[0][0] // 2
    win = tq + 2 * half
    lo = _lane_lo(tq)
    for qq in range(q_ref.shape[1] // tq):
        i = pl.program_id(2) * (q_ref.shape[1] // tq) + qq
        start = pl.multiple_of(jnp.clip(i * tq - half, 0, sub_len - win), half)
        var = jnp.where(i == 0, 0, jnp.where(i == nblk - 1, 2, 1))
        rows = slice(qq * tq, (qq + 1) * tq)
        qp = q_ref[0, rows, :]
        kw = k_ref[0, pl.ds(start, win), :]
        vw = v_ref[0, pl.ds(start, win), :]
        outs, lses = [], []
        for hh in range(2):
            qm = jnp.where(lo if hh == 0 else jnp.logical_not(lo), qp, jnp.zeros_like(qp))
            s = _dot_nt(qm, kw) + bias_ref[hh, var]
            m = jnp.max(s, axis=-1, keepdims=True)
            e = jnp.exp(s - m)
            l = jnp.sum(e, axis=-1, keepdims=True)
            outs.append(jnp.dot(e.astype(BF16), vw, preferred_element_type=F32) / l)
            lses.append(jnp.broadcast_to(m + jnp.log(l), (tq, LANES)))
        o_ref[0, rows, :] = jnp.where(lo, outs[0], outs[1])
        lse_ref[0, rows, :] = jnp.where(lo, lses[0], lses[1])


def _t5_bucket(rel):
    half = T5_BUCKETS // 2
    exact = half // 2
    n = jnp.abs(rel)
    nf = jnp.maximum(n, 1).astype(F32)
    large = exact + (jnp.log(nf / exact) / math.log(T5_MAX_DISTANCE / exact) * (half - exact)).astype(jnp.int32)
    large = jnp.minimum(large, half - 1)
    return jnp.where(rel > 0, half, 0) + jnp.where(n < exact, n, large)


def _dil_bias_tiles(t5_cols, dilation, tq):
    half = DIL_GROUPS[0][0] // 2
    win = tq + 2 * half
    wide = win + 2 * half
    j = jnp.arange(tq - 1 + wide) - (tq - 1) - 2 * half
    vals = t5_cols.astype(F32)[_t5_bucket(j * dilation)]
    u = jnp.where((jnp.abs(j) <= half)[:, None], vals, NEG_INF).T
    a = _toeplitz(u, tq, wide)
    return jnp.stack([a[:, :, 2 * half - s:2 * half - s + win] for s in (0, half, 2 * half)], axis=1)


def _dilated_group(hc, bias, g, dilation, batch, seq, tq, qb):
    sub_len = seq // dilation
    nblk = sub_len // tq
    ts = tq * qb
    if dilation == 1:
        view, nb, g = hc, hc.shape[-1] // LANES, 3 * g
    else:
        view, nb, g = hc[..., 3 * LANES * g:3 * LANES * (g + 1)].reshape(batch, sub_len, dilation * 3 * LANES), 3, 0
    nq = 1
    out, lse = pl.pallas_call(
        functools.partial(_dil_kernel, sub_len=sub_len, nblk=nblk, tq=tq),
        grid=(batch, dilation, nblk // qb),
        in_specs=[pl.BlockSpec((1, ts, LANES), lambda b, r, i: (b, i, r * nb + g)),
                  pl.BlockSpec((1, sub_len, LANES), lambda b, r, i: (b, 0, r * nb + nq + g)),
                  pl.BlockSpec((1, sub_len, LANES), lambda b, r, i: (b, 0, r * nb + 2 * nq + g)),
                  pl.BlockSpec(bias.shape, lambda b, r, i: (0, 0, 0, 0))],
        out_specs=[pl.BlockSpec((1, ts, LANES), lambda b, r, i: (b, i, r)),
                   pl.BlockSpec((1, ts, LANES), lambda b, r, i: (b, i, r))],
        out_shape=[jax.ShapeDtypeStruct((batch, sub_len, dilation * LANES), F32)] * 2,
        compiler_params=_cparams(("arbitrary", "arbitrary", "arbitrary"), 32),
        name=f"dilated_attention_d{dilation}",
    )(view, view, view, bias)
    return out.reshape(batch, seq, LANES), lse.reshape(batch, seq, LANES)


def _diff_kernel(q_ref, k_ref, v_ref, strip_ref, c_ref, o_ref, qm_ref, m_ref, l_ref, acc_ref, *,
                 tk, nk, delta_lo, delta_hi, track_max):
    tq = q_ref.shape[1]
    i = pl.program_id(1)
    lo = _lane_lo(tq)
    for h in range(DIFF_HEADS):
        qp = q_ref[0, :, LANES * h:LANES * (h + 1)]
        qm_ref[2 * h] = jnp.where(lo, qp, jnp.zeros_like(qp))
        qm_ref[2 * h + 1] = jnp.where(lo, jnp.zeros_like(qp), qp)
    _init_streams(m_ref, l_ref, acc_ref)

    def body(j, carry):
        ks = pl.multiple_of(j * tk, tk)
        sub = strip_ref.shape[2]
        blk0 = [(jnp.clip(j * tk - i * tq - a * sub, delta_lo, delta_hi) - delta_lo) >> LANE_SHIFT
                for a in range(tq // sub)]
        for h in range(DIFF_HEADS):
            cols = slice(LANES * h, LANES * (h + 1))
            kb = k_ref[0, pl.ds(ks, tk), cols]
            vb = v_ref[0, pl.ds(ks, tk), cols]
            bias = jnp.concatenate(
                [jnp.concatenate([strip_ref[h, b0 + c] for c in range(tk // LANES)], axis=-1) for b0 in blk0], axis=0)
            for t in range(2):
                s = _dot_nt(qm_ref[2 * h + t], kb) + bias
                if track_max:
                    _softmax_stream_update(s, vb, m_ref, l_ref, acc_ref, 2 * h + t)
                else:
                    p = jnp.exp2(s)
                    l_ref[2 * h + t] += jnp.sum(p, axis=-1, keepdims=True)
                    acc_ref[2 * h + t] += jnp.dot(p.astype(BF16), vb, preferred_element_type=F32)
        return carry

    lax.fori_loop(0, nk, body, 0)
    lam = c_ref[0:1, :]
    for h in range(DIFF_HEADS):
        out = acc_ref[2 * h] / l_ref[2 * h] - lam * (acc_ref[2 * h + 1] / l_ref[2 * h + 1])
        y = out * lax.rsqrt(jnp.mean(out * out, axis=-1, keepdims=True) + 1e-6)
        o_ref[0, :, LANES * h:LANES * (h + 1)] = ((y * c_ref[1:2, :]) * c_ref[2:3, :]).astype(o_ref.dtype)


def _t5_saturation_distance():
    half = T5_BUCKETS // 2
    exact = half // 2
    n = np.arange(1, 4 * T5_MAX_DISTANCE, dtype=np.float32)
    large = exact + (np.log(n / exact) / math.log(T5_MAX_DISTANCE / exact) * (half - exact)).astype(np.int32)
    return int(np.argmax(large >= half - 1)) + 1 + 8


def _diff_bias_strips(t5_cols, tq, tk):
    unit = math.gcd(tq, tk)
    sat = _t5_saturation_distance()
    delta_lo = -unit * ((sat + tk - 1) // unit + 1)
    delta_hi = unit * ((sat + tq - 1) // unit + 1)
    width = delta_hi - delta_lo + tk
    rel = jnp.arange(tq - 1 + width) - (tq - 1) + delta_lo
    u = (t5_cols.astype(F32) * LOG2E)[_t5_bucket(rel)].T
    strip = _toeplitz(u, tq, width)
    nh = strip.shape[0]
    return strip.reshape(nh, tq, width // LANES, LANES).transpose(0, 2, 1, 3), delta_lo, delta_hi


def _diff_attention(hd, strips, consts, delta_lo, delta_hi, tq, tk, track_max):
    batch, seq = hd.shape[:2]
    w = DIFF_HEADS * LANES
    once = pl.Buffered(1)
    state = pltpu.VMEM((2 * DIFF_HEADS, tq, LANES), F32)
    return pl.pallas_call(
        functools.partial(_diff_kernel, tk=tk, nk=seq // tk, delta_lo=delta_lo, delta_hi=delta_hi,
                          track_max=track_max),
        grid=(batch, seq // tq),
        in_specs=[pl.BlockSpec((1, tq, w), lambda b, i: (b, i, 0)),
                  pl.BlockSpec((1, seq, w), lambda b, i: (b, 0, 1), pipeline_mode=once),
                  pl.BlockSpec((1, seq, w), lambda b, i: (b, 0, 2), pipeline_mode=once),
                  pl.BlockSpec(strips.shape, lambda b, i: (0, 0, 0, 0), pipeline_mode=once),
                  pl.BlockSpec(consts.shape, lambda b, i: (0, 0))],
        out_specs=pl.BlockSpec((1, tq, w), lambda b, i: (b, i, 0)),
        out_shape=jax.ShapeDtypeStruct((batch, seq, w), BF16),
        scratch_shapes=[pltpu.VMEM((2 * DIFF_HEADS, tq, LANES), BF16), state, state, state],
        compiler_params=_cparams(("arbitrary", "arbitrary"), 56),
        name="diff_attention" if track_max else "diff_attention_bounded",
    )(hd, hd, hd, strips, consts)


def _proj_diff_kernel(x_ref, w_ref, ind_ref, o_ref, st_ref):
    hb = jnp.dot(x_ref[...], w_ref[...], preferred_element_type=F32).astype(BF16)
    o_ref[...] = hb
    qk = hb[:, :ind_ref.shape[0]].astype(F32)
    norms = jnp.dot((qk * qk).astype(BF16), ind_ref[...], preferred_element_type=F32)
    st_ref[0] = jnp.broadcast_to(jnp.max(norms, axis=0, keepdims=True), st_ref.shape[1:])


def _proj_diff(xb, w_dq, tm):
    n, d = xb.shape
    width = w_dq.shape[1]
    nqk = 2 * DIFF_HEADS * 2 * HEAD_DIM
    ind = (jnp.arange(nqk)[:, None] // HEAD_DIM == jnp.arange(LANES)[None, :]).astype(BF16)
    return pl.pallas_call(
        _proj_diff_kernel,
        grid=(n // tm,),
        in_specs=[pl.BlockSpec((tm, d), lambda i: (i, 0)), pl.BlockSpec(w_dq.shape, lambda i: (0, 0)),
                  pl.BlockSpec(ind.shape, lambda i: (0, 0))],
        out_specs=[pl.BlockSpec((tm, width), lambda i: (i, 0)), pl.BlockSpec((1, 8, LANES), lambda i: (i, 0, 0))],
        out_shape=[jax.ShapeDtypeStruct((n, width), BF16), jax.ShapeDtypeStruct((n // tm, 8, LANES), F32)],
        compiler_params=_cparams(("arbitrary",), 40),
        name="proj_diff",
    )(xb, w_dq, ind)


def _diff_logit_bound(norm_stats, t5_cols):
    n = DIFF_HEADS * 2
    mx = jnp.max(norm_stats, axis=(0, 1))
    return 1.05 * (jnp.sqrt(jnp.max(mx[:n] * mx[n:2 * n])) + LOG2E * jnp.max(jnp.abs(t5_cols)))


def _diff_mixer(hd, norm_stats, t5_cols, bounded_strips, consts):
    strips_b, lo_b, hi_b = bounded_strips

    def bounded(hd_, t5_, strips_, consts_):
        return _diff_attention(hd_, strips_, consts_, lo_b, hi_b, ATTN_TQ, DIFF_TK_BOUNDED, False)

    def tracked(hd_, t5_, strips_, consts_):
        strips, lo, hi = _diff_bias_strips(t5_, DIFF_BIAS_ROWS, ATTN_TK)
        return _diff_attention(hd_, strips, consts_, lo, hi, ATTN_TQ, ATTN_TK, True)

    return lax.cond(_diff_logit_bound(norm_stats, t5_cols) <= ATTN_LOGIT_LIMIT, bounded, tracked,
                    hd, t5_cols, strips_b, consts)


def _layer_norm(z, g, b):
    mu = jnp.mean(z, axis=-1, keepdims=True)
    zc = z - mu
    var = jnp.mean(zc * zc, axis=-1, keepdims=True)
    return zc * lax.rsqrt(var + 1e-5) * g + b


def _split3(a):
    hi = a.astype(BF16)
    r1 = a - hi.astype(F32)
    mid = r1.astype(BF16)
    lo = (r1 - mid.astype(F32)).astype(BF16)
    return hi, mid, lo


def _merge_kernel(x_ref, xb_ref, ya_ref, yb_ref, o0_ref, o1_ref, o2_ref, l0_ref, l1_ref, l2_ref, yd_ref,
                  wg_ref, wna_ref, wgqa_ref, wdil_ref, wdiff_ref, wo_ref, wr_ref, ln_ref,
                  x1_ref, x1b_ref, lg_ref, *, alpha, d_model):
    xb = xb_ref[...]

    def gate(branch):
        g = jnp.dot(xb, wg_ref[:, branch * d_model:(branch + 1) * d_model], preferred_element_type=F32)
        return 1.0 / (1.0 + jnp.exp(-g))

    l0, l1, l2 = l0_ref[...], l1_ref[...], l2_ref[...]
    mx = jnp.maximum(jnp.maximum(l0, l1), l2)
    e0, e1, e2 = jnp.exp(l0 - mx), jnp.exp(l1 - mx), jnp.exp(l2 - mx)
    den = e0 + e1 + e2
    yc = (e0 / den) * o0_ref[...] + (e1 / den) * o1_ref[...] + (e2 / den) * o2_ref[...]
    merged = gate(0) * jnp.dot(ya_ref[...], wna_ref[...], preferred_element_type=F32)
    merged += gate(1) * jnp.dot(yb_ref[...], wgqa_ref[...], preferred_element_type=F32)
    merged += gate(2) * jnp.dot(yc.astype(BF16), wdil_ref[...], preferred_element_type=F32)
    merged += gate(3) * jnp.dot(yd_ref[...], wdiff_ref[...], preferred_element_type=F32)
    y = jnp.dot(merged.astype(BF16), wo_ref[...], preferred_element_type=F32)
    x1 = _layer_norm(alpha * x_ref[...] + y, ln_ref[0:1, :], ln_ref[1:2, :])
    x1_ref[...] = x1
    x1b_ref[...] = x1.astype(BF16)
    xh, xm, _ = _split3(x1)
    wh, wm, _ = wr_ref[0], wr_ref[1], wr_ref[2]
    lg_ref[...] = _dot_nt(wh, xh) + (_dot_nt(wh, xm) + _dot_nt(wm, xh))


def _merge(x, xb, ya, yb, dil, yd, w_g, w_na, w_gqa, w_dil, w_diff, w_o, w_r3, ln, alpha, tm):
    n, d = x.shape
    e = w_r3.shape[1]
    (o0, l0), (o1, l1), (o2, l2) = dil
    row = lambda w: pl.BlockSpec((tm, w), lambda i: (i, 0))
    full = lambda a: pl.BlockSpec(a.shape, lambda i: (0,) * a.ndim)
    return pl.pallas_call(
        functools.partial(_merge_kernel, alpha=alpha, d_model=d),
        grid=(n // tm,),
        in_specs=[row(d), row(d), row(ya.shape[1]), row(yb.shape[1]), row(LANES), row(LANES), row(LANES),
                  row(LANES), row(LANES), row(LANES), row(yd.shape[1]),
                  full(w_g), full(w_na), full(w_gqa), full(w_dil), full(w_diff), full(w_o), full(w_r3), full(ln)],
        out_specs=[row(d), row(d), pl.BlockSpec((e, tm), lambda i: (0, i))],
        out_shape=[jax.ShapeDtypeStruct((n, d), F32), jax.ShapeDtypeStruct((n, d), BF16),
                   jax.ShapeDtypeStruct((e, n), F32)],
        compiler_params=_cparams(("arbitrary",), 56),
        name="branch_merge",
    )(x, xb, ya, yb, o0, o1, o2, l0, l1, l2, yd, w_g, w_na, w_gqa, w_dil, w_diff, w_o, w_r3, ln)


def _route_kernel(lg_ref, aff_ref, posm_ref, posx_ref, *, cap):
    lg = lg_ref[0]
    n_e, seq = lg.shape
    mx = jnp.max(lg, axis=0, keepdims=True)
    ex = jnp.exp(lg - mx)
    aff = ex / jnp.sum(ex, axis=0, keepdims=True)
    aff_ref[0] = aff
    bits = pltpu.bitcast(aff, jnp.int32)

    def bisect(_, carry):
        lo, hi = carry
        mid = lo + ((hi - lo + 1) >> 1)
        cnt = jnp.sum((bits >= mid).astype(F32), axis=1, keepdims=True)
        ok = cnt >= float(cap)
        return jnp.where(ok, mid, lo), jnp.where(ok, hi, mid - 1)

    lo0 = jnp.zeros((n_e, 1), jnp.int32)
    hi0 = jnp.full((n_e, 1), 0x7F800000, jnp.int32)
    thr, _ = lax.fori_loop(0, 31, bisect, (lo0, hi0))
    gt = bits > thr
    eq = bits == thr
    needf = float(cap) - jnp.sum(gt.astype(F32), axis=1, keepdims=True)

    tri = (lax.broadcasted_iota(jnp.int32, (LANES, LANES), 0)
           < lax.broadcasted_iota(jnp.int32, (LANES, LANES), 1)).astype(BF16)

    def prefix(mask_bf16, j, carry):
        blk = mask_bf16[:, j * LANES:(j + 1) * LANES]
        excl = jnp.dot(blk, tri, preferred_element_type=F32) + carry
        return excl, carry + jnp.sum(blk.astype(F32), axis=1, keepdims=True)

    eqb = eq.astype(BF16)
    carry_eq = jnp.zeros((n_e, 1), F32)
    carry_sel = jnp.zeros((n_e, 1), F32)
    for j in range(seq // LANES):
        cols = slice(j * LANES, (j + 1) * LANES)
        rank, carry_eq = prefix(eqb, j, carry_eq)
        sel = jnp.logical_or(gt[:, cols], jnp.logical_and(eq[:, cols], rank < needf))
        selb = sel.astype(BF16)
        excl = jnp.dot(selb, tri, preferred_element_type=F32) + carry_sel
        carry_sel = carry_sel + jnp.sum(selb.astype(F32), axis=1, keepdims=True)
        pos = excl.astype(jnp.int32)
        posx_ref[0, :, cols] = pos
        posm_ref[0, :, cols] = jnp.where(sel, pos, -1)


def _route(logits_t, batch, seq, cap):
    n_e = logits_t.shape[0]
    lg = logits_t.reshape(n_e, batch, seq).transpose(1, 0, 2)
    spec = pl.BlockSpec((1, n_e, seq), lambda b: (b, 0, 0))
    return pl.pallas_call(
        functools.partial(_route_kernel, cap=cap),
        grid=(batch,),
        in_specs=[spec],
        out_specs=[spec, spec, spec],
        out_shape=[jax.ShapeDtypeStruct((batch, n_e, seq), F32), jax.ShapeDtypeStruct((batch, n_e, seq), jnp.int32),
                   jax.ShapeDtypeStruct((batch, n_e, seq), jnp.int32)],
        compiler_params=_cparams(("arbitrary",), 32),
        name="ec_route",
    )(lg)


def _gather_kernel(clo_ref, chi_ref, x_ref, pos_ref, aff_ref, xin_ref, gs_ref, acc_ref, *, n_e, nsb):
    base = (pl.program_id(0) * n_e + pl.program_id(1)) * nsb
    for sb in range(nsb):
        slot_ids = sb * SLOT_BLOCK + lax.broadcasted_iota(jnp.int32, (SLOT_BLOCK, TOKEN_CHUNK), 0)
        acc_ref[...] = jnp.zeros_like(acc_ref)

        def body(pair, g, slot_ids=slot_ids):
            hits = [pos_ref[0, 0, pl.ds(2 * pair + c, 1), :] == slot_ids for c in range(2)]
            xc = x_ref[0, pl.ds(pl.multiple_of(pair * (2 * TOKEN_CHUNK), 2 * TOKEN_CHUNK), 2 * TOKEN_CHUNK), :]
            acc_ref[...] += jnp.dot(jnp.concatenate(hits, axis=1).astype(BF16), xc, preferred_element_type=F32)
            for c in range(2):
                g = g + jnp.sum(jnp.where(hits[c], aff_ref[0, 0, pl.ds(2 * pair + c, 1), :], 0.0), axis=-1, keepdims=True)
            return g

        g = lax.fori_loop(clo_ref[base + sb] >> 1, (chi_ref[base + sb] + 1) >> 1, body,
                          jnp.zeros((SLOT_BLOCK, 1), F32))
        rows = slice(sb * SLOT_BLOCK, (sb + 1) * SLOT_BLOCK)
        xin_ref[0, 0, rows, :] = acc_ref[...].astype(BF16)
        gs_ref[0, 0, rows, :] = jnp.broadcast_to(g, (SLOT_BLOCK, LANES))


def _gather(x1b, posm, aff, clo, chi, cap):
    batch, seq, d = x1b.shape
    n_e = posm.shape[1]
    nsb = cap // SLOT_BLOCK
    nch = seq // TOKEN_CHUNK
    pos4 = posm.reshape(batch, n_e, nch, TOKEN_CHUNK)
    aff4 = aff.reshape(batch, n_e, nch, TOKEN_CHUNK)
    grid_spec = pltpu.PrefetchScalarGridSpec(
        num_scalar_prefetch=2,
        grid=(batch, n_e),
        in_specs=[pl.BlockSpec((1, seq, d), lambda b, e, *_: (b, 0, 0)),
                  pl.BlockSpec((1, 1, nch, TOKEN_CHUNK), lambda b, e, *_: (b, e, 0, 0)),
                  pl.BlockSpec((1, 1, nch, TOKEN_CHUNK), lambda b, e, *_: (b, e, 0, 0))],
        out_specs=[pl.BlockSpec((1, 1, cap, d), lambda b, e, *_: (b, e, 0, 0)),
                   pl.BlockSpec((1, 1, cap, LANES), lambda b, e, *_: (b, e, 0, 0))],
        scratch_shapes=[pltpu.VMEM((SLOT_BLOCK, d), F32)],
    )
    return pl.pallas_call(
        functools.partial(_gather_kernel, n_e=n_e, nsb=nsb),
        grid_spec=grid_spec,
        out_shape=[jax.ShapeDtypeStruct((batch, n_e, cap, d), BF16),
                   jax.ShapeDtypeStruct((batch, n_e, cap, LANES), F32)],
        compiler_params=_cparams(("arbitrary", "arbitrary"), 56),
        name="ec_gather",
    )(clo, chi, x1b, pos4, aff4)


def _ffn_kernel(x_ref, gs_ref, wg_ref, wu_ref, wd_ref, o_ref, *, fchunk):
    x = x_ref[0, 0]
    ff = wg_ref.shape[3]
    acc = jnp.zeros((x.shape[0], wd_ref.shape[3]), F32)
    for f0 in range(0, ff, fchunk):
        g = jnp.dot(x, wg_ref[0, 0, :, f0:f0 + fchunk], preferred_element_type=F32)
        u = jnp.dot(x, wu_ref[0, 0, :, f0:f0 + fchunk], preferred_element_type=F32)
        hid = (g / (1.0 + jnp.exp(-g))) * u
        acc += jnp.dot(hid.astype(BF16), wd_ref[0, 0, f0:f0 + fchunk, :], preferred_element_type=F32)
    o_ref[0, 0] = (acc * gs_ref[0, 0][:, 0:1]).astype(o_ref.dtype)


def _expert_ffn(xin, gslot, wg, wu, wd, layer, tc, fchunk):
    batch, n_e, cap, d = xin.shape
    ff = wg.shape[3]
    return pl.pallas_call(
        functools.partial(_ffn_kernel, fchunk=fchunk),
        grid=(n_e, batch, cap // tc),
        in_specs=[pl.BlockSpec((1, 1, tc, d), lambda e, b, i: (b, e, i, 0)),
                  pl.BlockSpec((1, 1, tc, LANES), lambda e, b, i: (b, e, i, 0)),
                  pl.BlockSpec((1, 1, d, ff), lambda e, b, i: (layer, e, 0, 0)),
                  pl.BlockSpec((1, 1, d, ff), lambda e, b, i: (layer, e, 0, 0)),
                  pl.BlockSpec((1, 1, ff, d), lambda e, b, i: (layer, e, 0, 0))],
        out_specs=pl.BlockSpec((1, 1, tc, d), lambda e, b, i: (b, e, i, 0)),
        out_shape=jax.ShapeDtypeStruct((batch, n_e, cap, d), BF16),
        compiler_params=_cparams(("arbitrary", "arbitrary", "arbitrary"), 56),
        name="expert_ffn",
    )(xin, gslot, wg, wu, wd)


def _combine_kernel(cs_ref, yo_ref, post_ref, y_ref, *, n_e, nch, cap):
    b = pl.program_id(0)
    c = pl.program_id(2)
    win = 2 * SLOT_BLOCK
    pos_all = post_ref[0]
    win_ids = lax.broadcasted_iota(jnp.int32, (TOKEN_CHUNK, win), 1)
    hits, wins, tails = [], [], []
    for e in range(n_e):
        base = (b * n_e + e) * (nch + 1) + c
        s_lo = cs_ref[base]
        s_hi = cs_ref[base + 1]
        w0 = pl.multiple_of(jnp.minimum((s_lo >> SLOT_SHIFT) << SLOT_SHIFT, cap - win), SLOT_BLOCK)
        hits.append((pos_all[:, e:e + 1] == (w0 + win_ids)).astype(BF16))
        wins.append(yo_ref[0, e, pl.ds(w0, win), :])
        sb_end = jnp.where(s_hi > s_lo, ((s_hi - 1) >> SLOT_SHIFT) + 1, 0)
        tails.append(((w0 + win) >> SLOT_SHIFT, sb_end))
    y_ref[0] = jnp.dot(jnp.concatenate(hits, axis=1), jnp.concatenate(wins, axis=0), preferred_element_type=F32)
    for e, (sb_from, sb_end) in enumerate(tails):
        def body(sb, carry, e=e):
            blk_ids = lax.broadcasted_iota(jnp.int32, (TOKEN_CHUNK, SLOT_BLOCK), 1)
            hit = post_ref[0, :, e:e + 1] == (sb * SLOT_BLOCK + blk_ids)
            yb = yo_ref[0, e, pl.ds(pl.multiple_of(sb * SLOT_BLOCK, SLOT_BLOCK), SLOT_BLOCK), :]
            y_ref[0] += jnp.dot(hit.astype(BF16), yb, preferred_element_type=F32)
            return carry

        lax.fori_loop(sb_from, sb_end, body, 0)


def _combine(yo, pos_t, cs, td):
    batch, n_e, cap, d = yo.shape
    seq = pos_t.shape[1]
    nch = seq // TOKEN_CHUNK
    grid_spec = pltpu.PrefetchScalarGridSpec(
        num_scalar_prefetch=1,
        grid=(batch, d // td, nch),
        in_specs=[pl.BlockSpec((1, n_e, cap, td), lambda b, j, c, *_: (b, 0, 0, j)),
                  pl.BlockSpec((1, TOKEN_CHUNK, n_e), lambda b, j, c, *_: (b, c, 0))],
        out_specs=pl.BlockSpec((1, TOKEN_CHUNK, td), lambda b, j, c, *_: (b, c, j)),
    )
    return pl.pallas_call(
        functools.partial(_combine_kernel, n_e=n_e, nch=nch, cap=cap),
        grid_spec=grid_spec,
        out_shape=jax.ShapeDtypeStruct((batch, seq, d), F32),
        compiler_params=_cparams(("arbitrary", "arbitrary", "arbitrary"), 56),
        name="ec_combine",
    )(cs, yo, pos_t)


def _ple_kernel(x1_ref, y_ref, p_ref, wpg_ref, wpp_ref, ln_ref, x3_ref, x3b_ref, *, alpha):
    x2 = _layer_norm(alpha * x1_ref[...] + y_ref[...], ln_ref[0:1, :], ln_ref[1:2, :])
    gate = 1.0 / (1.0 + jnp.exp(-jnp.dot(x2.astype(BF16), wpg_ref[...], preferred_element_type=F32)))
    emb = jnp.dot(p_ref[...].astype(BF16), wpp_ref[...], preferred_element_type=F32) * gate
    x3 = _layer_norm(alpha * x2 + emb, ln_ref[2:3, :], ln_ref[3:4, :])
    x3_ref[...] = x3
    x3b_ref[...] = x3.astype(BF16)


def _ple(x1, y, p, w_pg, w_pp, ln, alpha, tm):
    n, d = x1.shape
    row = lambda w: pl.BlockSpec((tm, w), lambda i: (i, 0))
    full = lambda a: pl.BlockSpec(a.shape, lambda i: (0,) * a.ndim)
    return pl.pallas_call(
        functools.partial(_ple_kernel, alpha=alpha),
        grid=(n // tm,),
        in_specs=[row(d), row(d), row(p.shape[1]), full(w_pg), full(w_pp), full(ln)],
        out_specs=[row(d), row(d)],
        out_shape=[jax.ShapeDtypeStruct((n, d), F32), jax.ShapeDtypeStruct((n, d), BF16)],
        compiler_params=_cparams(("arbitrary",), 40),
        name="ple_norm",
    )(x1, y, p, w_pg, w_pp, ln)


def _token_mixer_inputs(xb, w_in, qk_gain, batch, seq):
    d = w_in.shape[0]
    scale = HEAD_DIM ** -0.5
    na_w = NA_HEADS * HEAD_DIM
    gq_w, gkv_w = GQA_HEADS * HEAD_DIM, GQA_KV_HEADS * HEAD_DIM
    dil_w = DIL_HEADS * HEAD_DIM
    dqk_w = DIFF_HEADS * 2 * HEAD_DIM
    o = 0
    w_a = w_in[:, o:o + 3 * na_w]; o += 3 * na_w
    w_b = w_in[:, o:o + gq_w + 2 * gkv_w]; o += gq_w + 2 * gkv_w
    w_c = w_in[:, o:o + 3 * dil_w]; o += 3 * dil_w
    w_d = w_in[:, o:o + 3 * dqk_w]; o += 3 * dqk_w
    w_g = w_in[:, o:]

    def qscaled(w, qw):
        return jnp.concatenate([w[:, :qw] * (scale * LOG2E), w[:, qw:]], axis=1).astype(BF16)

    n = batch * seq
    ha = _matmul(xb, qscaled(w_a, na_w), BF16, 1024, 3 * na_w, "proj_na").reshape(batch, seq, -1)
    w_cg = jnp.concatenate(
        [w_c[:, t * dil_w + LANES * g:t * dil_w + LANES * (g + 1)] * (scale if t == 0 else 1.0)
         for g in range(len(DIL_GROUPS)) for t in range(3)], axis=1).astype(BF16)
    hc = _matmul(xb, w_cg, BF16, 1024, 3 * dil_w, "proj_dil").reshape(batch, seq, -1)
    w_dq = jnp.concatenate([w_d[:, :dqk_w] * (scale * LOG2E), w_d[:, dqk_w:]], axis=1).astype(BF16)
    hd, hd_stats = _proj_diff(xb, w_dq, 1024)
    hd = hd.reshape(batch, seq, -1)
    qn, kd, vd = _proj_gqa(xb, w_b, qk_gain, seq, 512)
    shape3 = lambda t: t.reshape(batch, seq, t.shape[-1])
    return ha, shape3(qn), shape3(kd), shape3(vd), hc, hd, hd_stats, w_g.astype(BF16)


def _layer(x, xb, p, w, lam_init, alpha):
    batch, seq, d = x.shape
    n = batch * seq
    ha, qn, kd, vd, hc, hd, hd_stats, w_g = _token_mixer_inputs(xb.reshape(n, d), w["w_in"], w["qk_gain"], batch, seq)

    ya = _na_attention(ha, _na_bias_tiles(w["rpb"]), batch, seq)
    yb = lax.cond(_gqa_logit_bound(w["qk_gain"]) <= ATTN_LOGIT_LIMIT,
                  lambda *a: _gqa_attention(*a, ATTN_TQ, GQA_TK_BOUNDED, False),
                  lambda *a: _gqa_attention(*a, ATTN_TQ, ATTN_TK, True), qn, kd, vd)
    dil = []
    for g, (_, dilation) in enumerate(DIL_GROUPS):
        dil.append(_dilated_group(hc, w["dil_bias"][g], g, dilation, batch, seq, DIL_TQ, DIL_BLOCKS_PER_STEP))
    lp = w["lam_params"].astype(F32)
    lam = jnp.exp(jnp.sum(lp[0] * lp[1])) - jnp.exp(jnp.sum(lp[2] * lp[3])) + lam_init
    consts = jnp.zeros((8, LANES), F32).at[0].set(lam).at[1].set(w["diff_gain"]).at[2].set(1.0 - lam_init)
    yd = _diff_mixer(hd, hd_stats, w["t5"][:, DIL_HEADS:], w["diff_strips"], consts)

    flat = lambda t: t.reshape(n, t.shape[-1])
    w_r3 = jnp.stack(_split3(w["w_router"].T))
    ln = w["ln"]
    x1, x1b, logits_t = _merge(
        x.reshape(n, d), xb.reshape(n, d), flat(ya), flat(yb), [(flat(o), flat(l)) for o, l in dil], flat(yd), w_g,
        w["w_na"].astype(BF16), w["w_gqa"].astype(BF16), w["w_dil"].astype(BF16), w["w_diff"].astype(BF16),
        w["w_o"].astype(BF16), w_r3, jnp.stack([ln[0][0], ln[1][0]]), alpha, 512)

    cap = EC_CAPACITY * seq // N_EXPERTS
    aff, posm, posx = _route(logits_t, batch, seq, cap)
    nch = seq // TOKEN_CHUNK
    cs = jnp.concatenate([posx[:, :, ::TOKEN_CHUNK], jnp.full((batch, N_EXPERTS, 1), cap, jnp.int32)], axis=-1)
    edges = jnp.arange(cap // SLOT_BLOCK, dtype=jnp.int32) * SLOT_BLOCK
    clo = jnp.sum(cs[:, :, 1:, None] <= edges, axis=2).astype(jnp.int32)
    chi = jnp.sum(cs[:, :, :nch, None] < edges + SLOT_BLOCK, axis=2).astype(jnp.int32)
    xin, gslot = _gather(x1b.reshape(batch, seq, d), posm, aff, clo.reshape(-1), chi.reshape(-1), cap)
    yo = _expert_ffn(xin, gslot, w["w_eg"], w["w_eu"], w["w_ed"], w["layer"], 512, 512)
    y = _combine(yo, posm.transpose(0, 2, 1), cs.reshape(-1), 512)

    ln4 = jnp.stack([ln[0][1], ln[1][1], ln[0][2], ln[1][2]])
    x3, x3b = _ple(x1, y.reshape(n, d), p.reshape(n, -1), w["w_pg"].astype(BF16), w["w_pp"].astype(BF16), ln4, alpha, 512)
    return x3.reshape(batch, seq, d), x3b.reshape(batch, seq, d)


def kernel(x, p, w_in, w_branch_na, w_branch_gqa, w_branch_dil, w_branch_diff, w_out, na_rel_bias, qk_norm_gain,
           diff_lambda, diff_norm_gain, t5_rel_bias, w_router, w_expert_gate, w_expert_up, w_expert_down,
           w_ple_proj, w_ple_gate, ln_gain, ln_bias):
    depth = w_in.shape[0]
    alpha = (2 * depth) ** 0.25
    xb = x.astype(BF16)
    w_eg, w_eu, w_ed = (t.astype(BF16) for t in (w_expert_gate, w_expert_up, w_expert_down))
    diff_strips = _diff_bias_strips(t5_rel_bias[:, DIL_HEADS:], DIFF_BIAS_ROWS, DIFF_TK_BOUNDED)
    dil_bias = [_dil_bias_tiles(t5_rel_bias[:, 2 * g:2 * g + 2], dilation, DIL_TQ)
                for g, (_, dilation) in enumerate(DIL_GROUPS)]
    for i in range(depth):
        lam_init = 0.8 - 0.6 * math.exp(-0.3 * i)
        w = dict(diff_strips=diff_strips, dil_bias=dil_bias, w_in=w_in[i], w_na=w_branch_na[i], w_gqa=w_branch_gqa[i], w_dil=w_branch_dil[i],
                 w_diff=w_branch_diff[i], w_o=w_out[i], rpb=na_rel_bias[i], qk_gain=qk_norm_gain[i],
                 lam_params=diff_lambda[i], diff_gain=diff_norm_gain[i], t5=t5_rel_bias, w_router=w_router[i],
                 w_eg=w_eg, w_eu=w_eu, w_ed=w_ed, layer=i, w_pp=w_ple_proj[i],
                 w_pg=w_ple_gate[i], ln=(ln_gain[i], ln_bias[i]))
        x, xb = _layer(x, xb, p[i], w, lam_init, alpha)
    return x
```

```python
import functools
import math

import numpy as np
import jax
import jax.numpy as jnp
from jax import lax
from jax.experimental import pallas as pl
from jax.experimental.pallas import tpu as pltpu

F32 = jnp.float32
BF16 = jnp.bfloat16
NEG_INF = -1e30

V7X_VMEM_BYTES = 64 * 1024 * 1024
LANES = 128
LANE_SHIFT = 7

GRID_W = 64
HEAD_DIM = 64
NA_HEADS, NA_ROWS, NA_COLS = 6, 8, 16
NA_ROWS_PER_STEP = 4
LOG2E = 1.4426950408889634
ATTN_TQ, ATTN_TK = 1024, 256
GQA_TK_BOUNDED, DIFF_TK_BOUNDED = 1024, 1024
ATTN_LOGIT_LIMIT = 64.0
DIL_TQ = 128
DIL_BLOCKS_PER_STEP = 4
DIFF_BIAS_ROWS = 256
GQA_HEADS, GQA_KV_HEADS = 6, 2
ROPE_THETA = 10000.0
DIL_GROUPS = ((128, 1), (512, 4), (2048, 16))
DIL_HEADS = 6
DIFF_HEADS = 4
T5_BUCKETS, T5_MAX_DISTANCE = 32, 1024
N_EXPERTS, EC_CAPACITY = 16, 2
SLOT_BLOCK = 128
SLOT_SHIFT = SLOT_BLOCK.bit_length() - 1
TOKEN_CHUNK = 256


def _cparams(semantics, vmem_mib):
    return pltpu.CompilerParams(dimension_semantics=semantics,
                                vmem_limit_bytes=min(vmem_mib * 1024 * 1024, V7X_VMEM_BYTES - 4 * 1024 * 1024))


def _mm_kernel(a_ref, b_ref, o_ref):
    o_ref[...] = jnp.dot(a_ref[...], b_ref[...], preferred_element_type=F32).astype(o_ref.dtype)


def _matmul(a, b, out_dtype, tm, tn, name):
    m, k = a.shape
    n = b.shape[1]
    return pl.pallas_call(
        _mm_kernel,
        grid=(n // tn, m // tm),
        in_specs=[pl.BlockSpec((tm, k), lambda j, i: (i, 0)),
                  pl.BlockSpec((k, tn), lambda j, i: (0, j))],
        out_specs=pl.BlockSpec((tm, tn), lambda j, i: (i, j)),
        out_shape=jax.ShapeDtypeStruct((m, n), out_dtype),
        compiler_params=_cparams(("arbitrary", "arbitrary"), 40),
        name=name,
    )(a, b)


def _toeplitz(u, rows, width):
    lead = u.shape[:-1]
    period = rows + width
    text = jnp.concatenate([u[..., rows - 1:], jnp.zeros(lead + (1,), u.dtype), u[..., :rows - 1]], axis=-1)
    flat = jnp.tile(text, (1,) * len(lead) + (rows,))[..., :rows * (period - 1)]
    return flat.reshape(lead + (rows, period - 1))[..., :width]


def _lane_lo(rows):
    return lax.broadcasted_iota(jnp.int32, (rows, LANES), 1) < HEAD_DIM


def _dot_nt(a, b):
    return lax.dot_general(a, b, (((1,), (1,)), ((), ())), preferred_element_type=F32)


NA_WIN_ROWS = NA_ROWS + NA_ROWS_PER_STEP


def _na_kernel(q_ref, k_ref, v_ref, tp_ref, o_ref, *, rows):
    r0 = pl.program_id(1) * NA_ROWS_PER_STEP
    ws = jnp.clip(r0 - NA_ROWS // 2, 0, rows - NA_WIN_ROWS)
    start = pl.multiple_of(ws * GRID_W, GRID_W)
    win = NA_WIN_ROWS * GRID_W
    npair = NA_WIN_ROWS // 2
    none_code = 2 * (2 * NA_ROWS - 1)
    codes = []
    for rr in range(NA_ROWS_PER_STEP):
        r = r0 + rr
        rs = jnp.clip(r - NA_ROWS // 2, 0, rows - NA_ROWS)
        row_codes = []
        for j in range(npair):
            a0 = ws + 2 * j
            d0 = a0 - r + (NA_ROWS - 1)
            vl = jnp.logical_and(a0 >= rs, a0 < rs + NA_ROWS)
            vr = jnp.logical_and(a0 + 1 >= rs, a0 + 1 < rs + NA_ROWS)
            row_codes.append(jnp.where(jnp.logical_and(vl, vr), d0,
                                       jnp.where(jnp.logical_or(vl, vr), 2 * NA_ROWS - 1 + d0, none_code)))
        codes.append(row_codes)
    tq = NA_ROWS_PER_STEP * GRID_W
    lo = _lane_lo(tq)
    for p in range(NA_HEADS // 2):
        cols = slice(LANES * p, LANES * (p + 1))
        qp = q_ref[0, :, cols]
        kw = k_ref[0, pl.ds(start, win), cols]
        vw = v_ref[0, pl.ds(start, win), cols]
        outs = []
        for hh in range(2):
            h = 2 * p + hh
            qm = jnp.where(lo if hh == 0 else jnp.logical_not(lo), qp, jnp.zeros_like(qp))
            bias = jnp.concatenate(
                [jnp.concatenate([tp_ref[h, c] for c in row_codes], axis=-1) for row_codes in codes], axis=0)
            s = _dot_nt(qm, kw) + bias
            e = jnp.exp2(s - jnp.max(s, axis=-1, keepdims=True))
            pv = jnp.dot(e.astype(BF16), vw, preferred_element_type=F32)
            outs.append(pv / jnp.sum(e, axis=-1, keepdims=True))
        o_ref[0, :, cols] = jnp.where(lo, outs[0], outs[1]).astype(o_ref.dtype)


def _na_bias_tiles(rpb):
    col = jnp.arange(GRID_W)
    cstart = jnp.clip(col - NA_COLS // 2, 0, GRID_W - NA_COLS)
    kc = jnp.arange(GRID_W)
    inwin = (kc[None, :] >= cstart[:, None]) & (kc[None, :] < cstart[:, None] + NA_COLS)
    pad = GRID_W - NA_COLS
    u = jnp.pad(rpb.astype(F32) * LOG2E, ((0, 0), (0, 0), (pad, pad)))
    t = jnp.where(inwin[None, None], _toeplitz(u, GRID_W, GRID_W), NEG_INF)
    neg = jnp.full_like(t[:, :NA_ROWS], NEG_INF)
    both = jnp.concatenate([t[:, :-1], t[:, 1:]], axis=-1)
    right = jnp.concatenate([neg, t[:, :NA_ROWS]], axis=-1)
    left = jnp.concatenate([t[:, NA_ROWS - 1:], neg], axis=-1)
    none = jnp.concatenate([neg[:, :1], neg[:, :1]], axis=-1)
    return jnp.concatenate([both, right, left, none], axis=1)


def _na_attention(ha, t2, batch, seq):
    rows = seq // GRID_W
    w = NA_HEADS * HEAD_DIM
    tq = GRID_W * NA_ROWS_PER_STEP
    return pl.pallas_call(
        functools.partial(_na_kernel, rows=rows),
        grid=(batch, rows // NA_ROWS_PER_STEP),
        in_specs=[pl.BlockSpec((1, tq, w), lambda b, r: (b, r, 0)),
                  pl.BlockSpec((1, seq, w), lambda b, r: (b, 0, 1)),
                  pl.BlockSpec((1, seq, w), lambda b, r: (b, 0, 2)),
                  pl.BlockSpec(t2.shape, lambda b, r: (0, 0, 0, 0))],
        out_specs=pl.BlockSpec((1, tq, w), lambda b, r: (b, r, 0)),
        out_shape=jax.ShapeDtypeStruct((batch, seq, w), BF16),
        compiler_params=_cparams(("arbitrary", "arbitrary"), 56),
        name="na_attention",
    )(ha, ha, ha, t2)


def _softmax_stream_update(s, vb, m_ref, l_ref, acc_ref, idx):
    m_prev = m_ref[idx]
    m_new = jnp.maximum(m_prev, jnp.max(s, axis=-1, keepdims=True))
    alpha = jnp.exp2(m_prev - m_new)
    p = jnp.exp2(s - jnp.tile(m_new, (1, s.shape[1] // LANES)))
    m_ref[idx] = m_new
    l_ref[idx] = alpha * l_ref[idx] + jnp.sum(p, axis=-1, keepdims=True)
    acc_ref[idx] = alpha * acc_ref[idx] + jnp.dot(p.astype(BF16), vb, preferred_element_type=F32)


def _init_streams(m_ref, l_ref, acc_ref):
    m_ref[...] = jnp.full(m_ref.shape, NEG_INF, F32)
    l_ref[...] = jnp.zeros(l_ref.shape, F32)
    acc_ref[...] = jnp.zeros(acc_ref.shape, F32)


def _gqa_kernel(q_ref, k_ref, v_ref, o_ref, qm_ref, m_ref, l_ref, acc_ref, *, tk, nk, track_max):
    tq = q_ref.shape[1]
    lo = _lane_lo(tq)
    rep = GQA_HEADS // GQA_KV_HEADS
    for h in range(GQA_HEADS):
        qp = q_ref[0, :, LANES * (h // 2):LANES * (h // 2 + 1)]
        qm_ref[h] = jnp.where(lo if h % 2 == 0 else jnp.logical_not(lo), qp, jnp.zeros_like(qp))
    _init_streams(m_ref, l_ref, acc_ref)

    def body(j, carry):
        ks = pl.multiple_of(j * tk, tk)
        for g in range(GQA_KV_HEADS):
            gcols = slice(LANES * g, LANES * (g + 1))
            kb = k_ref[0, pl.ds(ks, tk), gcols]
            vb = v_ref[0, pl.ds(ks, tk), gcols]
            for h in range(rep * g, rep * (g + 1)):
                s = _dot_nt(qm_ref[h], kb)
                if track_max:
                    _softmax_stream_update(s, vb, m_ref, l_ref, acc_ref, h)
                else:
                    p = jnp.exp2(s)
                    l_ref[h] += jnp.sum(p, axis=-1, keepdims=True)
                    acc_ref[h] += jnp.dot(p.astype(BF16), vb, preferred_element_type=F32)
        return carry

    lax.fori_loop(0, nk, body, 0)
    for p in range(GQA_HEADS // 2):
        o0 = acc_ref[2 * p] / l_ref[2 * p]
        o1 = acc_ref[2 * p + 1] / l_ref[2 * p + 1]
        o_ref[0, :, LANES * p:LANES * (p + 1)] = jnp.where(lo, o0, o1).astype(o_ref.dtype)


def _gqa_attention(q, kd, vd, tq, tk, track_max):
    batch, seq, wq = q.shape
    wk = kd.shape[-1]
    state = pltpu.VMEM((GQA_HEADS, tq, LANES), F32)
    return pl.pallas_call(
        functools.partial(_gqa_kernel, tk=tk, nk=seq // tk, track_max=track_max),
        grid=(batch, seq // tq),
        in_specs=[pl.BlockSpec((1, tq, wq), lambda b, i: (b, i, 0)),
                  pl.BlockSpec((1, seq, wk), lambda b, i: (b, 0, 0)),
                  pl.BlockSpec((1, seq, wk), lambda b, i: (b, 0, 0))],
        out_specs=pl.BlockSpec((1, tq, wq), lambda b, i: (b, i, 0)),
        out_shape=jax.ShapeDtypeStruct((batch, seq, wq), BF16),
        scratch_shapes=[pltpu.VMEM((GQA_HEADS, tq, LANES), BF16), state, state, state],
        compiler_params=_cparams(("arbitrary", "arbitrary"), 48),
        name="gqa_attention" if track_max else "gqa_attention_bounded",
    )(q, kd, vd)


def _gqa_logit_bound(qk_gain):
    gq = jnp.max(jnp.abs(qk_gain[0])) * (HEAD_DIM ** -0.5 * LOG2E)
    return 1.05 * HEAD_DIM * gq * jnp.max(jnp.abs(qk_gain[1]))


def _rope_tables(seq):
    quarter = HEAD_DIM // 4
    freqs = ROPE_THETA ** (-jnp.arange(quarter, dtype=F32) / quarter)
    t = jnp.arange(seq)
    row = (t // GRID_W).astype(F32)
    col = (t % GRID_W).astype(F32)
    ang_r = row[:, None] * freqs[None, :]
    ang_c = col[:, None] * freqs[None, :]
    cos = jnp.concatenate([jnp.cos(ang_r)] * 2 + [jnp.cos(ang_c)] * 2, axis=-1)
    sin = jnp.concatenate([-jnp.sin(ang_r), jnp.sin(ang_r), -jnp.sin(ang_c), jnp.sin(ang_c)], axis=-1)
    return cos, sin


def _proj_gqa_kernel(x_ref, w_ref, cos_ref, sin_ref, gain_ref, mean_ref, q_ref, k_ref, v_ref):
    h = jnp.dot(x_ref[...], w_ref[...], preferred_element_type=F32)
    nq = q_ref.shape[1] // LANES
    nk = k_ref.shape[1] // LANES
    lane = lax.broadcasted_iota(jnp.int32, (h.shape[0], LANES), 1)
    first = (lane & (HEAD_DIM // 2 - 1)) < HEAD_DIM // 4
    for blk in range(nq + nk):
        t = h[:, LANES * blk:LANES * (blk + 1)]
        hi, mid, low = _split3(t * t)
        ms = (jnp.dot(hi, mean_ref[...], preferred_element_type=F32)
              + jnp.dot(mid, mean_ref[...], preferred_element_type=F32)
              + jnp.dot(low, mean_ref[...], preferred_element_type=F32))
        y = t * lax.rsqrt(ms + 1e-6) * gain_ref[blk:blk + 1, :]
        partner = jnp.where(first, pltpu.roll(y, LANES - HEAD_DIM // 4, 1), pltpu.roll(y, HEAD_DIM // 4, 1))
        out = (y * cos_ref[...] + partner * sin_ref[...]).astype(BF16)
        if blk < nq:
            q_ref[:, LANES * blk:LANES * (blk + 1)] = out
        else:
            k_ref[:, LANES * (blk - nq):LANES * (blk - nq + 1)] = out
    v_ref[...] = h[:, LANES * (nq + nk):].astype(BF16)


def _proj_gqa(xb, w_b, qk_gain, seq, tm):
    n, d = xb.shape
    gq, gkv = GQA_HEADS * HEAD_DIM, GQA_KV_HEADS * HEAD_DIM
    dup = lambda w: jnp.concatenate([w[:, i * HEAD_DIM:(i + 1) * HEAD_DIM] for i in range(GQA_KV_HEADS) for _ in range(2)], axis=1)
    w_all = jnp.concatenate([w_b[:, :gq], dup(w_b[:, gq:gq + gkv]), dup(w_b[:, gq + gkv:])], axis=1).astype(BF16)
    cos, sin = _rope_tables(seq)
    cos2, sin2 = jnp.tile(cos, (1, 2)), jnp.tile(sin, (1, 2))
    nq, nk = gq // LANES, 2 * gkv // LANES
    gain = jnp.concatenate([jnp.tile(qk_gain[0] * (HEAD_DIM ** -0.5 * LOG2E), (nq, 2)), jnp.tile(qk_gain[1], (nk, 2)),
                            jnp.zeros((8 - nq - nk, LANES), F32)], axis=0)
    head = jnp.arange(LANES) // HEAD_DIM
    mean_mat = jnp.where(head[:, None] == head[None, :], 1.0 / HEAD_DIM, 0.0).astype(BF16)
    nt = seq // tm
    row = lambda w: pl.BlockSpec((tm, w), lambda i: (i, 0))
    full = lambda a: pl.BlockSpec(a.shape, lambda i: (0,) * a.ndim)
    pos = pl.BlockSpec((tm, LANES), lambda i: (i % nt, 0))
    return pl.pallas_call(
        _proj_gqa_kernel,
        grid=(n // tm,),
        in_specs=[row(d), full(w_all), pos, pos, full(gain), full(mean_mat)],
        out_specs=[row(gq), row(2 * gkv), row(2 * gkv)],
        out_shape=[jax.ShapeDtypeStruct((n, gq), BF16), jax.ShapeDtypeStruct((n, 2 * gkv), BF16),
                   jax.ShapeDtypeStruct((n, 2 * gkv), BF16)],
        compiler_params=_cparams(("arbitrary",), 40),
        name="proj_gqa",
    )(xb, w_all, cos2, sin2, gain, mean_mat)


def _dil_kernel(q_ref, k_ref, v_ref, bias_ref, o_ref, lse_ref, *, sub_len, nblk, tq):
    half = DIL_GROUPS[0][0] // 2
    win = tq + 2 * half
    lo = _lane_lo(tq)
    for qq in range(q_ref.shape[1] // tq):
        i = pl.program_id(2) * (q_ref.shape[1] // tq) + qq
        start = pl.multiple_of(jnp.clip(i * tq - half, 0, sub_len - win), half)
        var = jnp.where(i == 0, 0, jnp.where(i == nblk - 1, 2, 1))
        rows = slice(qq * tq, (qq + 1) * tq)
        qp = q_ref[0, rows, :]
        kw = k_ref[0, pl.ds(start, win), :]
        vw = v_ref[0, pl.ds(start, win), :]
        outs, lses = [], []
        for hh in range(2):
            qm = jnp.where(lo if hh == 0 else jnp.logical_not(lo), qp, jnp.zeros_like(qp))
            s = _dot_nt(qm, kw) + bias_ref[hh, var]
            m = jnp.max(s, axis=-1, keepdims=True)
            e = jnp.exp(s - m)
            l = jnp.sum(e, axis=-1, keepdims=True)
            outs.append(jnp.dot(e.astype(BF16), vw, preferred_element_type=F32) / l)
            lses.append(jnp.broadcast_to(m + jnp.log(l), (tq, LANES)))
        o_ref[0, rows, :] = jnp.where(lo, outs[0], outs[1])
        lse_ref[0, rows, :] = jnp.where(lo, lses[0], lses[1])


def _t5_bucket(rel):
    half = T5_BUCKETS // 2
    exact = half // 2
    n = jnp.abs(rel)
    nf = jnp.maximum(n, 1).astype(F32)
    large = exact + (jnp.log(nf / exact) / math.log(T5_MAX_DISTANCE / exact) * (half - exact)).astype(jnp.int32)
    large = jnp.minimum(large, half - 1)
    return jnp.where(rel > 0, half, 0) + jnp.where(n < exact, n, large)


def _dil_bias_tiles(t5_cols, dilation, tq):
    half = DIL_GROUPS[0][0] // 2
    win = tq + 2 * half
    wide = win + 2 * half
    j = jnp.arange(tq - 1 + wide) - (tq - 1) - 2 * half
    vals = t5_cols.astype(F32)[_t5_bucket(j * dilation)]
    u = jnp.where((jnp.abs(j) <= half)[:, None], vals, NEG_INF).T
    a = _toeplitz(u, tq, wide)
    return jnp.stack([a[:, :, 2 * half - s:2 * half - s + win] for s in (0, half, 2 * half)], axis=1)


def _dilated_group(hc, bias, g, dilation, batch, seq, tq, qb):
    sub_len = seq // dilation
    nblk = sub_len // tq
    ts = tq * qb
    if dilation == 1:
        view, nb, g = hc, hc.shape[-1] // LANES, 3 * g
    else:
        view, nb, g = hc[..., 3 * LANES * g:3 * LANES * (g + 1)].reshape(batch, sub_len, dilation * 3 * LANES), 3, 0
    nq = 1
    out, lse = pl.pallas_call(
        functools.partial(_dil_kernel, sub_len=sub_len, nblk=nblk, tq=tq),
        grid=(batch, dilation, nblk // qb),
        in_specs=[pl.BlockSpec((1, ts, LANES), lambda b, r, i: (b, i, r * nb + g)),
                  pl.BlockSpec((1, sub_len, LANES), lambda b, r, i: (b, 0, r * nb + nq + g)),
                  pl.BlockSpec((1, sub_len, LANES), lambda b, r, i: (b, 0, r * nb + 2 * nq + g)),
                  pl.BlockSpec(bias.shape, lambda b, r, i: (0, 0, 0, 0))],
        out_specs=[pl.BlockSpec((1, ts, LANES), lambda b, r, i: (b, i, r)),
                   pl.BlockSpec((1, ts, LANES), lambda b, r, i: (b, i, r))],
        out_shape=[jax.ShapeDtypeStruct((batch, sub_len, dilation * LANES), F32)] * 2,
        compiler_params=_cparams(("arbitrary", "arbitrary", "arbitrary"), 32),
        name=f"dilated_attention_d{dilation}",
    )(view, view, view, bias)
    return out.reshape(batch, seq, LANES), lse.reshape(batch, seq, LANES)


def _diff_kernel(q_ref, k_ref, v_ref, strip_ref, c_ref, o_ref, qm_ref, m_ref, l_ref, acc_ref, *,
                 tk, nk, delta_lo, delta_hi, track_max):
    tq = q_ref.shape[1]
    i = pl.program_id(1)
    lo = _lane_lo(tq)
    for h in range(DIFF_HEADS):
        qp = q_ref[0, :, LANES * h:LANES * (h + 1)]
        qm_ref[2 * h] = jnp.where(lo, qp, jnp.zeros_like(qp))
        qm_ref[2 * h + 1] = jnp.where(lo, jnp.zeros_like(qp), qp)
    _init_streams(m_ref, l_ref, acc_ref)

    def body(j, carry):
        ks = pl.multiple_of(j * tk, tk)
        sub = strip_ref.shape[2]
        blk0 = [(jnp.clip(j * tk - i * tq - a * sub, delta_lo, delta_hi) - delta_lo) >> LANE_SHIFT
                for a in range(tq // sub)]
        for h in range(DIFF_HEADS):
            cols = slice(LANES * h, LANES * (h + 1))
            kb = k_ref[0, pl.ds(ks, tk), cols]
            vb = v_ref[0, pl.ds(ks, tk), cols]
            bias = jnp.concatenate(
                [jnp.concatenate([strip_ref[h, b0 + c] for c in range(tk // LANES)], axis=-1) for b0 in blk0], axis=0)
            for t in range(2):
                s = _dot_nt(qm_ref[2 * h + t], kb) + bias
                if track_max:
                    _softmax_stream_update(s, vb, m_ref, l_ref, acc_ref, 2 * h + t)
                else:
                    p = jnp.exp2(s)
                    l_ref[2 * h + t] += jnp.sum(p, axis=-1, keepdims=True)
                    acc_ref[2 * h + t] += jnp.dot(p.astype(BF16), vb, preferred_element_type=F32)
        return carry

    lax.fori_loop(0, nk, body, 0)
    lam = c_ref[0:1, :]
    for h in range(DIFF_HEADS):
        out = acc_ref[2 * h] / l_ref[2 * h] - lam * (acc_ref[2 * h + 1] / l_ref[2 * h + 1])
        y = out * lax.rsqrt(jnp.mean(out * out, axis=-1, keepdims=True) + 1e-6)
        o_ref[0, :, LANES * h:LANES * (h + 1)] = ((y * c_ref[1:2, :]) * c_ref[2:3, :]).astype(o_ref.dtype)


def _t5_saturation_distance():
    half = T5_BUCKETS // 2
    exact = half // 2
    n = np.arange(1, 4 * T5_MAX_DISTANCE, dtype=np.float32)
    large = exact + (np.log(n / exact) / math.log(T5_MAX_DISTANCE / exact) * (half - exact)).astype(np.int32)
    return int(np.argmax(large >= half - 1)) + 1 + 8


def _diff_bias_strips(t5_cols, tq, tk):
    unit = math.gcd(tq, tk)
    sat = _t5_saturation_distance()
    delta_lo = -unit * ((sat + tk - 1) // unit + 1)
    delta_hi = unit * ((sat + tq - 1) // unit + 1)
    width = delta_hi - delta_lo + tk
    rel = jnp.arange(tq - 1 + width) - (tq - 1) + delta_lo
    u = (t5_cols.astype(F32) * LOG2E)[_t5_bucket(rel)].T
    strip = _toeplitz(u, tq, width)
    nh = strip.shape[0]
    return strip.reshape(nh, tq, width // LANES, LANES).transpose(0, 2, 1, 3), delta_lo, delta_hi


def _diff_attention(hd, strips, consts, delta_lo, delta_hi, tq, tk, track_max):
    batch, seq = hd.shape[:2]
    w = DIFF_HEADS * LANES
    once = pl.Buffered(1)
    state = pltpu.VMEM((2 * DIFF_HEADS, tq, LANES), F32)
    return pl.pallas_call(
        functools.partial(_diff_kernel, tk=tk, nk=seq // tk, delta_lo=delta_lo, delta_hi=delta_hi,
                          track_max=track_max),
        grid=(batch, seq // tq),
        in_specs=[pl.BlockSpec((1, tq, w), lambda b, i: (b, i, 0)),
                  pl.BlockSpec((1, seq, w), lambda b, i: (b, 0, 1), pipeline_mode=once),
                  pl.BlockSpec((1, seq, w), lambda b, i: (b, 0, 2), pipeline_mode=once),
                  pl.BlockSpec(strips.shape, lambda b, i: (0, 0, 0, 0), pipeline_mode=once),
                  pl.BlockSpec(consts.shape, lambda b, i: (0, 0))],
        out_specs=pl.BlockSpec((1, tq, w), lambda b, i: (b, i, 0)),
        out_shape=jax.ShapeDtypeStruct((batch, seq, w), BF16),
        scratch_shapes=[pltpu.VMEM((2 * DIFF_HEADS, tq, LANES), BF16), state, state, state],
        compiler_params=_cparams(("arbitrary", "arbitrary"), 56),
        name="diff_attention" if track_max else "diff_attention_bounded",
    )(hd, hd, hd, strips, consts)


def _proj_diff_kernel(x_ref, w_ref, ind_ref, o_ref, st_ref):
    hb = jnp.dot(x_ref[...], w_ref[...], preferred_element_type=F32).astype(BF16)
    o_ref[...] = hb
    qk = hb[:, :ind_ref.shape[0]].astype(F32)
    norms = jnp.dot((qk * qk).astype(BF16), ind_ref[...], preferred_element_type=F32)
    st_ref[0] = jnp.broadcast_to(jnp.max(norms, axis=0, keepdims=True), st_ref.shape[1:])


def _proj_diff(xb, w_dq, tm):
    n, d = xb.shape
    width = w_dq.shape[1]
    nqk = 2 * DIFF_HEADS * 2 * HEAD_DIM
    ind = (jnp.arange(nqk)[:, None] // HEAD_DIM == jnp.arange(LANES)[None, :]).astype(BF16)
    return pl.pallas_call(
        _proj_diff_kernel,
        grid=(n // tm,),
        in_specs=[pl.BlockSpec((tm, d), lambda i: (i, 0)), pl.BlockSpec(w_dq.shape, lambda i: (0, 0)),
                  pl.BlockSpec(ind.shape, lambda i: (0, 0))],
        out_specs=[pl.BlockSpec((tm, width), lambda i: (i, 0)), pl.BlockSpec((1, 8, LANES), lambda i: (i, 0, 0))],
        out_shape=[jax.ShapeDtypeStruct((n, width), BF16), jax.ShapeDtypeStruct((n // tm, 8, LANES), F32)],
        compiler_params=_cparams(("arbitrary",), 40),
        name="proj_diff",
    )(xb, w_dq, ind)


def _diff_logit_bound(norm_stats, t5_cols):
    n = DIFF_HEADS * 2
    mx = jnp.max(norm_stats, axis=(0, 1))
    return 1.05 * (jnp.sqrt(jnp.max(mx[:n] * mx[n:2 * n])) + LOG2E * jnp.max(jnp.abs(t5_cols)))


def _diff_mixer(hd, norm_stats, t5_cols, bounded_strips, consts):
    strips_b, lo_b, hi_b = bounded_strips

    def bounded(hd_, t5_, strips_, consts_):
        return _diff_attention(hd_, strips_, consts_, lo_b, hi_b, ATTN_TQ, DIFF_TK_BOUNDED, False)

    def tracked(hd_, t5_, strips_, consts_):
        strips, lo, hi = _diff_bias_strips(t5_, DIFF_BIAS_ROWS, ATTN_TK)
        return _diff_attention(hd_, strips, consts_, lo, hi, ATTN_TQ, ATTN_TK, True)

    return lax.cond(_diff_logit_bound(norm_stats, t5_cols) <= ATTN_LOGIT_LIMIT, bounded, tracked,
                    hd, t5_cols, strips_b, consts)


def _layer_norm(z, g, b):
    mu = jnp.mean(z, axis=-1, keepdims=True)
    zc = z - mu
    var = jnp.mean(zc * zc, axis=-1, keepdims=True)
    return zc * lax.rsqrt(var + 1e-5) * g + b


def _split3(a):
    hi = a.astype(BF16)
    r1 = a - hi.astype(F32)
    mid = r1.astype(BF16)
    lo = (r1 - mid.astype(F32)).astype(BF16)
    return hi, mid, lo


def _merge_kernel(x_ref, xb_ref, ya_ref, yb_ref, o0_ref, o1_ref, o2_ref, l0_ref, l1_ref, l2_ref, yd_ref,
                  wg_ref, wna_ref, wgqa_ref, wdil_ref, wdiff_ref, wo_ref, wr_ref, ln_ref,
                  x1_ref, x1b_ref, lg_ref, *, alpha, d_model):
    xb = xb_ref[...]

    def gate(branch):
        g = jnp.dot(xb, wg_ref[:, branch * d_model:(branch + 1) * d_model], preferred_element_type=F32)
        return 1.0 / (1.0 + jnp.exp(-g))

    l0, l1, l2 = l0_ref[...], l1_ref[...], l2_ref[...]
    mx = jnp.maximum(jnp.maximum(l0, l1), l2)
    e0, e1, e2 = jnp.exp(l0 - mx), jnp.exp(l1 - mx), jnp.exp(l2 - mx)
    den = e0 + e1 + e2
    yc = (e0 / den) * o0_ref[...] + (e1 / den) * o1_ref[...] + (e2 / den) * o2_ref[...]
    merged = gate(0) * jnp.dot(ya_ref[...], wna_ref[...], preferred_element_type=F32)
    merged += gate(1) * jnp.dot(yb_ref[...], wgqa_ref[...], preferred_element_type=F32)
    merged += gate(2) * jnp.dot(yc.astype(BF16), wdil_ref[...], preferred_element_type=F32)
    merged += gate(3) * jnp.dot(yd_ref[...], wdiff_ref[...], preferred_element_type=F32)
    y = jnp.dot(merged.astype(BF16), wo_ref[...], preferred_element_type=F32)
    x1 = _layer_norm(alpha * x_ref[...] + y, ln_ref[0:1, :], ln_ref[1:2, :])
    x1_ref[...] = x1
    x1b_ref[...] = x1.astype(BF16)
    xh, xm, _ = _split3(x1)
    wh, wm, _ = wr_ref[0], wr_ref[1], wr_ref[2]
    lg_ref[...] = _dot_nt(wh, xh) + (_dot_nt(wh, xm) + _dot_nt(wm, xh))


def _merge(x, xb, ya, yb, dil, yd, w_g, w_na, w_gqa, w_dil, w_diff, w_o, w_r3, ln, alpha, tm):
    n, d = x.shape
    e = w_r3.shape[1]
    (o0, l0), (o1, l1), (o2, l2) = dil
    row = lambda w: pl.BlockSpec((tm, w), lambda i: (i, 0))
    full = lambda a: pl.BlockSpec(a.shape, lambda i: (0,) * a.ndim)
    return pl.pallas_call(
        functools.partial(_merge_kernel, alpha=alpha, d_model=d),
        grid=(n // tm,),
        in_specs=[row(d), row(d), row(ya.shape[1]), row(yb.shape[1]), row(LANES), row(LANES), row(LANES),
                  row(LANES), row(LANES), row(LANES), row(yd.shape[1]),
                  full(w_g), full(w_na), full(w_gqa), full(w_dil), full(w_diff), full(w_o), full(w_r3), full(ln)],
        out_specs=[row(d), row(d), pl.BlockSpec((e, tm), lambda i: (0, i))],
        out_shape=[jax.ShapeDtypeStruct((n, d), F32), jax.ShapeDtypeStruct((n, d), BF16),
                   jax.ShapeDtypeStruct((e, n), F32)],
        compiler_params=_cparams(("arbitrary",), 56),
        name="branch_merge",
    )(x, xb, ya, yb, o0, o1, o2, l0, l1, l2, yd, w_g, w_na, w_gqa, w_dil, w_diff, w_o, w_r3, ln)


def _route_kernel(lg_ref, aff_ref, posm_ref, posx_ref, *, cap):
    lg = lg_ref[0]
    n_e, seq = lg.shape
    mx = jnp.max(lg, axis=0, keepdims=True)
    ex = jnp.exp(lg - mx)
    aff = ex / jnp.sum(ex, axis=0, keepdims=True)
    aff_ref[0] = aff
    bits = pltpu.bitcast(aff, jnp.int32)

    def bisect(_, carry):
        lo, hi = carry
        mid = lo + ((hi - lo + 1) >> 1)
        cnt = jnp.sum((bits >= mid).astype(F32), axis=1, keepdims=True)
        ok = cnt >= float(cap)
        return jnp.where(ok, mid, lo), jnp.where(ok, hi, mid - 1)

    lo0 = jnp.zeros((n_e, 1), jnp.int32)
    hi0 = jnp.full((n_e, 1), 0x7F800000, jnp.int32)
    thr, _ = lax.fori_loop(0, 31, bisect, (lo0, hi0))
    gt = bits > thr
    eq = bits == thr
    needf = float(cap) - jnp.sum(gt.astype(F32), axis=1, keepdims=True)

    tri = (lax.broadcasted_iota(jnp.int32, (LANES, LANES), 0)
           < lax.broadcasted_iota(jnp.int32, (LANES, LANES), 1)).astype(BF16)

    def prefix(mask_bf16, j, carry):
        blk = mask_bf16[:, j * LANES:(j + 1) * LANES]
        excl = jnp.dot(blk, tri, preferred_element_type=F32) + carry
        return excl, carry + jnp.sum(blk.astype(F32), axis=1, keepdims=True)

    eqb = eq.astype(BF16)
    carry_eq = jnp.zeros((n_e, 1), F32)
    carry_sel = jnp.zeros((n_e, 1), F32)
    for j in range(seq // LANES):
        cols = slice(j * LANES, (j + 1) * LANES)
        rank, carry_eq = prefix(eqb, j, carry_eq)
        sel = jnp.logical_or(gt[:, cols], jnp.logical_and(eq[:, cols], rank < needf))
        selb = sel.astype(BF16)
        excl = jnp.dot(selb, tri, preferred_element_type=F32) + carry_sel
        carry_sel = carry_sel + jnp.sum(selb.astype(F32), axis=1, keepdims=True)
        pos = excl.astype(jnp.int32)
        posx_ref[0, :, cols] = pos
        posm_ref[0, :, cols] = jnp.where(sel, pos, -1)


def _route(logits_t, batch, seq, cap):
    n_e = logits_t.shape[0]
    lg = logits_t.reshape(n_e, batch, seq).transpose(1, 0, 2)
    spec = pl.BlockSpec((1, n_e, seq), lambda b: (b, 0, 0))
    return pl.pallas_call(
        functools.partial(_route_kernel, cap=cap),
        grid=(batch,),
        in_specs=[spec],
        out_specs=[spec, spec, spec],
        out_shape=[jax.ShapeDtypeStruct((batch, n_e, seq), F32), jax.ShapeDtypeStruct((batch, n_e, seq), jnp.int32),
                   jax.ShapeDtypeStruct((batch, n_e, seq), jnp.int32)],
        compiler_params=_cparams(("arbitrary",), 32),
        name="ec_route",
    )(lg)


def _gather_kernel(clo_ref, chi_ref, x_ref, pos_ref, aff_ref, xin_ref, gs_ref, acc_ref, *, n_e, nsb):
    base = (pl.program_id(0) * n_e + pl.program_id(1)) * nsb
    for sb in range(nsb):
        slot_ids = sb * SLOT_BLOCK + lax.broadcasted_iota(jnp.int32, (SLOT_BLOCK, TOKEN_CHUNK), 0)
        acc_ref[...] = jnp.zeros_like(acc_ref)

        def body(pair, g, slot_ids=slot_ids):
            hits = [pos_ref[0, 0, pl.ds(2 * pair + c, 1), :] == slot_ids for c in range(2)]
            xc = x_ref[0, pl.ds(pl.multiple_of(pair * (2 * TOKEN_CHUNK), 2 * TOKEN_CHUNK), 2 * TOKEN_CHUNK), :]
            acc_ref[...] += jnp.dot(jnp.concatenate(hits, axis=1).astype(BF16), xc, preferred_element_type=F32)
            for c in range(2):
                g = g + jnp.sum(jnp.where(hits[c], aff_ref[0, 0, pl.ds(2 * pair + c, 1), :], 0.0), axis=-1, keepdims=True)
            return g

        g = lax.fori_loop(clo_ref[base + sb] >> 1, (chi_ref[base + sb] + 1) >> 1, body,
                          jnp.zeros((SLOT_BLOCK, 1), F32))
        rows = slice(sb * SLOT_BLOCK, (sb + 1) * SLOT_BLOCK)
        xin_ref[0, 0, rows, :] = acc_ref[...].astype(BF16)
        gs_ref[0, 0, rows, :] = jnp.broadcast_to(g, (SLOT_BLOCK, LANES))


def _gather(x1b, posm, aff, clo, chi, cap):
    batch, seq, d = x1b.shape
    n_e = posm.shape[1]
    nsb = cap // SLOT_BLOCK
    nch = seq // TOKEN_CHUNK
    pos4 = posm.reshape(batch, n_e, nch, TOKEN_CHUNK)
    aff4 = aff.reshape(batch, n_e, nch, TOKEN_CHUNK)
    grid_spec = pltpu.PrefetchScalarGridSpec(
        num_scalar_prefetch=2,
        grid=(batch, n_e),
        in_specs=[pl.BlockSpec((1, seq, d), lambda b, e, *_: (b, 0, 0)),
                  pl.BlockSpec((1, 1, nch, TOKEN_CHUNK), lambda b, e, *_: (b, e, 0, 0)),
                  pl.BlockSpec((1, 1, nch, TOKEN_CHUNK), lambda b, e, *_: (b, e, 0, 0))],
        out_specs=[pl.BlockSpec((1, 1, cap, d), lambda b, e, *_: (b, e, 0, 0)),
                   pl.BlockSpec((1, 1, cap, LANES), lambda b, e, *_: (b, e, 0, 0))],
        scratch_shapes=[pltpu.VMEM((SLOT_BLOCK, d), F32)],
    )
    return pl.pallas_call(
        functools.partial(_gather_kernel, n_e=n_e, nsb=nsb),
        grid_spec=grid_spec,
        out_shape=[jax.ShapeDtypeStruct((batch, n_e, cap, d), BF16),
                   jax.ShapeDtypeStruct((batch, n_e, cap, LANES), F32)],
        compiler_params=_cparams(("arbitrary", "arbitrary"), 56),
        name="ec_gather",
    )(clo, chi, x1b, pos4, aff4)


def _ffn_kernel(x_ref, gs_ref, wg_ref, wu_ref, wd_ref, o_ref, *, fchunk):
    x = x_ref[0, 0]
    ff = wg_ref.shape[3]
    acc = jnp.zeros((x.shape[0], wd_ref.shape[3]), F32)
    for f0 in range(0, ff, fchunk):
        g = jnp.dot(x, wg_ref[0, 0, :, f0:f0 + fchunk], preferred_element_type=F32)
        u = jnp.dot(x, wu_ref[0, 0, :, f0:f0 + fchunk], preferred_element_type=F32)
        hid = (g / (1.0 + jnp.exp(-g))) * u
        acc += jnp.dot(hid.astype(BF16), wd_ref[0, 0, f0:f0 + fchunk, :], preferred_element_type=F32)
    o_ref[0, 0] = (acc * gs_ref[0, 0][:, 0:1]).astype(o_ref.dtype)


def _expert_ffn(xin, gslot, wg, wu, wd, layer, tc, fchunk):
    batch, n_e, cap, d = xin.shape
    ff = wg.shape[3]
    return pl.pallas_call(
        functools.partial(_ffn_kernel, fchunk=fchunk),
        grid=(n_e, batch, cap // tc),
        in_specs=[pl.BlockSpec((1, 1, tc, d), lambda e, b, i: (b, e, i, 0)),
                  pl.BlockSpec((1, 1, tc, LANES), lambda e, b, i: (b, e, i, 0)),
                  pl.BlockSpec((1, 1, d, ff), lambda e, b, i: (layer, e, 0, 0)),
                  pl.BlockSpec((1, 1, d, ff), lambda e, b, i: (layer, e, 0, 0)),
                  pl.BlockSpec((1, 1, ff, d), lambda e, b, i: (layer, e, 0, 0))],
        out_specs=pl.BlockSpec((1, 1, tc, d), lambda e, b, i: (b, e, i, 0)),
        out_shape=jax.ShapeDtypeStruct((batch, n_e, cap, d), BF16),
        compiler_params=_cparams(("arbitrary", "arbitrary", "arbitrary"), 56),
        name="expert_ffn",
    )(xin, gslot, wg, wu, wd)


def _combine_kernel(cs_ref, yo_ref, post_ref, y_ref, *, n_e, nch, cap):
    b = pl.program_id(0)
    c = pl.program_id(2)
    win = 2 * SLOT_BLOCK
    pos_all = post_ref[0]
    win_ids = lax.broadcasted_iota(jnp.int32, (TOKEN_CHUNK, win), 1)
    hits, wins, tails = [], [], []
    for e in range(n_e):
        base = (b * n_e + e) * (nch + 1) + c
        s_lo = cs_ref[base]
        s_hi = cs_ref[base + 1]
        w0 = pl.multiple_of(jnp.minimum((s_lo >> SLOT_SHIFT) << SLOT_SHIFT, cap - win), SLOT_BLOCK)
        hits.append((pos_all[:, e:e + 1] == (w0 + win_ids)).astype(BF16))
        wins.append(yo_ref[0, e, pl.ds(w0, win), :])
        sb_end = jnp.where(s_hi > s_lo, ((s_hi - 1) >> SLOT_SHIFT) + 1, 0)
        tails.append(((w0 + win) >> SLOT_SHIFT, sb_end))
    y_ref[0] = jnp.dot(jnp.concatenate(hits, axis=1), jnp.concatenate(wins, axis=0), preferred_element_type=F32)
    for e, (sb_from, sb_end) in enumerate(tails):
        def body(sb, carry, e=e):
            blk_ids = lax.broadcasted_iota(jnp.int32, (TOKEN_CHUNK, SLOT_BLOCK), 1)
            hit = post_ref[0, :, e:e + 1] == (sb * SLOT_BLOCK + blk_ids)
            yb = yo_ref[0, e, pl.ds(pl.multiple_of(sb * SLOT_BLOCK, SLOT_BLOCK), SLOT_BLOCK), :]
            y_ref[0] += jnp.dot(hit.astype(BF16), yb, preferred_element_type=F32)
            return carry

        lax.fori_loop(sb_from, sb_end, body, 0)


def _combine(yo, pos_t, cs, td):
    batch, n_e, cap, d = yo.shape
    seq = pos_t.shape[1]
    nch = seq // TOKEN_CHUNK
    grid_spec = pltpu.PrefetchScalarGridSpec(
        num_scalar_prefetch=1,
        grid=(batch, d // td, nch),
        in_specs=[pl.BlockSpec((1, n_e, cap, td), lambda b, j, c, *_: (b, 0, 0, j), pipeline_mode=pl.Buffered(1)),
                  pl.BlockSpec((1, TOKEN_CHUNK, n_e), lambda b, j, c, *_: (b, c, 0))],
        out_specs=pl.BlockSpec((1, TOKEN_CHUNK, td), lambda b, j, c, *_: (b, c, j)),
    )
    return pl.pallas_call(
        functools.partial(_combine_kernel, n_e=n_e, nch=nch, cap=cap),
        grid_spec=grid_spec,
        out_shape=jax.ShapeDtypeStruct((batch, seq, d), F32),
        compiler_params=_cparams(("arbitrary", "arbitrary", "arbitrary"), 56),
        name="ec_combine",
    )(cs, yo, pos_t)


def _ple_kernel(x1_ref, y_ref, p_ref, wpg_ref, wpp_ref, ln_ref, x3_ref, x3b_ref, *, alpha):
    x2 = _layer_norm(alpha * x1_ref[...] + y_ref[...], ln_ref[0:1, :], ln_ref[1:2, :])
    gate = 1.0 / (1.0 + jnp.exp(-jnp.dot(x2.astype(BF16), wpg_ref[...], preferred_element_type=F32)))
    emb = jnp.dot(p_ref[...].astype(BF16), wpp_ref[...], preferred_element_type=F32) * gate
    x3 = _layer_norm(alpha * x2 + emb, ln_ref[2:3, :], ln_ref[3:4, :])
    x3_ref[...] = x3
    x3b_ref[...] = x3.astype(BF16)


def _ple(x1, y, p, w_pg, w_pp, ln, alpha, tm):
    n, d = x1.shape
    row = lambda w: pl.BlockSpec((tm, w), lambda i: (i, 0))
    full = lambda a: pl.BlockSpec(a.shape, lambda i: (0,) * a.ndim)
    return pl.pallas_call(
        functools.partial(_ple_kernel, alpha=alpha),
        grid=(n // tm,),
        in_specs=[row(d), row(d), row(p.shape[1]), full(w_pg), full(w_pp), full(ln)],
        out_specs=[row(d), row(d)],
        out_shape=[jax.ShapeDtypeStruct((n, d), F32), jax.ShapeDtypeStruct((n, d), BF16)],
        compiler_params=_cparams(("arbitrary",), 40),
        name="ple_norm",
    )(x1, y, p, w_pg, w_pp, ln)


def _token_mixer_inputs(xb, w_in, qk_gain, batch, seq):
    d = w_in.shape[0]
    scale = HEAD_DIM ** -0.5
    na_w = NA_HEADS * HEAD_DIM
    gq_w, gkv_w = GQA_HEADS * HEAD_DIM, GQA_KV_HEADS * HEAD_DIM
    dil_w = DIL_HEADS * HEAD_DIM
    dqk_w = DIFF_HEADS * 2 * HEAD_DIM
    o = 0
    w_a = w_in[:, o:o + 3 * na_w]; o += 3 * na_w
    w_b = w_in[:, o:o + gq_w + 2 * gkv_w]; o += gq_w + 2 * gkv_w
    w_c = w_in[:, o:o + 3 * dil_w]; o += 3 * dil_w
    w_d = w_in[:, o:o + 3 * dqk_w]; o += 3 * dqk_w
    w_g = w_in[:, o:]

    def qscaled(w, qw):
        return jnp.concatenate([w[:, :qw] * (scale * LOG2E), w[:, qw:]], axis=1).astype(BF16)

    n = batch * seq
    ha = _matmul(xb, qscaled(w_a, na_w), BF16, 1024, 3 * na_w, "proj_na").reshape(batch, seq, -1)
    w_cg = jnp.concatenate(
        [w_c[:, t * dil_w + LANES * g:t * dil_w + LANES * (g + 1)] * (scale if t == 0 else 1.0)
         for g in range(len(DIL_GROUPS)) for t in range(3)], axis=1).astype(BF16)
    hc = _matmul(xb, w_cg, BF16, 1024, 3 * dil_w, "proj_dil").reshape(batch, seq, -1)
    w_dq = jnp.concatenate([w_d[:, :dqk_w] * (scale * LOG2E), w_d[:, dqk_w:]], axis=1).astype(BF16)
    hd, hd_stats = _proj_diff(xb, w_dq, 1024)
    hd = hd.reshape(batch, seq, -1)
    qn, kd, vd = _proj_gqa(xb, w_b, qk_gain, seq, 512)
    shape3 = lambda t: t.reshape(batch, seq, t.shape[-1])
    return ha, shape3(qn), shape3(kd), shape3(vd), hc, hd, hd_stats, w_g.astype(BF16)


def _layer(x, xb, p, w, lam_init, alpha):
    batch, seq, d = x.shape
    n = batch * seq
    ha, qn, kd, vd, hc, hd, hd_stats, w_g = _token_mixer_inputs(xb.reshape(n, d), w["w_in"], w["qk_gain"], batch, seq)

    ya = _na_attention(ha, _na_bias_tiles(w["rpb"]), batch, seq)
    yb = lax.cond(_gqa_logit_bound(w["qk_gain"]) <= ATTN_LOGIT_LIMIT,
                  lambda *a: _gqa_attention(*a, ATTN_TQ, GQA_TK_BOUNDED, False),
                  lambda *a: _gqa_attention(*a, ATTN_TQ, ATTN_TK, True), qn, kd, vd)
    dil = []
    for g, (_, dilation) in enumerate(DIL_GROUPS):
        dil.append(_dilated_group(hc, w["dil_bias"][g], g, dilation, batch, seq, DIL_TQ, DIL_BLOCKS_PER_STEP))
    lp = w["lam_params"].astype(F32)
    lam = jnp.exp(jnp.sum(lp[0] * lp[1])) - jnp.exp(jnp.sum(lp[2] * lp[3])) + lam_init
    consts = jnp.zeros((8, LANES), F32).at[0].set(lam).at[1].set(w["diff_gain"]).at[2].set(1.0 - lam_init)
    yd = _diff_mixer(hd, hd_stats, w["t5"][:, DIL_HEADS:], w["diff_strips"], consts)

    flat = lambda t: t.reshape(n, t.shape[-1])
    w_r3 = jnp.stack(_split3(w["w_router"].T))
    ln = w["ln"]
    x1, x1b, logits_t = _merge(
        x.reshape(n, d), xb.reshape(n, d), flat(ya), flat(yb), [(flat(o), flat(l)) for o, l in dil], flat(yd), w_g,
        w["w_na"].astype(BF16), w["w_gqa"].astype(BF16), w["w_dil"].astype(BF16), w["w_diff"].astype(BF16),
        w["w_o"].astype(BF16), w_r3, jnp.stack([ln[0][0], ln[1][0]]), alpha, 512)

    cap = EC_CAPACITY * seq // N_EXPERTS
    aff, posm, posx = _route(logits_t, batch, seq, cap)
    nch = seq // TOKEN_CHUNK
    cs = jnp.concatenate([posx[:, :, ::TOKEN_CHUNK], jnp.full((batch, N_EXPERTS, 1), cap, jnp.int32)], axis=-1)
    edges = jnp.arange(cap // SLOT_BLOCK, dtype=jnp.int32) * SLOT_BLOCK
    clo = jnp.sum(cs[:, :, 1:, None] <= edges, axis=2).astype(jnp.int32)
    chi = jnp.sum(cs[:, :, :nch, None] < edges + SLOT_BLOCK, axis=2).astype(jnp.int32)
    xin, gslot = _gather(x1b.reshape(batch, seq, d), posm, aff, clo.reshape(-1), chi.reshape(-1), cap)
    yo = _expert_ffn(xin, gslot, w["w_eg"], w["w_eu"], w["w_ed"], w["layer"], 512, 512)
    y = _combine(yo, posm.transpose(0, 2, 1), cs.reshape(-1), d)

    ln4 = jnp.stack([ln[0][1], ln[1][1], ln[0][2], ln[1][2]])
    x3, x3b = _ple(x1, y.reshape(n, d), p.reshape(n, -1), w["w_pg"].astype(BF16), w["w_pp"].astype(BF16), ln4, alpha, 512)
    return x3.reshape(batch, seq, d), x3b.reshape(batch, seq, d)


def kernel(x, p, w_in, w_branch_na, w_branch_gqa, w_branch_dil, w_branch_diff, w_out, na_rel_bias, qk_norm_gain,
           diff_lambda, diff_norm_gain, t5_rel_bias, w_router, w_expert_gate, w_expert_up, w_expert_down,
           w_ple_proj, w_ple_gate, ln_gain, ln_bias):
    depth = w_in.shape[0]
    alpha = (2 * depth) ** 0.25
    xb = x.astype(BF16)
    w_eg, w_eu, w_ed = (t.astype(BF16) for t in (w_expert_gate, w_expert_up, w_expert_down))
    diff_strips = _diff_bias_strips(t5_rel_bias[:, DIL_HEADS:], DIFF_BIAS_ROWS, DIFF_TK_BOUNDED)
    dil_bias = [_dil_bias_tiles(t5_rel_bias[:, 2 * g:2 * g + 2], dilation, DIL_TQ)
                for g, (_, dilation) in enumerate(DIL_GROUPS)]
    for i in range(depth):
        lam_init = 0.8 - 0.6 * math.exp(-0.3 * i)
        w = dict(diff_strips=diff_strips, dil_bias=dil_bias, w_in=w_in[i], w_na=w_branch_na[i], w_gqa=w_branch_gqa[i], w_dil=w_branch_dil[i],
                 w_diff=w_branch_diff[i], w_o=w_out[i], rpb=na_rel_bias[i], qk_gain=qk_norm_gain[i],
                 lam_params=diff_lambda[i], diff_gain=diff_norm_gain[i], t5=t5_rel_bias, w_router=w_router[i],
                 w_eg=w_eg, w_eu=w_eu, w_ed=w_ed, layer=i, w_pp=w_ple_proj[i],
                 w_pg=w_ple_gate[i], ln=(ln_gain[i], ln_bias[i]))
        x, xb = _layer(x, xb, p[i], w, lam_init, alpha)
    return x
```
